```python
import math
import jax
import jax.numpy as jnp
from jax import lax
import numpy as np

D_MODEL = 2048
BATCH = 4
SEQ = 2048
DEPTH = 1
DEC_BATCH = 128
DEC_SEQ = 8
PAST_LEN = 2048
PAGE_SIZE = 128

A_HEADS = 8
A_KV_GROUPS = 2
A_HPG = A_HEADS // A_KV_GROUPS
A_HEAD_DIM = D_MODEL // 16
A_WIDTH = A_HEADS * A_HEAD_DIM
A_KV_WIDTH = A_KV_GROUPS * A_HEAD_DIM
CMP_BLOCK = 32
CMP_STRIDE = 16
SLC_BLOCK = 64
N_SELECT = 16
WINDOW = 512
QUERY_BLOCK = 128
B_HEADS = 8
B_KEY_DIM = D_MODEL // 16
B_VAL_DIM = D_MODEL // 16
B_WIDTH = B_HEADS * B_VAL_DIM
HGRN_CHUNK = 64
MIX_WIDTH = A_WIDTH + B_WIDTH
REL_BUCKETS = 32
REL_MAX_DIST = 128
EPS = 1e-6
IN_SPLITS = (A_WIDTH, 6 * A_KV_WIDTH, 3 * A_HEADS, A_WIDTH, B_HEADS * B_KEY_DIM, B_HEADS * B_KEY_DIM, B_WIDTH, B_WIDTH)

kernel_name = 'nsa_hgrn2_parallel_heads_decode_step'


def _rms_norm(x, g):
    xf = x.astype(jnp.float32)
    y = xf * lax.rsqrt(jnp.mean(xf * xf, axis=-1, keepdims=True) + EPS)
    return (y * g.astype(jnp.float32)).astype(x.dtype)


def _masked_softmax(logits, mask):
    z = jnp.where(mask, logits.astype(jnp.float32), -jnp.inf)
    m = jnp.max(z, axis=-1, keepdims=True)
    m = jnp.where(jnp.isfinite(m), m, 0.0)
    e = jnp.where(mask, jnp.exp(z - m), 0.0)
    s = jnp.sum(e, axis=-1, keepdims=True)
    return e / jnp.where(s > 0, s, 1.0)


def _rel_bucket(dist):
    n = jnp.maximum(dist, 0)
    exact = REL_BUCKETS // 2
    scale = (REL_BUCKETS - exact) / math.log(REL_MAX_DIST / exact)
    large = exact + (jnp.log(jnp.maximum(n, exact).astype(jnp.float32) / exact) * scale).astype(jnp.int32)
    return jnp.where(n < exact, n, jnp.minimum(large, REL_BUCKETS - 1))


def _cmp_to_slc(n_cmp, n_slc):
    c0 = jnp.arange(n_cmp)[:, None] * CMP_STRIDE
    s0 = jnp.arange(n_slc)[None, :] * SLC_BLOCK
    ov = jnp.minimum(c0 + CMP_BLOCK, s0 + SLC_BLOCK) - jnp.maximum(c0, s0)
    return jnp.maximum(ov, 0).astype(jnp.float32) / CMP_STRIDE


def _nsa_sparse_seq(q, kv_cmp, kv_slc, qpos, w_cmp_k, w_cmp_v, pe_k, pe_v, g_k_cmp, rel_bias):
    t, g, hpg, dh = q.shape
    length = kv_cmp.shape[0]
    bias_g = rel_bias.reshape(REL_BUCKETS, g, hpg)
    n_cmp = (length - CMP_BLOCK) // CMP_STRIDE + 1
    starts = jnp.arange(n_cmp) * CMP_STRIDE
    blk = kv_cmp[starts[:, None] + jnp.arange(CMP_BLOCK)[None, :]]
    k_c = jnp.einsum('clgd,lde->cge', blk[:, :, 0] + pe_k[:, None, :], w_cmp_k.reshape(CMP_BLOCK, dh, dh))
    v_c = jnp.einsum('clgd,lde->cge', blk[:, :, 1] + pe_v[:, None, :], w_cmp_v.reshape(CMP_BLOCK, dh, dh))
    k_c = _rms_norm(k_c, g_k_cmp)
    dist_c = qpos[:, None] - (starts + CMP_BLOCK - 1)[None, :]
    bias_c = jnp.transpose(bias_g[_rel_bucket(dist_c)], (0, 2, 3, 1))
    logit_c = jnp.einsum('tghd,cgd->tghc', q, k_c).astype(jnp.float32) + bias_c
    p_c = _masked_softmax(logit_c, (dist_c >= 0)[:, None, None, :])
    o_c = jnp.einsum('tghc,cgd->tghd', p_c.astype(v_c.dtype), v_c)
    n_slc = -(-length // SLC_BLOCK)
    score = jnp.einsum('tghc,cj->tgj', p_c, _cmp_to_slc(n_cmp, n_slc))
    blk_id = jnp.arange(n_slc)[None, :]
    cur = (qpos // SLC_BLOCK)[:, None]
    valid = (blk_id * SLC_BLOCK <= qpos[:, None])[:, None, :]
    forced = ((blk_id == 0) | (blk_id == cur) | (blk_id == cur - 1))[:, None, :]
    score = jnp.where(valid, jnp.where(forced, jnp.inf, score), -jnp.inf)
    n_sel = min(N_SELECT, n_slc)
    _, sel = lax.top_k(score, n_sel)
    sel_ok = jnp.take_along_axis(jnp.broadcast_to(valid, score.shape), sel, axis=-1)
    pad = n_slc * SLC_BLOCK - length
    kv_b = jnp.pad(kv_slc, ((0, pad), (0, 0), (0, 0), (0, 0))).reshape(n_slc, SLC_BLOCK, 2, g, dh)
    k_blk = jnp.transpose(kv_b[:, :, 0], (2, 0, 1, 3))
    v_blk = jnp.transpose(kv_b[:, :, 1], (2, 0, 1, 3))
    g_idx = jnp.arange(g)[None, :, None]
    offs = jnp.arange(SLC_BLOCK)
    qb = math.gcd(t, QUERY_BLOCK)
    nb = t // qb

    def sel_block(args):
        q_blk, sel_blk, ok_blk, pos_blk = args
        k_g = k_blk[g_idx, sel_blk]
        v_g = v_blk[g_idx, sel_blk]
        dist = pos_blk[:, None, None, None] - (sel_blk[..., None] * SLC_BLOCK + offs)
        mask = ((dist >= 0) & ok_blk[..., None]).reshape(qb, g, 1, n_sel * SLC_BLOCK)
        bias = jnp.moveaxis(bias_g[_rel_bucket(dist), g_idx[..., None]], -1, 2)
        logit = jnp.einsum('qghd,qgnkd->qghnk', q_blk, k_g).astype(jnp.float32) + bias
        p = _masked_softmax(logit.reshape(qb, g, hpg, n_sel * SLC_BLOCK), mask)
        return jnp.einsum('qghm,qgmd->qghd', p.astype(v_g.dtype), v_g.reshape(qb, g, n_sel * SLC_BLOCK, dh))

    o_s = lax.map(sel_block, (q.reshape(nb, qb, g, hpg, dh), sel.reshape(nb, qb, g, n_sel),
                              sel_ok.reshape(nb, qb, g, n_sel), qpos.reshape(nb, qb)))
    return o_c, o_s.reshape(t, g, hpg, dh)


def _window_attn(q, kv, qpos, kpos, rel_bias):
    g, hpg = q.shape[3], q.shape[4]
    dist = qpos[:, :, None] - kpos[:, None, :]
    mask = (dist >= 0) & (dist <= WINDOW) & (kpos[:, None, :] >= 0)
    bias = jnp.transpose(rel_bias.reshape(REL_BUCKETS, g, hpg)[_rel_bucket(dist)], (0, 3, 4, 1, 2))
    logit = jnp.einsum('nbqghd,nbkgd->nbghqk', q, kv[:, :, :, 0]).astype(jnp.float32) + bias[None]
    p = _masked_softmax(logit, mask[None, :, None, None])
    return jnp.einsum('nbghqk,nbkgd->nbqghd', p.astype(kv.dtype), kv[:, :, :, 1])


def _prompt_window(q, kv_win, rel_bias):
    n, t = q.shape[0], q.shape[1]
    qb = math.gcd(t, QUERY_BLOCK)
    nb = t // qb
    kv_pad = jnp.pad(kv_win, ((0, 0), (WINDOW, 0), (0, 0), (0, 0), (0, 0)))
    idx = (jnp.arange(nb) * qb)[:, None] + jnp.arange(WINDOW + qb)[None, :]
    o = _window_attn(q.reshape((n, nb, qb) + q.shape[2:]), kv_pad[:, idx],
                     jnp.arange(t).reshape(nb, qb), idx - WINDOW, rel_bias)
    return o.reshape(q.shape), kv_win[:, -min(WINDOW, t):]


def _sample_window(q, kv_win, buf, past_len, rel_bias):
    kv_all = jnp.concatenate([buf, kv_win], axis=1)
    wb, t = buf.shape[1], q.shape[1]
    qpos = (past_len + jnp.arange(t))[None]
    kpos = (past_len - wb + jnp.arange(wb + t))[None]
    o = _window_attn(q[:, None], kv_all[:, None], qpos, kpos, rel_bias)[:, 0]
    return o, kv_all[:, -min(WINDOW, wb + t):]


def _hgrn2(q, k, v, log_f, s0):
    n, t, h, dk = q.shape
    dv = v.shape[-1]
    c = math.gcd(t, HGRN_CHUNK)
    nc = t // c

    def chunks(a):
        return jnp.moveaxis(a.astype(jnp.float32).reshape((n, nc, c) + a.shape[2:]), 1, 0)

    tri = jnp.tril(jnp.ones((c, c), bool))[None, :, :, None, None]

    def step(s, xs):
        qc, kc, vc, lf = xs
        b = jnp.cumsum(lf, axis=1)
        decay = jnp.exp(jnp.where(tri, b[:, :, None] - b[:, None, :], -jnp.inf))
        att = jnp.einsum('nthk,ntshk,nshk->nths', qc, decay, kc)
        o = jnp.einsum('nths,nshv->nthv', att, vc) + jnp.einsum('nthk,nhkv->nthv', qc * jnp.exp(b), s)
        b_last = b[:, -1]
        s = jnp.exp(b_last)[..., None] * s + jnp.einsum('nshk,nshv->nhkv', kc * jnp.exp(b_last[:, None] - b), vc)
        return s, o

    s, o = lax.scan(step, s0.astype(jnp.float32), (chunks(q), chunks(k), chunks(v), chunks(log_f)))
    return jnp.moveaxis(o, 0, 1).reshape(n, t, h, dv), s


def _project(x, g_norm, w_in, g_q, g_k_slc, g_k_win, lb):
    n, t, _ = x.shape
    g, hpg, dh = A_KV_GROUPS, A_HPG, A_HEAD_DIM
    h = _rms_norm(x, g_norm)
    cuts = np.cumsum(IN_SPLITS)[:-1].tolist()
    q_a, kv_raw, gate, z_a, q_b, f_b, i_b, z_b = jnp.split(h @ w_in, cuts, axis=-1)
    q_a = _rms_norm(q_a.reshape(n, t, g, hpg, dh), g_q) * (dh ** -0.5)
    kv_raw = kv_raw.reshape(n, t, 3, 2, g, dh)
    kv_cmp = kv_raw[:, :, 0]
    kv_slc = jnp.stack([_rms_norm(kv_raw[:, :, 1, 0], g_k_slc), kv_raw[:, :, 1, 1]], axis=2)
    kv_win = jnp.stack([_rms_norm(kv_raw[:, :, 2, 0], g_k_win), kv_raw[:, :, 2, 1]], axis=2)
    gate = jax.nn.sigmoid(gate.astype(jnp.float32)).reshape(n, t, 3, g, hpg)
    f = lb + (1.0 - lb) * jax.nn.sigmoid(f_b.astype(jnp.float32))
    shp = (n, t, B_HEADS, B_KEY_DIM)
    return (q_a, kv_cmp, kv_slc, kv_win, gate, z_a, q_b.reshape(shp), (1.0 - f).reshape(shp),
            i_b.reshape(n, t, B_HEADS, B_VAL_DIM), jnp.log(f).reshape(shp), z_b)


def _combine(x, gate, o_c, o_s, o_w, z_a, o_h, z_b, g_o, w_out):
    n, t, _ = x.shape
    gt = gate[..., None]
    o_a = (gt[:, :, 0] * o_c + gt[:, :, 1] * o_s + gt[:, :, 2] * o_w).reshape(n, t, A_WIDTH).astype(x.dtype) * jax.nn.silu(z_a)
    o_b = _rms_norm(o_h, g_o).reshape(n, t, B_WIDTH).astype(x.dtype) * jax.nn.silu(z_b)
    return x + jnp.concatenate([o_a, o_b], axis=-1) @ w_out


def setup_inputs(seed: int = 0) -> dict:
    key = jax.random.key(seed)
    k = jax.random.split(key, 22)
    n_pages = PAST_LEN // PAGE_SIZE
    n_used = DEC_BATCH * n_pages
    n_pool = n_used + (n_used + 3) // 4
    wb = min(WINDOW, PAST_LEN)
    g, dh = A_KV_GROUPS, A_HEAD_DIM
    n_in = sum(IN_SPLITS)

    def nrm(kk, shape, s=1.0):
        return s * jax.random.normal(kk, shape, jnp.float32)

    page_table = jax.random.permutation(k[6], n_pool)[:n_used].reshape(DEC_BATCH, n_pages).astype(jnp.int32)
    return {
        'x_prompt': nrm(k[0], (BATCH, SEQ, D_MODEL)),
        'x_sample': nrm(k[1], (DEC_BATCH, DEC_SEQ, D_MODEL)),
        'cache_kv_cmp': nrm(k[2], (DEPTH, n_pool, PAGE_SIZE, 2, g, dh)),
        'cache_kv_slc': nrm(k[3], (DEPTH, n_pool, PAGE_SIZE, 2, g, dh)),
        'cache_kv_win': nrm(k[4], (DEPTH, DEC_BATCH, wb, 2, g, dh)),
        'state_hgrn': nrm(k[5], (DEPTH, DEC_BATCH, B_HEADS, B_KEY_DIM, B_VAL_DIM), 0.5),
        'page_table': page_table,
        'g_norm': 1.0 + nrm(k[7], (DEPTH, D_MODEL), 0.02),
        'w_in': nrm(k[8], (DEPTH, D_MODEL, n_in), D_MODEL ** -0.5),
        'w_out': nrm(k[9], (DEPTH, MIX_WIDTH, D_MODEL), MIX_WIDTH ** -0.5),
        'g_q': 1.0 + nrm(k[10], (DEPTH, dh), 0.02),
        'g_k_slc': 1.0 + nrm(k[11], (DEPTH, dh), 0.02),
        'g_k_win': 1.0 + nrm(k[12], (DEPTH, dh), 0.02),
        'g_k_cmp': 1.0 + nrm(k[13], (DEPTH, dh), 0.02),
        'w_cmp_k': nrm(k[14], (DEPTH, CMP_BLOCK * dh, dh), (CMP_BLOCK * dh) ** -0.5),
        'w_cmp_v': nrm(k[15], (DEPTH, CMP_BLOCK * dh, dh), (CMP_BLOCK * dh) ** -0.5),
        'pe_cmp_k': nrm(k[16], (DEPTH, CMP_BLOCK, dh), 0.1),
        'pe_cmp_v': nrm(k[17], (DEPTH, CMP_BLOCK, dh), 0.1),
        'rel_bias': nrm(k[18], (REL_BUCKETS, A_HEADS), 0.5),
        'lb_logits': nrm(k[19], (DEPTH + 1, B_HEADS * B_KEY_DIM)),
        'g_o_hgrn': 1.0 + nrm(k[20], (DEPTH, B_VAL_DIM), 0.02),
    }


def reference(x_prompt, x_sample, cache_kv_cmp, cache_kv_slc, cache_kv_win, state_hgrn, page_table,
              g_norm, w_in, w_out, g_q, g_k_slc, g_k_win, g_k_cmp, w_cmp_k, w_cmp_v, pe_cmp_k, pe_cmp_v,
              rel_bias, lb_logits, g_o_hgrn):
    lower_bounds = jnp.cumsum(jax.nn.softmax(lb_logits.astype(jnp.float32), axis=0), axis=0)
    past_len = page_table.shape[1] * cache_kv_cmp.shape[2]
    xp, xs = x_prompt, x_sample
    p_cmp, p_slc, p_win, p_st = [], [], [], []
    s_cmp, s_slc, s_win, s_st = [], [], [], []
    for l in range(DEPTH):
        sparse_w = (w_cmp_k[l], w_cmp_v[l], pe_cmp_k[l], pe_cmp_v[l], g_k_cmp[l], rel_bias)
        proj_w = (g_norm[l], w_in[l], g_q[l], g_k_slc[l], g_k_win[l], lower_bounds[l])
        qa, kvc, kvs, kvw, gate, za, qh, kh, vh, lfh, zh = _project(xp, *proj_w)
        qpos_p = jnp.arange(xp.shape[1])
        o_c, o_s = lax.map(lambda a: _nsa_sparse_seq(a[0], a[1], a[2], qpos_p, *sparse_w), (qa, kvc, kvs))
        o_w, win_p = _prompt_window(qa, kvw, rel_bias)
        s0 = jnp.zeros((xp.shape[0], B_HEADS, B_KEY_DIM, B_VAL_DIM), jnp.float32)
        o_h, st_p = _hgrn2(qh, kh, vh, lfh, s0)
        xp_new = _combine(xp, gate, o_c, o_s, o_w, za, o_h, zh, g_o_hgrn[l], w_out[l])
        p_cmp.append(kvc)
        p_slc.append(kvs)
        p_win.append(win_p)
        p_st.append(st_p.astype(xp.dtype))
        qa, kvc, kvs, kvw, gate, za, qh, kh, vh, lfh, zh = _project(xs, *proj_w)
        qpos_s = past_len + jnp.arange(xs.shape[1])
        pool_c, pool_s = cache_kv_cmp[l], cache_kv_slc[l]

        def sample_seq(a, pool_c=pool_c, pool_s=pool_s, qpos_s=qpos_s, sparse_w=sparse_w):
            q_s, kvc_new, kvs_new, pages = a
            rows = (-1,) + kvc_new.shape[1:]
            kvc_all = jnp.concatenate([pool_c[pages].reshape(rows), kvc_new], axis=0)
            kvs_all = jnp.concatenate([pool_s[pages].reshape(rows), kvs_new], axis=0)
            return _nsa_sparse_seq(q_s, kvc_all, kvs_all, qpos_s, *sparse_w)

        o_c, o_s = lax.map(sample_seq, (qa, kvc, kvs, page_table))
        o_w, win_s = _sample_window(qa, kvw, cache_kv_win[l], past_len, rel_bias)
        o_h, st_s = _hgrn2(qh, kh, vh, lfh, state_hgrn[l])
        xs_new = _combine(xs, gate, o_c, o_s, o_w, za, o_h, zh, g_o_hgrn[l], w_out[l])
        s_cmp.append(kvc)
        s_slc.append(kvs)
        s_win.append(win_s)
        s_st.append(st_s.astype(state_hgrn.dtype))
        xp, xs = xp_new, xs_new
    return (xp, xs, jnp.stack(p_cmp), jnp.stack(p_slc), jnp.stack(p_win), jnp.stack(p_st),
            jnp.stack(s_cmp), jnp.stack(s_slc), jnp.stack(s_win), jnp.stack(s_st))
```

```python
import functools
import math

import jax
import jax.numpy as jnp
import numpy as np
from jax import lax
from jax.experimental import pallas as pl
from jax.experimental.pallas import tpu as pltpu

f32 = jnp.float32
bf16 = jnp.bfloat16

D_MODEL = 2048
A_HEADS = 8
A_KV_GROUPS = 2
A_HPG = A_HEADS // A_KV_GROUPS
DH = 128
A_WIDTH = A_HEADS * DH
CMP_BLOCK = 32
CMP_STRIDE = 16
SLC_BLOCK = 64
N_SELECT = 16
WINDOW = 512
B_HEADS = 8
B_WIDTH = B_HEADS * DH
REL_BUCKETS = 32
REL_MAX_DIST = 128
EPS = 1e-6
NEG = -1e30

VMEM_LIMIT = 56 * 1024 * 1024


def _sigmoid(x):
    return 1.0 / (1.0 + jnp.exp(-x))


def _head_norm(a, g):
    return a * lax.rsqrt(jnp.mean(a * a, axis=-1, keepdims=True) + EPS) * g


PROJ_TN = 512
PROJ_NT = 15


def _proj_kernel(x_ref, gn_ref, w_ref, wg_ref, gq_ref, gks_ref, gkw_ref, lb_ref,
                 qa_ref, kvc_ref, kvs_ref, kvw_ref, gate_ref, za_ref, qb_ref, f_ref, ib_ref, zb_ref,
                 h_scr):
    j = pl.program_id(1)
    tm = x_ref.shape[0]

    @pl.when(j == 0)
    def _():
        x = x_ref[...]
        h = x * lax.rsqrt(jnp.mean(x * x, axis=-1, keepdims=True) + EPS) * gn_ref[...]
        hb = h.astype(bf16)
        h_scr[...] = hb
        gate_ref[...] = _sigmoid(jnp.dot(hb, wg_ref[...], preferred_element_type=f32))

    acc = jnp.dot(h_scr[...], w_ref[...], preferred_element_type=f32)

    def cols(c):
        return acc[:, c * DH:(c + 1) * DH]

    @pl.when(j < 2)
    def _():
        gq = gq_ref[...]
        qa_ref[...] = jnp.concatenate(
            [_head_norm(cols(c), gq) * (DH ** -0.5) for c in range(4)], axis=1).astype(qa_ref.dtype)

    @pl.when(j == 2)
    def _():
        for c in range(4):
            kvc_ref[pl.ds(c, tm, stride=4), :] = cols(c)

    @pl.when(j == 3)
    def _():
        g = gks_ref[...]
        for c in range(4):
            kvs_ref[pl.ds(c, tm, stride=4), :] = _head_norm(cols(c), g) if c < 2 else cols(c)

    @pl.when(j == 4)
    def _():
        g = gkw_ref[...]
        for c in range(4):
            kvw_ref[pl.ds(c, tm, stride=4), :] = _head_norm(cols(c), g) if c < 2 else cols(c)

    @pl.when((j >= 5) & (j < 7))
    def _():
        za_ref[...] = acc * _sigmoid(acc)

    @pl.when((j >= 7) & (j < 9))
    def _():
        qb_ref[...] = acc

    for jj in range(2):
        @pl.when(j == 9 + jj)
        def _():
            lb = lb_ref[:, jj * PROJ_TN:(jj + 1) * PROJ_TN]
            f_ref[...] = lb + (1.0 - lb) * _sigmoid(acc)

    @pl.when((j >= 11) & (j < 13))
    def _():
        ib_ref[...] = acc

    @pl.when(j >= 13)
    def _():
        zb_ref[...] = acc * _sigmoid(acc)


def _proj(x2d, gn, w_main, w_gate, gq, gks, gkw, lb, tm):
    m = x2d.shape[0]
    assert m % tm == 0
    grid = (m // tm, PROJ_NT)

    def seg(start):
        return lambda i, j: (i, jnp.clip(j - start, 0, 1))

    row = lambda i, j: (i, 0)
    const = lambda i, j: (0, 0)
    out_shape = [
        jax.ShapeDtypeStruct((m, A_WIDTH), bf16),
        jax.ShapeDtypeStruct((m * 4, DH), f32),
        jax.ShapeDtypeStruct((m * 4, DH), f32),
        jax.ShapeDtypeStruct((m * 4, DH), f32),
        jax.ShapeDtypeStruct((m, A_KV_GROUPS * DH), f32),
        jax.ShapeDtypeStruct((m, A_WIDTH), f32),
        jax.ShapeDtypeStruct((m, B_WIDTH), f32),
        jax.ShapeDtypeStruct((m, B_WIDTH), f32),
        jax.ShapeDtypeStruct((m, B_WIDTH), f32),
        jax.ShapeDtypeStruct((m, B_WIDTH), f32),
    ]
    out_specs = [
        pl.BlockSpec((tm, PROJ_TN), seg(0)),
        pl.BlockSpec((tm * 4, DH), row),
        pl.BlockSpec((tm * 4, DH), row),
        pl.BlockSpec((tm * 4, DH), row),
        pl.BlockSpec((tm, A_KV_GROUPS * DH), row),
        pl.BlockSpec((tm, PROJ_TN), seg(5)),
        pl.BlockSpec((tm, PROJ_TN), seg(7)),
        pl.BlockSpec((tm, PROJ_TN), seg(9)),
        pl.BlockSpec((tm, PROJ_TN), seg(11)),
        pl.BlockSpec((tm, PROJ_TN), seg(13)),
    ]
    in_specs = [
        pl.BlockSpec((tm, D_MODEL), row),
        pl.BlockSpec((1, D_MODEL), const),
        pl.BlockSpec((D_MODEL, PROJ_TN), lambda i, j: (0, j)),
        pl.BlockSpec((D_MODEL, A_KV_GROUPS * DH), const),
        pl.BlockSpec((1, DH), const),
        pl.BlockSpec((1, DH), const),
        pl.BlockSpec((1, DH), const),
        pl.BlockSpec((1, B_WIDTH), const),
    ]
    return pl.pallas_call(
        _proj_kernel, grid=grid, in_specs=in_specs, out_specs=out_specs, out_shape=out_shape,
        scratch_shapes=[pltpu.VMEM((tm, D_MODEL), bf16)],
        compiler_params=pltpu.CompilerParams(
            dimension_semantics=("arbitrary", "arbitrary"), vmem_limit_bytes=VMEM_LIMIT),
        name="proj",
    )(x2d, gn, w_main, w_gate, gq, gks, gkw, lb)


def _prep_proj_weights(w_in):
    a0 = A_WIDTH + 6 * A_KV_GROUPS * DH
    a1 = a0 + 3 * A_HEADS
    w_main = jnp.concatenate([w_in[:, :a0], w_in[:, a1:]], axis=1).astype(bf16)
    wg = w_in[:, a0:a1].reshape(-1, 3, A_KV_GROUPS, A_HPG).transpose(0, 2, 1, 3).reshape(-1, A_KV_GROUPS, 3 * A_HPG)
    w_gate = jnp.pad(wg, ((0, 0), (0, 0), (0, DH - 3 * A_HPG))).reshape(-1, A_KV_GROUPS * DH).astype(bf16)
    return w_main, w_gate


def _hgrn_kernel(q_ref, f_ref, v_ref, zs_ref, s0_ref, go_ref, tri_ref, o_ref, sout_ref, st_scr, *, sc):
    c = pl.program_id(1)
    tc = q_ref.shape[0]

    @pl.when(c == 0)
    def _():
        for h in range(B_HEADS):
            st_scr[h] = s0_ref[0, h].T

    f = f_ref[...]
    lf = jnp.log(f)
    b_all = lax.dot_general(tri_ref[...], lf, (((1,), (0,)), ((), ())),
                            precision=lax.Precision.HIGHEST, preferred_element_type=f32)
    t_idx = lax.broadcasted_iota(jnp.int32, (sc, DH), 0)
    go = go_ref[...]
    for h in range(B_HEADS):
        lanes = slice(h * DH, (h + 1) * DH)
        outs = []
        for i in range(tc // sc):
            rows = slice(i * sc, (i + 1) * sc)
            b = b_all[rows, lanes]
            q = q_ref[rows, lanes]
            k = 1.0 - f[rows, lanes]
            v = v_ref[rows, lanes]
            st = st_scr[h]
            o = lax.dot_general((q * jnp.exp(b)).astype(bf16), st.astype(bf16),
                                (((1,), (1,)), ((), ())), preferred_element_type=f32)
            for s in range(sc):
                e = jnp.exp(jnp.minimum(b - b[s:s + 1, :], 0.0))
                a = jnp.where(t_idx >= s, q * e * k[s:s + 1, :], 0.0)
                o = o + jnp.sum(a, axis=-1, keepdims=True) * v[s:s + 1, :]
            bl = b[sc - 1:sc, :]
            kd = k * jnp.exp(bl - b)
            ut = lax.dot_general(v.astype(bf16), kd.astype(bf16), (((0,), (0,)), ((), ())),
                                 preferred_element_type=f32)
            st_scr[h] = st * jnp.exp(bl) + ut
            outs.append(o)
        oh = jnp.concatenate(outs, axis=0) if len(outs) > 1 else outs[0]
        o_ref[:, lanes] = _head_norm(oh, go) * zs_ref[:, lanes]

    @pl.when(c == pl.num_programs(1) - 1)
    def _():
        for h in range(B_HEADS):
            sout_ref[0, h] = st_scr[h].T


def _block_tri(tc, sc):
    r = np.arange(tc)
    return jnp.asarray(((r[:, None] // sc == r[None, :] // sc) & (r[None, :] <= r[:, None])).astype(np.float32))


def _hgrn(qb, f, ib, zs, s0, go, n, t, tc, sc):
    assert t % tc == 0 and tc % sc == 0
    nc = t // tc
    rows = lambda i, c: (i * nc + c, 0)
    st = lambda i, c: (i, 0, 0, 0)
    const = lambda i, c: (0, 0)
    return pl.pallas_call(
        functools.partial(_hgrn_kernel, sc=sc),
        grid=(n, nc),
        in_specs=[pl.BlockSpec((tc, B_WIDTH), rows)] * 4 + [
            pl.BlockSpec((1, B_HEADS, DH, DH), st),
            pl.BlockSpec((1, DH), const),
            pl.BlockSpec((tc, tc), const),
        ],
        out_specs=[pl.BlockSpec((tc, B_WIDTH), rows), pl.BlockSpec((1, B_HEADS, DH, DH), st)],
        out_shape=[jax.ShapeDtypeStruct((n * t, B_WIDTH), f32), jax.ShapeDtypeStruct((n, B_HEADS, DH, DH), f32)],
        scratch_shapes=[pltpu.VMEM((B_HEADS, DH, DH), f32)],
        compiler_params=pltpu.CompilerParams(
            dimension_semantics=("arbitrary", "arbitrary"), vmem_limit_bytes=VMEM_LIMIT),
        name="hgrn",
    )(qb, f, ib, zs, s0, go, _block_tri(tc, sc))


def _rel_bucket_np(dist):
    n = np.maximum(dist, 0)
    exact = REL_BUCKETS // 2
    scale = np.float32((REL_BUCKETS - exact) / math.log(REL_MAX_DIST / exact))
    large = exact + (np.log(np.maximum(n, exact).astype(np.float32) / np.float32(exact)) * scale).astype(np.int32)
    return np.where(n < exact, n, np.minimum(large, REL_BUCKETS - 1)).astype(np.int32)


def _bias_lookup(rel_bias, dist_np):
    b = jnp.take(rel_bias.astype(f32), jnp.asarray(_rel_bucket_np(dist_np)), axis=0)
    return jnp.moveaxis(b, -1, 0)


def _near_tiles(rel_bias):
    i = np.arange(128)
    d = np.stack([r * 128 + i[:, None] - i[None, :] for r in range(2)])
    return _bias_lookup(rel_bias, d) - rel_bias[REL_BUCKETS - 1].astype(f32)[:, None, None, None]


def _compress_combine(lhs, wcat, pe2):
    out = jnp.dot(lhs, wcat, preferred_element_type=f32)
    pc = jnp.dot(pe2, wcat, preferred_element_type=f32)
    const = pc[0:1, :DH] + pc[1:2, DH:]
    r = lhs.shape[0]
    return out[:, :DH] + pltpu.roll(out[:, DH:], r - 1, 0) + const


def _compress_kernel(kv_ref, wk_ref, wv_ref, pek_ref, pev_ref, gk_ref, kc_ref, vc_ref):
    nchunk = kv_ref.shape[0] // (4 * CMP_STRIDE)

    def chunk_rows(cg):
        return jnp.concatenate(
            [kv_ref[pl.ds(4 * l + cg, nchunk, stride=4 * CMP_STRIDE), :].astype(bf16) for l in range(CMP_STRIDE)],
            axis=1)

    kc = _compress_combine(jnp.concatenate([chunk_rows(0), chunk_rows(1)], axis=0), wk_ref[...], pek_ref[...])
    vc = _compress_combine(jnp.concatenate([chunk_rows(2), chunk_rows(3)], axis=0), wv_ref[...], pev_ref[...])
    kc = _head_norm(kc, gk_ref[...])
    for g in range(A_KV_GROUPS):
        kc_ref[0, g] = kc[g * nchunk:(g + 1) * nchunk]
        vc_ref[0, g] = vc[g * nchunk:(g + 1) * nchunk]


def _prep_cmp_weights(w, pe):
    half = CMP_STRIDE * DH
    wcat = jnp.concatenate([w[:half], w[half:]], axis=1).astype(bf16)
    return wcat, pe.reshape(2, half).astype(bf16)


def _compress(kvc, wk, wv, pek, pev, gk, n, t):
    nchunk = t // CMP_STRIDE
    const = lambda i: (0, 0)
    return pl.pallas_call(
        _compress_kernel, grid=(n,),
        in_specs=[pl.BlockSpec((t * 4, DH), lambda i: (i, 0)),
                  pl.BlockSpec(wk.shape, const), pl.BlockSpec(wv.shape, const),
                  pl.BlockSpec(pek.shape, const), pl.BlockSpec(pev.shape, const), pl.BlockSpec((1, DH), const)],
        out_specs=[pl.BlockSpec((1, A_KV_GROUPS, nchunk, DH), lambda i: (i, 0, 0, 0))] * 2,
        out_shape=[jax.ShapeDtypeStruct((n, A_KV_GROUPS, nchunk, DH), f32)] * 2,
        compiler_params=pltpu.CompilerParams(dimension_semantics=("arbitrary",), vmem_limit_bytes=VMEM_LIMIT),
        name="compress",
    )(kvc, wk, wv, pek, pev, gk)


def _softmax_tile(carry, s, mask, v_bf):
    m, l, acc = carry
    s = jnp.where(mask, s, NEG)
    m_new = jnp.maximum(m, jnp.max(s, axis=-1, keepdims=True))
    alpha = jnp.exp(m - m_new)
    p = jnp.where(mask, jnp.exp(s - m_new), 0.0)
    l = alpha * l + jnp.sum(p, axis=-1, keepdims=True)
    h, r, k = p.shape
    pv = jnp.dot(p.reshape(h * r, k).astype(bf16), v_bf, preferred_element_type=f32).reshape(h, r, DH)
    return m_new, l, alpha * acc + pv


def _softmax_finish(carry):
    m, l, acc = carry
    return acc / jnp.where(l > 0, l, 1.0)


def _select_blocks(score_t, tpos, n_blocks):
    nb = score_t.shape[0]
    j = lax.broadcasted_iota(jnp.int32, score_t.shape, 0)
    valid = (j * SLC_BLOCK <= tpos) & (j < n_blocks)
    cur = tpos >> 6
    forced = (j == 0) | (j == cur) | (j == cur - 1)
    val = jnp.where(valid, jnp.where(forced, 1e30, score_t), -1.0)
    rank = jnp.zeros(score_t.shape, jnp.int32)
    for i in range(n_blocks):
        vi = val[i:i + 1, :]
        beats = (vi > val) | ((vi == val) & (i < j))
        rank = rank + beats.astype(jnp.int32)
    return jnp.where((rank < N_SELECT) & valid, 1.0, 0.0)


QB = 128


def _nsa_prompt_kernel(q_ref, kc_ref, vc_ref, bc_ref, kvs_ref, kvw_ref, gate_ref, za_ref, bt_ref, mt_ref, e_ref,
                       o_ref, msk_scr):
    g = pl.program_id(1)
    qb = pl.program_id(2)
    q0 = qb * QB
    nkt = msk_scr.shape[0]
    qall = q_ref[...]
    q = jnp.concatenate([qall[:, h * DH:(h + 1) * DH] for h in range(A_HPG)], axis=0)
    ii = lax.broadcasted_iota(jnp.int32, (QB, QB), 0)
    jj = lax.broadcasted_iota(jnp.int32, (QB, QB), 1)

    def logits(k_bf):
        s = lax.dot_general(q, k_bf, (((1,), (1,)), ((), ())), preferred_element_type=f32)
        return s.reshape(A_HPG, QB, k_bf.shape[0])

    def init():
        return (jnp.full((A_HPG, QB, 1), NEG, f32), jnp.zeros((A_HPG, QB, 1), f32), jnp.zeros((A_HPG, QB, DH), f32))

    sc = logits(kc_ref[0, 0].astype(bf16)) + bc_ref[...]
    mask_c = (q0 + ii >= CMP_STRIDE * jj + (CMP_BLOCK - 1))[None]
    sc = jnp.where(mask_c, sc, NEG)
    mc = jnp.max(sc, axis=-1, keepdims=True)
    ec = jnp.where(mask_c, jnp.exp(sc - mc), 0.0)
    lc = jnp.sum(ec, axis=-1, keepdims=True)
    pc = ec / jnp.where(lc > 0, lc, 1.0)
    o_c = jnp.dot(pc.reshape(A_HPG * QB, QB).astype(bf16), vc_ref[0, 0].astype(bf16),
                  preferred_element_type=f32).reshape(A_HPG, QB, DH)

    ps = pc[0] + pc[1] + pc[2] + pc[3]
    score_t = lax.dot_general(mt_ref[...], ps, (((1,), (1,)), ((), ())),
                              precision=lax.Precision.HIGHEST, preferred_element_type=f32)
    nb = score_t.shape[0]
    tpos = q0 + lax.broadcasted_iota(jnp.int32, (nb, QB), 1)
    sel_t = _select_blocks(score_t, tpos, nb)
    key_mask = lax.dot_general(sel_t.astype(bf16), e_ref[...], (((0,), (0,)), ((), ())),
                               preferred_element_type=f32)
    for kt in range(nkt):
        msk_scr[kt] = key_mask[:, kt * QB:(kt + 1) * QB]

    def kv_tile(ref, kt, c):
        return ref[pl.ds(pl.multiple_of(kt * (4 * QB), 4 * QB) + c, QB, stride=4), :].astype(bf16)

    def far(kt, carry):
        msk = (msk_scr[kt] > 0.5)[None]
        return _softmax_tile(carry, logits(kv_tile(kvs_ref, kt, g)), msk, kv_tile(kvs_ref, kt, 2 + g))

    carry = lax.fori_loop(0, jnp.maximum(qb - 1, 0), far, init())
    for r in (1, 0):
        kt = jnp.maximum(qb - r, 0)
        msk = (msk_scr[kt] > 0.5) & (ii * 0 + (qb - r) >= 0)
        if r == 0:
            msk = msk & (ii >= jj)
        carry = _softmax_tile(carry, logits(kv_tile(kvs_ref, kt, g)) + bt_ref[:, r], msk[None],
                              kv_tile(kvs_ref, kt, 2 + g))
    o_s = _softmax_finish(carry)

    carry = init()
    for r in (4, 3, 2, 1, 0):
        kt = jnp.maximum(qb - r, 0)
        msk = ii * 0 + (qb - r) >= 0
        if r == 0:
            msk = msk & (ii >= jj)
        if r == WINDOW // QB:
            msk = msk & (jj >= ii)
        s = logits(kv_tile(kvw_ref, kt, g))
        if r < 2:
            s = s + bt_ref[:, r]
        carry = _softmax_tile(carry, s, msk[None], kv_tile(kvw_ref, kt, 2 + g))
    o_w = _softmax_finish(carry)

    gate = gate_ref[...]
    za = za_ref[...]
    outs = []
    for h in range(A_HPG):
        def gcol(br):
            return gate[:, br * A_HPG + h:br * A_HPG + h + 1]
        o = gcol(0) * o_c[h] + gcol(1) * o_s[h] + gcol(2) * o_w[h]
        outs.append(o * za[:, h * DH:(h + 1) * DH])
    o_ref[...] = jnp.concatenate(outs, axis=1)


def _cmp_to_slc_t(n_cmp_pad, n_slc_pad, n_cmp, n_slc):
    c0 = np.arange(n_cmp_pad)[None, :] * CMP_STRIDE
    s0 = np.arange(n_slc_pad)[:, None] * SLC_BLOCK
    ov = np.minimum(c0 + CMP_BLOCK, s0 + SLC_BLOCK) - np.maximum(c0, s0)
    m = np.maximum(ov, 0).astype(np.float32) / CMP_STRIDE
    m[:, n_cmp:] = 0
    m[n_slc:, :] = 0
    return m


def _nsa_prompt(qa, kc, vc, kvs, kvw, gate, za, rel_bias, n, t):
    nq = t // QB
    n_cmp = (t - CMP_BLOCK) // CMP_STRIDE + 1
    n_slc = -(-t // SLC_BLOCK)
    assert kc.shape[2] == QB and n_slc % 8 == 0
    tq = np.arange(t)[:, None]
    cc = np.arange(QB)[None, :]
    bias_c = _bias_lookup(rel_bias, tq - (cc * CMP_STRIDE + CMP_BLOCK - 1))
    bt = _near_tiles(rel_bias)
    mt = jnp.asarray(_cmp_to_slc_t(QB, n_slc, n_cmp, n_slc))
    e = jnp.asarray((np.arange(t)[None, :] // SLC_BLOCK == np.arange(n_slc)[:, None]).astype(np.float32), dtype=bf16)
    qrow = lambda i, g, b: (i * nq + b, g)
    return pl.pallas_call(
        _nsa_prompt_kernel, grid=(n, A_KV_GROUPS, nq),
        in_specs=[
            pl.BlockSpec((QB, A_HPG * DH), qrow),
            pl.BlockSpec((1, 1, QB, DH), lambda i, g, b: (i, g, 0, 0)),
            pl.BlockSpec((1, 1, QB, DH), lambda i, g, b: (i, g, 0, 0)),
            pl.BlockSpec((A_HPG, QB, QB), lambda i, g, b: (g, b, 0)),
            pl.BlockSpec((t * 4, DH), lambda i, g, b: (i, 0)),
            pl.BlockSpec((t * 4, DH), lambda i, g, b: (i, 0)),
            pl.BlockSpec((QB, DH), qrow),
            pl.BlockSpec((QB, A_HPG * DH), qrow),
            pl.BlockSpec((A_HPG, 2, QB, QB), lambda i, g, b: (g, 0, 0, 0)),
            pl.BlockSpec(mt.shape, lambda i, g, b: (0, 0)),
            pl.BlockSpec(e.shape, lambda i, g, b: (0, 0)),
        ],
        out_specs=pl.BlockSpec((QB, A_HPG * DH), qrow),
        out_shape=jax.ShapeDtypeStruct((n * t, A_WIDTH), f32),
        scratch_shapes=[pltpu.VMEM((nq, QB, QB), f32)],
        compiler_params=pltpu.CompilerParams(
            dimension_semantics=("arbitrary", "arbitrary", "arbitrary"), vmem_limit_bytes=VMEM_LIMIT),
        name="nsa_prompt",
    )(qa, kc, vc, bias_c, kvs, kvw, gate, za, bt, mt, e)


def _nsa_sample_kernel(pt_ref, *refs, n_pages, page, t_new, n_blocks):
    del pt_ref
    cmp_pages = refs[:n_pages]
    slc_pages = refs[n_pages:2 * n_pages]
    (q_ref, kvs_new_ref, kvw_new_ref, win_ref, wk_ref, wv_ref, pek_ref, pev_ref, gk_ref, bc_ref, bl_ref, bn_ref,
     gate_ref, za_ref, mt_ref, e_ref, o_ref, win_out_ref, xc_scr) = refs[2 * n_pages:]
    past = n_pages * page
    nchunk = past // CMP_STRIDE
    pad_new = QB
    rows_q = A_HPG * t_new

    for p in range(n_pages):
        for cg in range(4):
            xc_scr[cg, p * page:(p + 1) * page, :] = cmp_pages[p][0, pl.ds(cg, page, stride=4), :]

    def chunk_rows(cg):
        return jnp.concatenate(
            [xc_scr[cg, pl.ds(l, nchunk, stride=CMP_STRIDE), :].astype(bf16) for l in range(CMP_STRIDE)], axis=1)

    kc = _compress_combine(jnp.concatenate([chunk_rows(0), chunk_rows(1)], axis=0), wk_ref[...], pek_ref[...])
    vc = _compress_combine(jnp.concatenate([chunk_rows(2), chunk_rows(3)], axis=0), wv_ref[...], pev_ref[...])
    kc = _head_norm(kc, gk_ref[...]).astype(bf16)
    vc = vc.astype(bf16)

    tt = lax.broadcasted_iota(jnp.int32, (t_new, pad_new), 0)
    uu = lax.broadcasted_iota(jnp.int32, (t_new, pad_new), 1)
    new_mask = (uu <= tt)
    qall = q_ref[...]
    gate_all = gate_ref[...]
    za = za_ref[...]
    zeros_pad = jnp.zeros((pad_new - t_new, DH), bf16)

    def new_rows(ref, c):
        return jnp.concatenate([ref[pl.ds(c, t_new, stride=4), :].astype(bf16), zeros_pad], axis=0)

    def attend(q, k_all, v_all, bias_tail, mask):
        nk = k_all.shape[0]
        s = lax.dot_general(q, k_all, (((1,), (1,)), ((), ())), preferred_element_type=f32).reshape(A_HPG, t_new, nk)
        s = jnp.concatenate([s[:, :, :nk - 2 * QB], s[:, :, nk - 2 * QB:] + bias_tail], axis=-1)
        carry = (jnp.full((A_HPG, t_new, 1), NEG, f32), jnp.zeros((A_HPG, t_new, 1), f32),
                 jnp.zeros((A_HPG, t_new, DH), f32))
        return _softmax_finish(_softmax_tile(carry, s, mask[None], v_all))

    outs = []
    for g in range(A_KV_GROUPS):
        q = jnp.concatenate([qall[:, (g * A_HPG + h) * DH:(g * A_HPG + h + 1) * DH] for h in range(A_HPG)], axis=0)
        cmask = (past + lax.broadcasted_iota(jnp.int32, (t_new, nchunk), 0)
                 >= CMP_STRIDE * lax.broadcasted_iota(jnp.int32, (t_new, nchunk), 1) + (CMP_BLOCK - 1))[None]
        sc = lax.dot_general(q, kc[g * nchunk:(g + 1) * nchunk], (((1,), (1,)), ((), ())),
                             preferred_element_type=f32).reshape(A_HPG, t_new, nchunk) + bc_ref[g * A_HPG:(g + 1) * A_HPG]
        sc = jnp.where(cmask, sc, NEG)
        mc = jnp.max(sc, axis=-1, keepdims=True)
        ec = jnp.where(cmask, jnp.exp(sc - mc), 0.0)
        lc = jnp.sum(ec, axis=-1, keepdims=True)
        pc = ec / jnp.where(lc > 0, lc, 1.0)
        o_c = jnp.dot(pc.reshape(rows_q, nchunk).astype(bf16), vc[g * nchunk:(g + 1) * nchunk],
                      preferred_element_type=f32).reshape(A_HPG, t_new, DH)
        ps = pc[0] + pc[1] + pc[2] + pc[3]
        score_t = lax.dot_general(mt_ref[...], ps, (((1,), (1,)), ((), ())),
                                  precision=lax.Precision.HIGHEST, preferred_element_type=f32)
        tpos = past + lax.broadcasted_iota(jnp.int32, score_t.shape, 1)
        sel_t = _select_blocks(score_t, tpos, n_blocks)
        key_mask = lax.dot_general(sel_t.astype(bf16), e_ref[...], (((0,), (0,)), ((), ())),
                                   preferred_element_type=f32) > 0.5
        k_all = jnp.concatenate([slc_pages[p][0, pl.ds(g, page, stride=4), :].astype(bf16) for p in range(n_pages)]
                                + [new_rows(kvs_new_ref, g)], axis=0)
        v_all = jnp.concatenate([slc_pages[p][0, pl.ds(2 + g, page, stride=4), :].astype(bf16) for p in range(n_pages)]
                                + [new_rows(kvs_new_ref, 2 + g)], axis=0)
        bias_tail = jnp.concatenate([bl_ref[g * A_HPG:(g + 1) * A_HPG], bn_ref[g * A_HPG:(g + 1) * A_HPG]], axis=-1)
        o_s = attend(q, k_all, v_all, bias_tail, jnp.concatenate([key_mask, new_mask], axis=1))
        nw = win_ref.shape[1] // 4
        kw = jnp.concatenate([win_ref[0, pl.ds(g, nw, stride=4), :].astype(bf16), new_rows(kvw_new_ref, g)], axis=0)
        vw = jnp.concatenate([win_ref[0, pl.ds(2 + g, nw, stride=4), :].astype(bf16), new_rows(kvw_new_ref, 2 + g)],
                             axis=0)
        wmask = (lax.broadcasted_iota(jnp.int32, (t_new, nw), 1) >= lax.broadcasted_iota(jnp.int32, (t_new, nw), 0))
        o_w = attend(q, kw, vw, bias_tail, jnp.concatenate([wmask, new_mask], axis=1))
        gate = gate_all[:, g * DH:(g + 1) * DH]
        for h in range(A_HPG):
            def gcol(br):
                return gate[:, br * A_HPG + h:br * A_HPG + h + 1]
            o = gcol(0) * o_c[h] + gcol(1) * o_s[h] + gcol(2) * o_w[h]
            outs.append(o * za[:, (g * A_HPG + h) * DH:(g * A_HPG + h + 1) * DH])
    o_ref[...] = jnp.concatenate(outs, axis=1)

    nrow = win_ref.shape[1]
    win_out_ref[0, 0:nrow - 4 * t_new, :] = win_ref[0, 4 * t_new:nrow, :]
    win_out_ref[0, nrow - 4 * t_new:nrow, :] = kvw_new_ref[...]


def _nsa_sample(qa, kvs_new, kvw_new, cache_cmp, cache_slc, cache_win, page_table, wk, wv, pek, pev, gk,
                gate, za, rel_bias, t_new):
    n, n_pages = page_table.shape
    page = cache_cmp.shape[1] // 4
    past = n_pages * page
    wb = cache_win.shape[1] // 4
    assert wb == WINDOW and past % QB == 0 and past >= WINDOW
    nchunk = past // CMP_STRIDE
    n_cmp = (past + t_new - CMP_BLOCK) // CMP_STRIDE + 1
    assert n_cmp <= nchunk - 1 or n_cmp == nchunk - 1
    n_slc = -(-(past + t_new) // SLC_BLOCK)
    nb_pad = -(-n_slc // 8) * 8
    tq = past + np.arange(t_new)[:, None]
    bias_c = _bias_lookup(rel_bias, tq - (np.arange(nchunk)[None, :] * CMP_STRIDE + CMP_BLOCK - 1))
    far = rel_bias[REL_BUCKETS - 1].astype(f32)[:, None, None]
    jt = np.arange(QB)[None, :]
    b_last = _bias_lookup(rel_bias, QB + np.arange(t_new)[:, None] - jt) - far
    b_new = _bias_lookup(rel_bias, np.arange(t_new)[:, None] - jt) - far
    mt = jnp.asarray(_cmp_to_slc_t(nchunk, nb_pad, n_cmp, n_slc))
    e = jnp.asarray((np.arange(past)[None, :] // SLC_BLOCK == np.arange(nb_pad)[:, None]).astype(np.float32), dtype=bf16)

    def page_spec(p):
        return pl.BlockSpec((1, page * 4, DH), lambda i, pt: (pt[i, p], 0, 0))

    rowblk = lambda w: pl.BlockSpec((t_new, w), lambda i, pt: (i, 0))
    full = lambda a: pl.BlockSpec(a.shape, lambda i, pt: (0,) * a.ndim)
    in_specs = ([page_spec(p) for p in range(n_pages)] * 2 + [
        rowblk(A_WIDTH),
        pl.BlockSpec((t_new * 4, DH), lambda i, pt: (i, 0)),
        pl.BlockSpec((t_new * 4, DH), lambda i, pt: (i, 0)),
        pl.BlockSpec((1, wb * 4, DH), lambda i, pt: (i, 0, 0)),
        full(wk), full(wv), full(pek), full(pev), full(gk), full(bias_c), full(b_last), full(b_new),
        rowblk(A_KV_GROUPS * DH), rowblk(A_WIDTH), full(mt), full(e)])
    grid_spec = pltpu.PrefetchScalarGridSpec(
        num_scalar_prefetch=1, grid=(n,), in_specs=in_specs,
        out_specs=[rowblk(A_WIDTH), pl.BlockSpec((1, wb * 4, DH), lambda i, pt: (i, 0, 0))],
        scratch_shapes=[pltpu.VMEM((4, past, DH), f32)])
    return pl.pallas_call(
        functools.partial(_nsa_sample_kernel, n_pages=n_pages, page=page, t_new=t_new, n_blocks=n_slc),
        grid_spec=grid_spec,
        out_shape=[jax.ShapeDtypeStruct((n * t_new, A_WIDTH), f32), jax.ShapeDtypeStruct(cache_win.shape, f32)],
        compiler_params=pltpu.CompilerParams(dimension_semantics=("arbitrary",), vmem_limit_bytes=VMEM_LIMIT),
        name="nsa_sample",
    )(page_table, *([cache_cmp] * n_pages), *([cache_slc] * n_pages), qa, kvs_new, kvw_new, cache_win,
      wk, wv, pek, pev, gk, bias_c, b_last, b_new, gate, za, mt, e)


def _outproj_kernel(x_ref, oa_ref, ob_ref, w_ref, y_ref):
    acc = jnp.dot(oa_ref[...].astype(bf16), w_ref[0:A_WIDTH, :], preferred_element_type=f32)
    acc = acc + jnp.dot(ob_ref[...].astype(bf16), w_ref[A_WIDTH:, :], preferred_element_type=f32)
    y_ref[...] = x_ref[...] + acc


def _outproj(x2d, oa, ob, w_out_bf, tm):
    m = x2d.shape[0]
    assert m % tm == 0
    row = lambda i: (i, 0)
    return pl.pallas_call(
        _outproj_kernel, grid=(m // tm,),
        in_specs=[pl.BlockSpec((tm, D_MODEL), row), pl.BlockSpec((tm, A_WIDTH), row), pl.BlockSpec((tm, B_WIDTH), row),
                  pl.BlockSpec(w_out_bf.shape, lambda i: (0, 0))],
        out_specs=pl.BlockSpec((tm, D_MODEL), row),
        out_shape=jax.ShapeDtypeStruct((m, D_MODEL), f32),
        compiler_params=pltpu.CompilerParams(dimension_semantics=("arbitrary",), vmem_limit_bytes=VMEM_LIMIT),
        name="outproj",
    )(x2d, oa, ob, w_out_bf)


def kernel(x_prompt, x_sample, cache_kv_cmp, cache_kv_slc, cache_kv_win, state_hgrn, page_table, g_norm, w_in, w_out,
           g_q, g_k_slc, g_k_win, g_k_cmp, w_cmp_k, w_cmp_v, pe_cmp_k, pe_cmp_v, rel_bias, lb_logits, g_o_hgrn):
    depth = w_in.shape[0]
    assert depth == 1, "single-layer trunk"
    nb, t, _ = x_prompt.shape
    ns, ts, _ = x_sample.shape
    row = lambda a: a.astype(f32)[None]
    lower = jnp.cumsum(jax.nn.softmax(lb_logits.astype(f32), axis=0), axis=0)[0]
    w_main, w_gate = _prep_proj_weights(w_in[0])
    w_out_bf = w_out[0].astype(bf16)
    wk, pek = _prep_cmp_weights(w_cmp_k[0], pe_cmp_k[0])
    wv, pev = _prep_cmp_weights(w_cmp_v[0], pe_cmp_v[0])
    proj_args = (row(g_norm[0]), w_main, w_gate, row(g_q[0]), row(g_k_slc[0]), row(g_k_win[0]), row(lower))
    kv6 = lambda a, n_, t_: a.reshape(1, n_, t_, 2, A_KV_GROUPS, DH)

    xp = x_prompt.reshape(nb * t, D_MODEL)
    qa, kvc, kvs, kvw, gate, za, qb, f, ib, zb = _proj(xp, *proj_args, tm=512)
    s0 = jnp.zeros((nb, B_HEADS, DH, DH), f32)
    ob, st_p = _hgrn(qb, f, ib, zb, s0, row(g_o_hgrn[0]), nb, t, tc=128, sc=16)
    kc, vc = _compress(kvc, wk, wv, pek, pev, row(g_k_cmp[0]), nb, t)
    oa = _nsa_prompt(qa, kc, vc, kvs, kvw, gate, za, rel_bias, nb, t)
    y_p = _outproj(xp, oa, ob, w_out_bf, tm=256).reshape(nb, t, D_MODEL)
    wlen = min(WINDOW, t)
    win_p = kv6(kvw, nb, t)[:, :, t - wlen:]

    xs = x_sample.reshape(ns * ts, D_MODEL)
    qa, kvc_s, kvs_s, kvw_s, gate, za, qb, f, ib, zb = _proj(xs, *proj_args, tm=512)
    ob, st_s = _hgrn(qb, f, ib, zb, state_hgrn[0].astype(f32), row(g_o_hgrn[0]), ns, ts, tc=ts, sc=ts)
    pool = cache_kv_cmp.shape[1]
    page = cache_kv_cmp.shape[2]
    oa, win_s = _nsa_sample(
        qa, kvs_s, kvw_s, cache_kv_cmp[0].reshape(pool, page * 4, DH), cache_kv_slc[0].reshape(pool, page * 4, DH),
        cache_kv_win[0].reshape(ns, -1, DH), page_table, wk, wv, pek, pev, row(g_k_cmp[0]), gate, za, rel_bias, ts)
    y_s = _outproj(xs, oa, ob, w_out_bf, tm=256).reshape(ns, ts, D_MODEL)

    return (y_p, y_s, kv6(kvc, nb, t), kv6(kvs, nb, t), win_p, st_p[None].astype(x_prompt.dtype),
            kv6(kvc_s, ns, ts), kv6(kvs_s, ns, ts), kv6(win_s, ns, WINDOW), st_s[None].astype(state_hgrn.dtype))
```

```python
import functools
import math

import jax
import jax.numpy as jnp
import numpy as np
from jax import lax
from jax.experimental import pallas as pl
from jax.experimental.pallas import tpu as pltpu

f32 = jnp.float32
bf16 = jnp.bfloat16

D_MODEL = 2048
A_HEADS = 8
A_KV_GROUPS = 2
A_HPG = A_HEADS // A_KV_GROUPS
DH = 128
A_WIDTH = A_HEADS * DH
CMP_BLOCK = 32
CMP_STRIDE = 16
SLC_BLOCK = 64
N_SELECT = 16
WINDOW = 512
B_HEADS = 8
B_WIDTH = B_HEADS * DH
REL_BUCKETS = 32
REL_MAX_DIST = 128
EPS = 1e-6
NEG = -1e30

VMEM_LIMIT = 56 * 1024 * 1024


def _sigmoid(x):
    return 1.0 / (1.0 + jnp.exp(-x))


def _head_norm(a, g):
    return a * lax.rsqrt(jnp.mean(a * a, axis=-1, keepdims=True) + EPS) * g


PROJ_TN = 512
PROJ_NT = 15


def _proj_kernel(x_ref, gn_ref, w_ref, wg_ref, gq_ref, gks_ref, gkw_ref, lb_ref,
                 qa_ref, kvc_ref, kvs_ref, kvw_ref, gate_ref, za_ref, qb_ref, f_ref, ib_ref, zb_ref,
                 h_scr):
    j = pl.program_id(1)
    tm = x_ref.shape[0]

    @pl.when(j == 0)
    def _():
        x = x_ref[...]
        h = x * lax.rsqrt(jnp.mean(x * x, axis=-1, keepdims=True) + EPS) * gn_ref[...]
        hb = h.astype(bf16)
        h_scr[...] = hb
        gate_ref[...] = _sigmoid(jnp.dot(hb, wg_ref[...], preferred_element_type=f32))

    acc = jnp.dot(h_scr[...], w_ref[...], preferred_element_type=f32)

    def cols(c):
        return acc[:, c * DH:(c + 1) * DH]

    @pl.when(j < 2)
    def _():
        gq = gq_ref[...]
        qa_ref[...] = jnp.concatenate(
            [_head_norm(cols(c), gq) * (DH ** -0.5) for c in range(4)], axis=1).astype(qa_ref.dtype)

    @pl.when(j == 2)
    def _():
        for c in range(4):
            kvc_ref[pl.ds(c, tm, stride=4), :] = cols(c)

    @pl.when(j == 3)
    def _():
        g = gks_ref[...]
        for c in range(4):
            kvs_ref[pl.ds(c, tm, stride=4), :] = _head_norm(cols(c), g) if c < 2 else cols(c)

    @pl.when(j == 4)
    def _():
        g = gkw_ref[...]
        for c in range(4):
            kvw_ref[pl.ds(c, tm, stride=4), :] = _head_norm(cols(c), g) if c < 2 else cols(c)

    @pl.when((j >= 5) & (j < 7))
    def _():
        za_ref[...] = acc * _sigmoid(acc)

    @pl.when((j >= 7) & (j < 9))
    def _():
        qb_ref[...] = acc

    for jj in range(2):
        @pl.when(j == 9 + jj)
        def _():
            lb = lb_ref[:, jj * PROJ_TN:(jj + 1) * PROJ_TN]
            f_ref[...] = lb + (1.0 - lb) * _sigmoid(acc)

    @pl.when((j >= 11) & (j < 13))
    def _():
        ib_ref[...] = acc

    @pl.when(j >= 13)
    def _():
        zb_ref[...] = acc * _sigmoid(acc)


def _proj(x2d, gn, w_main, w_gate, gq, gks, gkw, lb, tm):
    m = x2d.shape[0]
    assert m % tm == 0
    grid = (m // tm, PROJ_NT)

    def seg(start):
        return lambda i, j: (i, jnp.clip(j - start, 0, 1))

    row = lambda i, j: (i, 0)
    const = lambda i, j: (0, 0)
    out_shape = [
        jax.ShapeDtypeStruct((m, A_WIDTH), bf16),
        jax.ShapeDtypeStruct((m * 4, DH), f32),
        jax.ShapeDtypeStruct((m * 4, DH), f32),
        jax.ShapeDtypeStruct((m * 4, DH), f32),
        jax.ShapeDtypeStruct((m, A_KV_GROUPS * DH), f32),
        jax.ShapeDtypeStruct((m, A_WIDTH), f32),
        jax.ShapeDtypeStruct((m, B_WIDTH), f32),
        jax.ShapeDtypeStruct((m, B_WIDTH), f32),
        jax.ShapeDtypeStruct((m, B_WIDTH), f32),
        jax.ShapeDtypeStruct((m, B_WIDTH), f32),
    ]
    out_specs = [
        pl.BlockSpec((tm, PROJ_TN), seg(0)),
        pl.BlockSpec((tm * 4, DH), row),
        pl.BlockSpec((tm * 4, DH), row),
        pl.BlockSpec((tm * 4, DH), row),
        pl.BlockSpec((tm, A_KV_GROUPS * DH), row),
        pl.BlockSpec((tm, PROJ_TN), seg(5)),
        pl.BlockSpec((tm, PROJ_TN), seg(7)),
        pl.BlockSpec((tm, PROJ_TN), seg(9)),
        pl.BlockSpec((tm, PROJ_TN), seg(11)),
        pl.BlockSpec((tm, PROJ_TN), seg(13)),
    ]
    in_specs = [
        pl.BlockSpec((tm, D_MODEL), row),
        pl.BlockSpec((1, D_MODEL), const),
        pl.BlockSpec((D_MODEL, PROJ_TN), lambda i, j: (0, j)),
        pl.BlockSpec((D_MODEL, A_KV_GROUPS * DH), const),
        pl.BlockSpec((1, DH), const),
        pl.BlockSpec((1, DH), const),
        pl.BlockSpec((1, DH), const),
        pl.BlockSpec((1, B_WIDTH), const),
    ]
    return pl.pallas_call(
        _proj_kernel, grid=grid, in_specs=in_specs, out_specs=out_specs, out_shape=out_shape,
        scratch_shapes=[pltpu.VMEM((tm, D_MODEL), bf16)],
        compiler_params=pltpu.CompilerParams(
            dimension_semantics=("arbitrary", "arbitrary"), vmem_limit_bytes=VMEM_LIMIT),
        name="proj",
    )(x2d, gn, w_main, w_gate, gq, gks, gkw, lb)


def _prep_proj_weights(w_in):
    a0 = A_WIDTH + 6 * A_KV_GROUPS * DH
    a1 = a0 + 3 * A_HEADS
    w_main = jnp.concatenate([w_in[:, :a0], w_in[:, a1:]], axis=1).astype(bf16)
    wg = w_in[:, a0:a1].reshape(-1, 3, A_KV_GROUPS, A_HPG).transpose(0, 2, 1, 3).reshape(-1, A_KV_GROUPS, 3 * A_HPG)
    w_gate = jnp.pad(wg, ((0, 0), (0, 0), (0, DH - 3 * A_HPG))).reshape(-1, A_KV_GROUPS * DH).astype(bf16)
    return w_main, w_gate


HGRN_SAFE_LOG_DECAY = 80.0


def _hgrn_exact(q, k, v, b, st):
    sc = q.shape[0]
    t_idx = lax.broadcasted_iota(jnp.int32, (sc, DH), 0)
    o = lax.dot_general((q * jnp.exp(b)).astype(bf16), st.astype(bf16), (((1,), (1,)), ((), ())),
                        preferred_element_type=f32)
    for s in range(sc):
        e = jnp.exp(jnp.minimum(b - b[s:s + 1, :], 0.0))
        a = jnp.where(t_idx >= s, q * e * k[s:s + 1, :], 0.0)
        o = o + jnp.sum(a, axis=-1, keepdims=True) * v[s:s + 1, :]
    bl = b[sc - 1:sc, :]
    ut = lax.dot_general(v.astype(bf16), (k * jnp.exp(bl - b)).astype(bf16), (((0,), (0,)), ((), ())),
                         preferred_element_type=f32)
    return o, st * jnp.exp(bl) + ut


def _hgrn_factored(q, k, v, b, st):
    fc = q.shape[0]
    qd = (q * jnp.exp(b)).astype(bf16)
    att = lax.dot_general(qd, (k * jnp.exp(-b)).astype(bf16), (((1,), (1,)), ((), ())), preferred_element_type=f32)
    tri = lax.broadcasted_iota(jnp.int32, (fc, fc), 0) >= lax.broadcasted_iota(jnp.int32, (fc, fc), 1)
    att = jnp.where(tri, att, 0.0)
    o = jnp.dot(att.astype(bf16), v.astype(bf16), preferred_element_type=f32)
    o = o + lax.dot_general(qd, st.astype(bf16), (((1,), (1,)), ((), ())), preferred_element_type=f32)
    bl = b[fc - 1:fc, :]
    ut = lax.dot_general(v.astype(bf16), (k * jnp.exp(bl - b)).astype(bf16), (((0,), (0,)), ((), ())),
                         preferred_element_type=f32)
    return o, st * jnp.exp(bl) + ut


def _hgrn_kernel(q_ref, f_ref, v_ref, zs_ref, s0_ref, go_ref, tri_ref, o_ref, sout_ref, st_scr, *, sc, fc):
    c = pl.program_id(1)
    tc = q_ref.shape[0]

    @pl.when(c == 0)
    def _():
        for h in range(B_HEADS):
            st_scr[h] = s0_ref[0, h].T

    f = f_ref[...]
    b_all = lax.dot_general(tri_ref[...], jnp.log(f), (((1,), (0,)), ((), ())),
                            precision=lax.Precision.HIGHEST, preferred_element_type=f32)
    go = go_ref[...]

    def finish(rows, lanes, o):
        o_ref[rows, lanes] = _head_norm(o, go) * zs_ref[rows, lanes]

    def exact_rows(r0, n_rows, rebase):
        for h in range(B_HEADS):
            lanes = slice(h * DH, (h + 1) * DH)
            outs = []
            for i in range(n_rows // sc):
                rows = slice(r0 + i * sc, r0 + (i + 1) * sc)
                b = b_all[rows, lanes]
                if rebase and i > 0:
                    b = b - b_all[r0 + i * sc - 1:r0 + i * sc, lanes]
                o, st = _hgrn_exact(q_ref[rows, lanes], 1.0 - f[rows, lanes], v_ref[rows, lanes], b, st_scr[h])
                st_scr[h] = st
                outs.append(o)
            finish(slice(r0, r0 + n_rows), lanes, jnp.concatenate(outs, axis=0) if len(outs) > 1 else outs[0])

    if fc is None:
        exact_rows(0, tc, False)
    else:
        for i in range(tc // fc):
            r0 = i * fc
            rows = slice(r0, r0 + fc)
            total = jnp.max(-b_all[r0 + fc - 1:r0 + fc, :])

            @pl.when(total < HGRN_SAFE_LOG_DECAY)
            def _():
                for h in range(B_HEADS):
                    lanes = slice(h * DH, (h + 1) * DH)
                    o, st = _hgrn_factored(q_ref[rows, lanes], 1.0 - f[rows, lanes], v_ref[rows, lanes],
                                           b_all[rows, lanes], st_scr[h])
                    st_scr[h] = st
                    finish(rows, lanes, o)

            @pl.when(jnp.logical_not(total < HGRN_SAFE_LOG_DECAY))
            def _():
                exact_rows(r0, fc, True)

    @pl.when(c == pl.num_programs(1) - 1)
    def _():
        for h in range(B_HEADS):
            sout_ref[0, h] = st_scr[h].T


def _block_tri(tc, blk):
    r = np.arange(tc)
    return jnp.asarray(((r[:, None] // blk == r[None, :] // blk) & (r[None, :] <= r[:, None])).astype(np.float32))


def _hgrn(qb, f, ib, zs, s0, go, n, t, tc, sc, fc=None):
    assert t % tc == 0 and tc % sc == 0 and (fc is None or (tc % fc == 0 and fc % sc == 0))
    nc = t // tc
    rows = lambda i, c: (i * nc + c, 0)
    st = lambda i, c: (i, 0, 0, 0)
    const = lambda i, c: (0, 0)
    return pl.pallas_call(
        functools.partial(_hgrn_kernel, sc=sc, fc=fc),
        grid=(n, nc),
        in_specs=[pl.BlockSpec((tc, B_WIDTH), rows)] * 4 + [
            pl.BlockSpec((1, B_HEADS, DH, DH), st),
            pl.BlockSpec((1, DH), const),
            pl.BlockSpec((tc, tc), const),
        ],
        out_specs=[pl.BlockSpec((tc, B_WIDTH), rows), pl.BlockSpec((1, B_HEADS, DH, DH), st)],
        out_shape=[jax.ShapeDtypeStruct((n * t, B_WIDTH), f32), jax.ShapeDtypeStruct((n, B_HEADS, DH, DH), f32)],
        scratch_shapes=[pltpu.VMEM((B_HEADS, DH, DH), f32)],
        compiler_params=pltpu.CompilerParams(
            dimension_semantics=("arbitrary", "arbitrary"), vmem_limit_bytes=VMEM_LIMIT),
        name="hgrn",
    )(qb, f, ib, zs, s0, go, _block_tri(tc, sc if fc is None else fc))


def _rel_bucket_np(dist):
    n = np.maximum(dist, 0)
    exact = REL_BUCKETS // 2
    scale = np.float32((REL_BUCKETS - exact) / math.log(REL_MAX_DIST / exact))
    large = exact + (np.log(np.maximum(n, exact).astype(np.float32) / np.float32(exact)) * scale).astype(np.int32)
    return np.where(n < exact, n, np.minimum(large, REL_BUCKETS - 1)).astype(np.int32)


def _bias_by_distance(rel_bias, d_max):
    return jnp.take(rel_bias.astype(f32), jnp.asarray(_rel_bucket_np(np.arange(d_max))), axis=0).T


def _shifted_rows(w, n_rows, step):
    h, p = w.shape
    return jnp.tile(w, (1, n_rows + 1))[:, :n_rows * (p - step)].reshape(h, n_rows, p - step)


def _cmp_bias_t(rel_bias, t0, nt, n_chunk):
    off = CMP_STRIDE * n_chunk
    p = -(-(t0 + nt + off) // CMP_STRIDE) * CMP_STRIDE + CMP_STRIDE
    tab = _bias_by_distance(rel_bias, t0 + nt)
    k = np.clip(np.arange(p) - off - (CMP_BLOCK - 1), 0, t0 + nt - 1)
    rows = _shifted_rows(tab[:, k], n_chunk, CMP_STRIDE)
    return rows[:, :, off + t0:off + t0 + nt]


def _near_tiles_t(rel_bias):
    tab = _bias_by_distance(rel_bias, 2 * QB) - rel_bias[REL_BUCKETS - 1].astype(f32)[:, None]
    k = np.clip(np.arange(3 * QB) - (QB - 1), 0, 2 * QB - 1)
    rows = _shifted_rows(tab[:, k], QB, 1)
    return jnp.stack([rows[:, :, QB - 1:2 * QB - 1], rows[:, :, 2 * QB - 1:3 * QB - 1]], axis=1)


def _compress_combine(lhs, wcat, pe2):
    out = jnp.dot(lhs, wcat, preferred_element_type=f32)
    pc = jnp.dot(pe2, wcat, preferred_element_type=f32)
    const = pc[0:1, :DH] + pc[1:2, DH:]
    r = lhs.shape[0]
    return out[:, :DH] + pltpu.roll(out[:, DH:], r - 1, 0) + const


def _compress_kernel(kv_ref, wk_ref, wv_ref, pek_ref, pev_ref, gk_ref, kc_ref, vct_ref):
    nchunk = kv_ref.shape[0] // (4 * CMP_STRIDE)

    def chunk_rows(cg):
        return jnp.concatenate(
            [kv_ref[pl.ds(4 * l + cg, nchunk, stride=4 * CMP_STRIDE), :].astype(bf16) for l in range(CMP_STRIDE)],
            axis=1)

    kc = _compress_combine(jnp.concatenate([chunk_rows(0), chunk_rows(1)], axis=0), wk_ref[...], pek_ref[...])
    vc = _compress_combine(jnp.concatenate([chunk_rows(2), chunk_rows(3)], axis=0), wv_ref[...], pev_ref[...])
    kc = _head_norm(kc, gk_ref[...])
    for g in range(A_KV_GROUPS):
        kc_ref[0, g] = kc[g * nchunk:(g + 1) * nchunk].astype(kc_ref.dtype)
        vct_ref[0, g] = vc[g * nchunk:(g + 1) * nchunk].T.astype(vct_ref.dtype)


def _prep_cmp_weights(w, pe):
    half = CMP_STRIDE * DH
    wcat = jnp.concatenate([w[:half], w[half:]], axis=1).astype(bf16)
    return wcat, pe.reshape(2, half).astype(bf16)


def _compress(kvc, wk, wv, pek, pev, gk, n, t):
    nchunk = t // CMP_STRIDE
    assert nchunk == DH, "the transposed v_c block is square"
    const = lambda i: (0, 0)
    return pl.pallas_call(
        _compress_kernel, grid=(n,),
        in_specs=[pl.BlockSpec((t * 4, DH), lambda i: (i, 0)),
                  pl.BlockSpec(wk.shape, const), pl.BlockSpec(wv.shape, const),
                  pl.BlockSpec(pek.shape, const), pl.BlockSpec(pev.shape, const), pl.BlockSpec((1, DH), const)],
        out_specs=[pl.BlockSpec((1, A_KV_GROUPS, nchunk, DH), lambda i: (i, 0, 0, 0))] * 2,
        out_shape=[jax.ShapeDtypeStruct((n, A_KV_GROUPS, nchunk, DH), bf16)] * 2,
        compiler_params=pltpu.CompilerParams(dimension_semantics=("arbitrary",), vmem_limit_bytes=VMEM_LIMIT),
        name="compress",
    )(kvc, wk, wv, pek, pev, gk)


def _softmax_tile(carry, s, mask, v_bf):
    m, l, acc = carry
    s = jnp.where(mask, s, NEG)
    m_new = jnp.maximum(m, jnp.max(s, axis=-1, keepdims=True))
    alpha = jnp.exp(m - m_new)
    p = jnp.where(mask, jnp.exp(s - m_new), 0.0)
    l = alpha * l + jnp.sum(p, axis=-1, keepdims=True)
    h, r, k = p.shape
    pv = jnp.dot(p.reshape(h * r, k).astype(bf16), v_bf, preferred_element_type=f32).reshape(h, r, DH)
    return m_new, l, alpha * acc + pv


def _softmax_finish(carry):
    m, l, acc = carry
    return acc / jnp.where(l > 0, l, 1.0)


def _select_blocks(score_t, tpos, n_blocks):
    nb = score_t.shape[0]
    j = lax.broadcasted_iota(jnp.int32, score_t.shape, 0)
    valid = (j * SLC_BLOCK <= tpos) & (j < n_blocks)
    cur = tpos >> 6
    forced = (j == 0) | (j == cur) | (j == cur - 1)
    val = jnp.where(valid, jnp.where(forced, 1e30, score_t), -1.0)
    rank = jnp.zeros(score_t.shape, jnp.int32)
    for i in range(n_blocks):
        vi = val[i:i + 1, :]
        beats = (vi > val) | ((vi == val) & (i < j))
        rank = rank + beats.astype(jnp.int32)
    return jnp.where((rank < N_SELECT) & valid, 1.0, 0.0)


QB = 128


NEG_M = -1e30
NEG_S = -2e30


def _nsa_prompt_kernel(q_ref, kc_ref, vct_ref, bc_ref, kvs_ref, kvw_ref, gate_ref, za_ref, bt_ref, mt_ref, et_ref,
                       o_ref, ks_scr, vts_scr, kw_scr, vtw_scr, km_scr):
    qb = pl.program_id(1)
    q0 = qb * QB
    nkt = vts_scr.shape[1]
    nq4 = A_HPG * QB
    groups = range(A_KV_GROUPS)

    @pl.when(qb == 0)
    def _():
        for g in groups:
            for kt in range(nkt):
                rows = slice(kt * QB, (kt + 1) * QB)
                base = kt * 4 * QB
                ks_scr[g, rows, :] = kvs_ref[pl.ds(base + g, QB, stride=4), :].astype(bf16)
                vts_scr[g, kt] = kvs_ref[pl.ds(base + 2 + g, QB, stride=4), :].T.astype(bf16)
                kw_scr[g, rows, :] = kvw_ref[pl.ds(base + g, QB, stride=4), :].astype(bf16)
                vtw_scr[g, kt] = kvw_ref[pl.ds(base + 2 + g, QB, stride=4), :].T.astype(bf16)

    qall = q_ref[...]
    qts = [jnp.concatenate([qall[:, (g * A_HPG + h) * DH:(g * A_HPG + h + 1) * DH].astype(f32).T
                            for h in range(A_HPG)], axis=1).astype(bf16) for g in groups]
    kk = lax.broadcasted_iota(jnp.int32, (QB, nq4), 0)
    tt = lax.broadcasted_iota(jnp.int32, (QB, nq4), 1) & (QB - 1)

    def heads(ref, g, *idx):
        return jnp.concatenate([ref[(g * A_HPG + h,) + idx] for h in range(A_HPG)], axis=1)

    o_c = []
    for g in groups:
        sc = jnp.dot(kc_ref[0, g], qts[g], preferred_element_type=f32) + heads(bc_ref, g)
        mask_c = q0 + tt >= CMP_STRIDE * kk + (CMP_BLOCK - 1)
        sc = jnp.where(mask_c, sc, NEG)
        mc = jnp.max(sc, axis=0, keepdims=True)
        ec = jnp.where(mask_c, jnp.exp(sc - mc), 0.0)
        lc = jnp.sum(ec, axis=0, keepdims=True)
        pc = ec / jnp.where(lc > 0, lc, 1.0)
        o_c.append(jnp.dot(vct_ref[0, g], pc.astype(bf16), preferred_element_type=f32))
        ps = pc[:, 0:QB] + pc[:, QB:2 * QB] + pc[:, 2 * QB:3 * QB] + pc[:, 3 * QB:4 * QB]
        score_t = jnp.dot(mt_ref[...], ps, precision=lax.Precision.HIGHEST, preferred_element_type=f32)
        nb = score_t.shape[0]
        sel_t = _select_blocks(score_t, q0 + lax.broadcasted_iota(jnp.int32, (nb, QB), 1), nb)
        km_scr[g] = jnp.dot(et_ref[...], sel_t.astype(bf16), preferred_element_type=f32)

    def tile(carry, qt, k_tile, vt_tile, mask, bias):
        m, l, acc = carry
        s = jnp.dot(k_tile, qt, preferred_element_type=f32)
        if bias is not None:
            s = s + bias
        s = jnp.where(mask, s, NEG_S)
        m_new = jnp.maximum(m, jnp.max(s, axis=0, keepdims=True))
        alpha = jnp.exp(m - m_new)
        p = jnp.exp(s - m_new)
        l = alpha * l + jnp.sum(p, axis=0, keepdims=True)
        return m_new, l, alpha * acc + jnp.dot(vt_tile, p.astype(bf16), preferred_element_type=f32)

    def init():
        return jnp.full((1, nq4), NEG_M, f32), jnp.zeros((1, nq4), f32), jnp.zeros((DH, nq4), f32)

    def finish(carry):
        m, l, acc = carry
        return acc / jnp.where(l > 0, l, 1.0)

    def key_rows(kt):
        return pl.ds(pl.multiple_of(kt * QB, QB), QB)

    def lanes4(x):
        return jnp.concatenate([x] * A_HPG, axis=1)

    def near_span(g, k_scr, vt_scr, rs, use_sel):
        ks, vts, masks, biases = [], [], [], []
        for r in rs:
            kt = jnp.maximum(qb - r, 0)
            ks.append(k_scr[g, key_rows(kt), :])
            vts.append(vt_scr[g, kt])
            mask = kk * 0 + (qb - r) >= 0
            if use_sel:
                mask = mask & (lanes4(km_scr[g, key_rows(kt), :]) > 0.5)
            if r == 0:
                mask = mask & (kk <= tt)
            if r == WINDOW // QB:
                mask = mask & (kk >= tt)
            masks.append(mask)
            biases.append(heads(bt_ref, g, r) if r < 2 else jnp.zeros((QB, nq4), f32))
        cat = lambda xs, axis: jnp.concatenate(xs, axis=axis)
        return cat(ks, 0), cat(vts, 1), cat(masks, 0), cat(biases, 0)

    n_far = jnp.maximum(qb - 1, 0)
    kk2 = lax.broadcasted_iota(jnp.int32, (2 * QB, nq4), 0)

    def far(i, carries):
        rows = pl.ds(pl.multiple_of(i * 2 * QB, 2 * QB), 2 * QB)
        in_range = (kk2 + i * 2 * QB) < n_far * QB
        out = []
        for g in groups:
            mask = (lanes4(km_scr[g, rows, :]) > 0.5) & in_range
            vt = jnp.concatenate([vts_scr[g, 2 * i], vts_scr[g, jnp.minimum(2 * i + 1, nkt - 1)]], axis=1)
            out.append(tile(carries[g], qts[g], ks_scr[g, rows, :], vt, mask, None))
        return tuple(out)

    carries = lax.fori_loop(0, (n_far + 1) // 2, far, tuple(init() for g in groups))
    o_s = [finish(tile(carries[g], qts[g], *near_span(g, ks_scr, vts_scr, (1, 0), True))) for g in groups]

    o_w = [finish(tile(init(), qts[g], *near_span(g, kw_scr, vtw_scr, (4, 3, 2, 1, 0), False))) for g in groups]

    za = za_ref[...]
    outs = []
    for g in groups:
        gate_t = gate_ref[:, g * DH:(g + 1) * DH].T
        for h in range(A_HPG):
            cols = slice(h * QB, (h + 1) * QB)

            def grow(br):
                return gate_t[br * A_HPG + h:br * A_HPG + h + 1, :]
            o = grow(0) * o_c[g][:, cols] + grow(1) * o_s[g][:, cols] + grow(2) * o_w[g][:, cols]
            outs.append(o.T * za[:, (g * A_HPG + h) * DH:(g * A_HPG + h + 1) * DH])
    o_ref[...] = jnp.concatenate(outs, axis=1)


def _cmp_to_slc_t(n_cmp_pad, n_slc_pad, n_cmp, n_slc):
    c0 = np.arange(n_cmp_pad)[None, :] * CMP_STRIDE
    s0 = np.arange(n_slc_pad)[:, None] * SLC_BLOCK
    ov = np.minimum(c0 + CMP_BLOCK, s0 + SLC_BLOCK) - np.maximum(c0, s0)
    m = np.maximum(ov, 0).astype(np.float32) / CMP_STRIDE
    m[:, n_cmp:] = 0
    m[n_slc:, :] = 0
    return m


def _nsa_prompt(qa, kc, vct, kvs, kvw, gate, za, rel_bias, n, t):
    nq = t // QB
    n_cmp = (t - CMP_BLOCK) // CMP_STRIDE + 1
    n_slc = -(-t // SLC_BLOCK)
    assert kc.shape[2] == QB and n_slc % 8 == 0
    bias_c = _cmp_bias_t(rel_bias, 0, t, QB)
    bt = _near_tiles_t(rel_bias)
    mt = jnp.asarray(_cmp_to_slc_t(QB, n_slc, n_cmp, n_slc))
    et = jnp.asarray((np.arange(t)[:, None] // SLC_BLOCK == np.arange(n_slc)[None, :]).astype(np.float32), dtype=bf16)
    qrow = lambda i, b: (i * nq + b, 0)
    seq = lambda i, b: (i, 0, 0, 0)
    ng = A_KV_GROUPS
    return pl.pallas_call(
        _nsa_prompt_kernel, grid=(n, nq),
        in_specs=[
            pl.BlockSpec((QB, A_WIDTH), qrow),
            pl.BlockSpec((1, ng, QB, DH), seq),
            pl.BlockSpec((1, ng, DH, QB), seq),
            pl.BlockSpec((A_HEADS, QB, QB), lambda i, b: (0, 0, b)),
            pl.BlockSpec((t * 4, DH), lambda i, b: (i, 0)),
            pl.BlockSpec((t * 4, DH), lambda i, b: (i, 0)),
            pl.BlockSpec((QB, ng * DH), qrow),
            pl.BlockSpec((QB, A_WIDTH), qrow),
            pl.BlockSpec((A_HEADS, 2, QB, QB), lambda i, b: (0, 0, 0, 0)),
            pl.BlockSpec(mt.shape, lambda i, b: (0, 0)),
            pl.BlockSpec(et.shape, lambda i, b: (0, 0)),
        ],
        out_specs=pl.BlockSpec((QB, A_WIDTH), qrow),
        out_shape=jax.ShapeDtypeStruct((n * t, A_WIDTH), f32),
        scratch_shapes=[pltpu.VMEM((ng, t, DH), bf16), pltpu.VMEM((ng, nq, DH, QB), bf16),
                        pltpu.VMEM((ng, t, DH), bf16), pltpu.VMEM((ng, nq, DH, QB), bf16),
                        pltpu.VMEM((ng, t, QB), f32)],
        compiler_params=pltpu.CompilerParams(
            dimension_semantics=("arbitrary", "arbitrary"), vmem_limit_bytes=VMEM_LIMIT),
        name="nsa_prompt",
    )(qa, kc, vct, bias_c, kvs, kvw, gate, za, bt, mt, et)


CMP_PITCH = 24


def _nsa_sample_kernel(pt_ref, *refs, n_pages, page, t_new, n_blocks):
    del pt_ref
    cmp_pages = refs[:n_pages]
    slc_pages = refs[n_pages:2 * n_pages]
    (q_ref, kvs_new_ref, kvw_new_ref, win_ref, wk_ref, wv_ref, pek_ref, pev_ref, gk_ref, bc_ref, bl_ref, bn_ref,
     gate_ref, za_ref, mt_ref, e_ref, o_ref, win_out_ref, xc_scr) = refs[2 * n_pages:]
    past = n_pages * page
    nchunk = past // CMP_STRIDE
    pad_new = QB
    rows_q = A_HPG * t_new

    cpp = page // CMP_STRIDE
    for p in range(n_pages):
        for cg in range(4):
            x = cmp_pages[p][0, pl.ds(cg, page, stride=4), :]
            for c in range(cpp):
                r0 = (p * cpp + c) * CMP_PITCH
                xc_scr[cg, r0:r0 + CMP_STRIDE, :] = x[c * CMP_STRIDE:(c + 1) * CMP_STRIDE]

    def chunk_rows(cg):
        return jnp.concatenate(
            [xc_scr[cg, pl.ds(l, nchunk, stride=CMP_PITCH), :].astype(bf16) for l in range(CMP_STRIDE)], axis=1)

    kc = _compress_combine(jnp.concatenate([chunk_rows(0), chunk_rows(1)], axis=0), wk_ref[...], pek_ref[...])
    vc = _compress_combine(jnp.concatenate([chunk_rows(2), chunk_rows(3)], axis=0), wv_ref[...], pev_ref[...])
    kc = _head_norm(kc, gk_ref[...]).astype(bf16)
    vc = vc.astype(bf16)

    tt = lax.broadcasted_iota(jnp.int32, (t_new, pad_new), 0)
    uu = lax.broadcasted_iota(jnp.int32, (t_new, pad_new), 1)
    new_mask = (uu <= tt)
    qall = q_ref[...]
    gate_all = gate_ref[...]
    za = za_ref[...]
    zeros_pad = jnp.zeros((pad_new - t_new, DH), bf16)

    def new_rows(ref, c):
        return jnp.concatenate([ref[pl.ds(c, t_new, stride=4), :].astype(bf16), zeros_pad], axis=0)

    def attend(q, k_all, v_all, bias_tail, mask):
        nk = k_all.shape[0]
        s = lax.dot_general(q, k_all, (((1,), (1,)), ((), ())), preferred_element_type=f32).reshape(A_HPG, t_new, nk)
        s = jnp.concatenate([s[:, :, :nk - 2 * QB], s[:, :, nk - 2 * QB:] + bias_tail], axis=-1)
        carry = (jnp.full((A_HPG, t_new, 1), NEG, f32), jnp.zeros((A_HPG, t_new, 1), f32),
                 jnp.zeros((A_HPG, t_new, DH), f32))
        return _softmax_finish(_softmax_tile(carry, s, mask[None], v_all))

    outs = []
    for g in range(A_KV_GROUPS):
        q = jnp.concatenate([qall[:, (g * A_HPG + h) * DH:(g * A_HPG + h + 1) * DH] for h in range(A_HPG)], axis=0)
        cmask = (past + lax.broadcasted_iota(jnp.int32, (t_new, nchunk), 0)
                 >= CMP_STRIDE * lax.broadcasted_iota(jnp.int32, (t_new, nchunk), 1) + (CMP_BLOCK - 1))[None]
        sc = lax.dot_general(q, kc[g * nchunk:(g + 1) * nchunk], (((1,), (1,)), ((), ())),
                             preferred_element_type=f32).reshape(A_HPG, t_new, nchunk) + bc_ref[g * A_HPG:(g + 1) * A_HPG]
        sc = jnp.where(cmask, sc, NEG)
        mc = jnp.max(sc, axis=-1, keepdims=True)
        ec = jnp.where(cmask, jnp.exp(sc - mc), 0.0)
        lc = jnp.sum(ec, axis=-1, keepdims=True)
        pc = ec / jnp.where(lc > 0, lc, 1.0)
        o_c = jnp.dot(pc.reshape(rows_q, nchunk).astype(bf16), vc[g * nchunk:(g + 1) * nchunk],
                      preferred_element_type=f32).reshape(A_HPG, t_new, DH)
        ps = pc[0] + pc[1] + pc[2] + pc[3]
        score_t = lax.dot_general(mt_ref[...], ps, (((1,), (1,)), ((), ())),
                                  precision=lax.Precision.HIGHEST, preferred_element_type=f32)
        tpos = past + lax.broadcasted_iota(jnp.int32, score_t.shape, 1)
        sel_t = _select_blocks(score_t, tpos, n_blocks)
        key_mask = lax.dot_general(sel_t.astype(bf16), e_ref[...], (((0,), (0,)), ((), ())),
                                   preferred_element_type=f32) > 0.5
        k_all = jnp.concatenate([slc_pages[p][0, pl.ds(g, page, stride=4), :].astype(bf16) for p in range(n_pages)]
                                + [new_rows(kvs_new_ref, g)], axis=0)
        v_all = jnp.concatenate([slc_pages[p][0, pl.ds(2 + g, page, stride=4), :].astype(bf16) for p in range(n_pages)]
                                + [new_rows(kvs_new_ref, 2 + g)], axis=0)
        bias_tail = jnp.concatenate([bl_ref[g * A_HPG:(g + 1) * A_HPG], bn_ref[g * A_HPG:(g + 1) * A_HPG]], axis=-1)
        o_s = attend(q, k_all, v_all, bias_tail, jnp.concatenate([key_mask, new_mask], axis=1))
        nw = win_ref.shape[1] // 4
        kw = jnp.concatenate([win_ref[0, pl.ds(g, nw, stride=4), :].astype(bf16), new_rows(kvw_new_ref, g)], axis=0)
        vw = jnp.concatenate([win_ref[0, pl.ds(2 + g, nw, stride=4), :].astype(bf16), new_rows(kvw_new_ref, 2 + g)],
                             axis=0)
        wmask = (lax.broadcasted_iota(jnp.int32, (t_new, nw), 1) >= lax.broadcasted_iota(jnp.int32, (t_new, nw), 0))
        o_w = attend(q, kw, vw, bias_tail, jnp.concatenate([wmask, new_mask], axis=1))
        gate = gate_all[:, g * DH:(g + 1) * DH]
        for h in range(A_HPG):
            def gcol(br):
                return gate[:, br * A_HPG + h:br * A_HPG + h + 1]
            o = gcol(0) * o_c[h] + gcol(1) * o_s[h] + gcol(2) * o_w[h]
            outs.append(o * za[:, (g * A_HPG + h) * DH:(g * A_HPG + h + 1) * DH])
    o_ref[...] = jnp.concatenate(outs, axis=1)

    nrow = win_ref.shape[1]
    win_out_ref[0, 0:nrow - 4 * t_new, :] = win_ref[0, 4 * t_new:nrow, :]
    win_out_ref[0, nrow - 4 * t_new:nrow, :] = kvw_new_ref[...]


def _nsa_sample(qa, kvs_new, kvw_new, cache_cmp, cache_slc, cache_win, page_table, wk, wv, pek, pev, gk,
                gate, za, rel_bias, t_new):
    n, n_pages = page_table.shape
    page = cache_cmp.shape[1] // 4
    past = n_pages * page
    wb = cache_win.shape[1] // 4
    assert wb == WINDOW and past % QB == 0 and past >= WINDOW
    nchunk = past // CMP_STRIDE
    n_cmp = (past + t_new - CMP_BLOCK) // CMP_STRIDE + 1
    assert n_cmp <= nchunk - 1 or n_cmp == nchunk - 1
    n_slc = -(-(past + t_new) // SLC_BLOCK)
    nb_pad = -(-n_slc // 8) * 8
    bias_c = jnp.swapaxes(_cmp_bias_t(rel_bias, past, t_new, nchunk), 1, 2)
    bt = _near_tiles_t(rel_bias)
    b_last = jnp.swapaxes(bt[:, 1, :, :t_new], 1, 2)
    b_new = jnp.swapaxes(bt[:, 0, :, :t_new], 1, 2)
    mt = jnp.asarray(_cmp_to_slc_t(nchunk, nb_pad, n_cmp, n_slc))
    e = jnp.asarray((np.arange(past)[None, :] // SLC_BLOCK == np.arange(nb_pad)[:, None]).astype(np.float32), dtype=bf16)

    def page_spec(p):
        return pl.BlockSpec((1, page * 4, DH), lambda i, pt: (pt[i, p], 0, 0))

    rowblk = lambda w: pl.BlockSpec((t_new, w), lambda i, pt: (i, 0))
    full = lambda a: pl.BlockSpec(a.shape, lambda i, pt: (0,) * a.ndim)
    in_specs = ([page_spec(p) for p in range(n_pages)] * 2 + [
        rowblk(A_WIDTH),
        pl.BlockSpec((t_new * 4, DH), lambda i, pt: (i, 0)),
        pl.BlockSpec((t_new * 4, DH), lambda i, pt: (i, 0)),
        pl.BlockSpec((1, wb * 4, DH), lambda i, pt: (i, 0, 0)),
        full(wk), full(wv), full(pek), full(pev), full(gk), full(bias_c), full(b_last), full(b_new),
        rowblk(A_KV_GROUPS * DH), rowblk(A_WIDTH), full(mt), full(e)])
    grid_spec = pltpu.PrefetchScalarGridSpec(
        num_scalar_prefetch=1, grid=(n,), in_specs=in_specs,
        out_specs=[rowblk(A_WIDTH), pl.BlockSpec((1, wb * 4, DH), lambda i, pt: (i, 0, 0))],
        scratch_shapes=[pltpu.VMEM((4, nchunk * CMP_PITCH, DH), f32)])
    return pl.pallas_call(
        functools.partial(_nsa_sample_kernel, n_pages=n_pages, page=page, t_new=t_new, n_blocks=n_slc),
        grid_spec=grid_spec,
        out_shape=[jax.ShapeDtypeStruct((n * t_new, A_WIDTH), f32), jax.ShapeDtypeStruct(cache_win.shape, f32)],
        compiler_params=pltpu.CompilerParams(dimension_semantics=("arbitrary",), vmem_limit_bytes=VMEM_LIMIT),
        name="nsa_sample",
    )(page_table, *([cache_cmp] * n_pages), *([cache_slc] * n_pages), qa, kvs_new, kvw_new, cache_win,
      wk, wv, pek, pev, gk, bias_c, b_last, b_new, gate, za, mt, e)


def _outproj_kernel(x_ref, oa_ref, ob_ref, w_ref, y_ref):
    acc = jnp.dot(oa_ref[...].astype(bf16), w_ref[0:A_WIDTH, :], preferred_element_type=f32)
    acc = acc + jnp.dot(ob_ref[...].astype(bf16), w_ref[A_WIDTH:, :], preferred_element_type=f32)
    y_ref[...] = x_ref[...] + acc


def _outproj(x2d, oa, ob, w_out_bf, tm):
    m = x2d.shape[0]
    assert m % tm == 0
    row = lambda i: (i, 0)
    return pl.pallas_call(
        _outproj_kernel, grid=(m // tm,),
        in_specs=[pl.BlockSpec((tm, D_MODEL), row), pl.BlockSpec((tm, A_WIDTH), row), pl.BlockSpec((tm, B_WIDTH), row),
                  pl.BlockSpec(w_out_bf.shape, lambda i: (0, 0))],
        out_specs=pl.BlockSpec((tm, D_MODEL), row),
        out_shape=jax.ShapeDtypeStruct((m, D_MODEL), f32),
        compiler_params=pltpu.CompilerParams(dimension_semantics=("arbitrary",), vmem_limit_bytes=VMEM_LIMIT),
        name="outproj",
    )(x2d, oa, ob, w_out_bf)


def kernel(x_prompt, x_sample, cache_kv_cmp, cache_kv_slc, cache_kv_win, state_hgrn, page_table, g_norm, w_in, w_out,
           g_q, g_k_slc, g_k_win, g_k_cmp, w_cmp_k, w_cmp_v, pe_cmp_k, pe_cmp_v, rel_bias, lb_logits, g_o_hgrn):
    depth = w_in.shape[0]
    assert depth == 1, "single-layer trunk"
    nb, t, _ = x_prompt.shape
    ns, ts, _ = x_sample.shape
    row = lambda a: a.astype(f32)[None]
    lower = jnp.cumsum(jax.nn.softmax(lb_logits.astype(f32), axis=0), axis=0)[0]
    w_main, w_gate = _prep_proj_weights(w_in[0])
    w_out_bf = w_out[0].astype(bf16)
    wk, pek = _prep_cmp_weights(w_cmp_k[0], pe_cmp_k[0])
    wv, pev = _prep_cmp_weights(w_cmp_v[0], pe_cmp_v[0])
    proj_args = (row(g_norm[0]), w_main, w_gate, row(g_q[0]), row(g_k_slc[0]), row(g_k_win[0]), row(lower))
    kv6 = lambda a, n_, t_: a.reshape(1, n_, t_, 2, A_KV_GROUPS, DH)

    xp = x_prompt.reshape(nb * t, D_MODEL)
    qa, kvc, kvs, kvw, gate, za, qb, f, ib, zb = _proj(xp, *proj_args, tm=512)
    s0 = jnp.zeros((nb, B_HEADS, DH, DH), f32)
    ob, st_p = _hgrn(qb, f, ib, zb, s0, row(g_o_hgrn[0]), nb, t, tc=128, sc=16, fc=64)
    kc, vct = _compress(kvc, wk, wv, pek, pev, row(g_k_cmp[0]), nb, t)
    oa = _nsa_prompt(qa, kc, vct, kvs, kvw, gate, za, rel_bias, nb, t)
    y_p = _outproj(xp, oa, ob, w_out_bf, tm=256).reshape(nb, t, D_MODEL)
    wlen = min(WINDOW, t)
    win_p = kv6(kvw, nb, t)[:, :, t - wlen:]

    xs = x_sample.reshape(ns * ts, D_MODEL)
    qa, kvc_s, kvs_s, kvw_s, gate, za, qb, f, ib, zb = _proj(xs, *proj_args, tm=512)
    ob, st_s = _hgrn(qb, f, ib, zb, state_hgrn[0].astype(f32), row(g_o_hgrn[0]), ns, ts, tc=ts, sc=ts)
    pool = cache_kv_cmp.shape[1]
    page = cache_kv_cmp.shape[2]
    oa, win_s = _nsa_sample(
        qa, kvs_s, kvw_s, cache_kv_cmp[0].reshape(pool, page * 4, DH), cache_kv_slc[0].reshape(pool, page * 4, DH),
        cache_kv_win[0].reshape(ns, -1, DH), page_table, wk, wv, pek, pev, row(g_k_cmp[0]), gate, za, rel_bias, ts)
    y_s = _outproj(xs, oa, ob, w_out_bf, tm=256).reshape(ns, ts, D_MODEL)

    return (y_p, y_s, kv6(kvc, nb, t), kv6(kvs, nb, t), win_p, st_p[None].astype(x_prompt.dtype),
            kv6(kvc_s, ns, ts), kv6(kvs_s, ns, ts), kv6(win_s, ns, WINDOW), st_s[None].astype(state_hgrn.dtype))
```

```python
import functools
import math

import jax
import jax.numpy as jnp
import numpy as np
from jax import lax
from jax.experimental import pallas as pl
from jax.experimental.pallas import tpu as pltpu

f32 = jnp.float32
bf16 = jnp.bfloat16

D_MODEL = 2048
A_HEADS = 8
A_KV_GROUPS = 2
A_HPG = A_HEADS // A_KV_GROUPS
DH = 128
A_WIDTH = A_HEADS * DH
CMP_BLOCK = 32
CMP_STRIDE = 16
SLC_BLOCK = 64
N_SELECT = 16
WINDOW = 512
B_HEADS = 8
B_WIDTH = B_HEADS * DH
REL_BUCKETS = 32
REL_MAX_DIST = 128
EPS = 1e-6
NEG = -1e30

VMEM_LIMIT = 56 * 1024 * 1024


def _sigmoid(x):
    return 1.0 / (1.0 + jnp.exp(-x))


def _head_norm(a, g):
    return a * lax.rsqrt(jnp.mean(a * a, axis=-1, keepdims=True) + EPS) * g


PROJ_TN = 512
PROJ_NT = 15
PROJ_SEG_STARTS = (0, 2, 3, 4, 5, 7, 9, 11, 13)


def _proj_kernel(x_ref, gn_ref, w_hbm, wg_ref, gq_ref, gks_ref, gkw_ref, lb_ref,
                 qa_ref, kvc_ref, kvs_ref, kvw_ref, gate_ref, za_ref, qb_ref, f_ref, ib_ref, zb_ref,
                 h_scr, w_buf, w_sem):
    tm = x_ref.shape[0]

    def w_copy(j):
        return pltpu.make_async_copy(w_hbm.at[j], w_buf.at[j % 2], w_sem.at[j % 2])

    w_copy(0).start()
    x = x_ref[...]
    h = x * lax.rsqrt(jnp.mean(x * x, axis=-1, keepdims=True) + EPS) * gn_ref[...]
    h_scr[...] = h.astype(bf16)
    gate_ref[...] = _sigmoid(jnp.dot(h_scr[...], wg_ref[...], preferred_element_type=f32))

    for j in range(PROJ_NT):
        if j + 1 < PROJ_NT:
            w_copy(j + 1).start()
        w_copy(j).wait()
        acc = jnp.dot(h_scr[...], w_buf[j % 2], preferred_element_type=f32)
        seg_start = max(s for s in PROJ_SEG_STARTS if s <= j)
        lanes = slice((j - seg_start) * PROJ_TN, (j - seg_start + 1) * PROJ_TN)

        def cols(c):
            return acc[:, c * DH:(c + 1) * DH]

        if seg_start == 0:
            gq = gq_ref[...]
            qa_ref[:, lanes] = jnp.concatenate(
                [_head_norm(cols(c), gq) * (DH ** -0.5) for c in range(4)], axis=1).astype(qa_ref.dtype)
        elif seg_start in (2, 3, 4):
            ref, g_ref = {2: (kvc_ref, None), 3: (kvs_ref, gks_ref), 4: (kvw_ref, gkw_ref)}[seg_start]
            for c in range(4):
                normed = g_ref is not None and c < 2
                ref[pl.ds(c, tm, stride=4), :] = _head_norm(cols(c), g_ref[...]) if normed else cols(c)
        elif seg_start == 5:
            za_ref[:, lanes] = acc * _sigmoid(acc)
        elif seg_start == 7:
            qb_ref[:, lanes] = acc
        elif seg_start == 9:
            lb = lb_ref[:, lanes]
            f_ref[:, lanes] = lb + (1.0 - lb) * _sigmoid(acc)
        elif seg_start == 11:
            ib_ref[:, lanes] = acc
        else:
            zb_ref[:, lanes] = acc * _sigmoid(acc)


def _proj(x2d, gn, w_tiles, w_gate, gq, gks, gkw, lb, tm):
    m = x2d.shape[0]
    assert m % tm == 0 and w_tiles.shape == (PROJ_NT, D_MODEL, PROJ_TN)
    row = lambda i: (i, 0)
    const = lambda i: (0, 0)
    out_shape = [
        jax.ShapeDtypeStruct((m, A_WIDTH), bf16),
        jax.ShapeDtypeStruct((m * 4, DH), f32),
        jax.ShapeDtypeStruct((m * 4, DH), f32),
        jax.ShapeDtypeStruct((m * 4, DH), f32),
        jax.ShapeDtypeStruct((m, A_KV_GROUPS * DH), f32),
        jax.ShapeDtypeStruct((m, A_WIDTH), f32),
        jax.ShapeDtypeStruct((m, B_WIDTH), f32),
        jax.ShapeDtypeStruct((m, B_WIDTH), f32),
        jax.ShapeDtypeStruct((m, B_WIDTH), f32),
        jax.ShapeDtypeStruct((m, B_WIDTH), f32),
    ]
    out_specs = [pl.BlockSpec((tm * 4, DH) if s.shape[0] == m * 4 else (tm, s.shape[1]), row) for s in out_shape]
    in_specs = [
        pl.BlockSpec((tm, D_MODEL), row),
        pl.BlockSpec((1, D_MODEL), const),
        pl.BlockSpec(memory_space=pl.ANY),
        pl.BlockSpec((D_MODEL, A_KV_GROUPS * DH), const),
        pl.BlockSpec((1, DH), const),
        pl.BlockSpec((1, DH), const),
        pl.BlockSpec((1, DH), const),
        pl.BlockSpec((1, B_WIDTH), const),
    ]
    return pl.pallas_call(
        _proj_kernel, grid=(m // tm,), in_specs=in_specs, out_specs=out_specs, out_shape=out_shape,
        scratch_shapes=[pltpu.VMEM((tm, D_MODEL), bf16), pltpu.VMEM((2, D_MODEL, PROJ_TN), bf16),
                        pltpu.SemaphoreType.DMA((2,))],
        compiler_params=pltpu.CompilerParams(dimension_semantics=("arbitrary",), vmem_limit_bytes=VMEM_LIMIT),
        name="proj",
    )(x2d, gn, w_tiles, w_gate, gq, gks, gkw, lb)


def _prep_proj_weights(w_in):
    a0 = A_WIDTH + 6 * A_KV_GROUPS * DH
    a1 = a0 + 3 * A_HEADS
    w_main = jnp.concatenate([w_in[:, :a0], w_in[:, a1:]], axis=1).astype(bf16)
    w_main = w_main.reshape(D_MODEL, PROJ_NT, PROJ_TN).transpose(1, 0, 2)
    wg = w_in[:, a0:a1].reshape(-1, 3, A_KV_GROUPS, A_HPG).transpose(0, 2, 1, 3).reshape(-1, A_KV_GROUPS, 3 * A_HPG)
    w_gate = jnp.pad(wg, ((0, 0), (0, 0), (0, DH - 3 * A_HPG))).reshape(-1, A_KV_GROUPS * DH).astype(bf16)
    return w_main, w_gate


HGRN_SAFE_LOG_DECAY = 80.0


def _hgrn_exact(q, k, v, b, st):
    sc = q.shape[0]
    t_idx = lax.broadcasted_iota(jnp.int32, (sc, DH), 0)
    o = lax.dot_general((q * jnp.exp(b)).astype(bf16), st.astype(bf16), (((1,), (1,)), ((), ())),
                        preferred_element_type=f32)
    for s in range(sc):
        e = jnp.exp(jnp.minimum(b - b[s:s + 1, :], 0.0))
        a = jnp.where(t_idx >= s, q * e * k[s:s + 1, :], 0.0)
        o = o + jnp.sum(a, axis=-1, keepdims=True) * v[s:s + 1, :]
    bl = b[sc - 1:sc, :]
    ut = lax.dot_general(v.astype(bf16), (k * jnp.exp(bl - b)).astype(bf16), (((0,), (0,)), ((), ())),
                         preferred_element_type=f32)
    return o, st * jnp.exp(bl) + ut


def _hgrn_factored(q, k, v, b, st):
    fc = q.shape[0]
    qd = (q * jnp.exp(b)).astype(bf16)
    att = lax.dot_general(qd, (k * jnp.exp(-b)).astype(bf16), (((1,), (1,)), ((), ())), preferred_element_type=f32)
    tri = lax.broadcasted_iota(jnp.int32, (fc, fc), 0) >= lax.broadcasted_iota(jnp.int32, (fc, fc), 1)
    att = jnp.where(tri, att, 0.0)
    o = jnp.dot(att.astype(bf16), v.astype(bf16), preferred_element_type=f32)
    o = o + lax.dot_general(qd, st.astype(bf16), (((1,), (1,)), ((), ())), preferred_element_type=f32)
    bl = b[fc - 1:fc, :]
    ut = lax.dot_general(v.astype(bf16), (k * jnp.exp(bl - b)).astype(bf16), (((0,), (0,)), ((), ())),
                         preferred_element_type=f32)
    return o, st * jnp.exp(bl) + ut


def _hgrn_kernel(q_ref, f_ref, v_ref, zs_ref, s0_ref, go_ref, tri_ref, o_ref, sout_ref, st_scr, *, sc, fc):
    c = pl.program_id(1)
    tc = q_ref.shape[0]

    @pl.when(c == 0)
    def _():
        for h in range(B_HEADS):
            st_scr[h] = s0_ref[0, h].T

    f = f_ref[...]
    b_all = lax.dot_general(tri_ref[...], jnp.log(f), (((1,), (0,)), ((), ())),
                            precision=lax.Precision.HIGHEST, preferred_element_type=f32)
    go = go_ref[...]

    def finish(rows, lanes, o):
        o_ref[rows, lanes] = _head_norm(o, go) * zs_ref[rows, lanes]

    def exact_rows(r0, n_rows, rebase):
        for h in range(B_HEADS):
            lanes = slice(h * DH, (h + 1) * DH)
            outs = []
            for i in range(n_rows // sc):
                rows = slice(r0 + i * sc, r0 + (i + 1) * sc)
                b = b_all[rows, lanes]
                if rebase and i > 0:
                    b = b - b_all[r0 + i * sc - 1:r0 + i * sc, lanes]
                o, st = _hgrn_exact(q_ref[rows, lanes], 1.0 - f[rows, lanes], v_ref[rows, lanes], b, st_scr[h])
                st_scr[h] = st
                outs.append(o)
            finish(slice(r0, r0 + n_rows), lanes, jnp.concatenate(outs, axis=0) if len(outs) > 1 else outs[0])

    if fc is None:
        exact_rows(0, tc, False)
    else:
        for i in range(tc // fc):
            r0 = i * fc
            rows = slice(r0, r0 + fc)
            total = jnp.max(-b_all[r0 + fc - 1:r0 + fc, :])

            @pl.when(total < HGRN_SAFE_LOG_DECAY)
            def _():
                for h in range(B_HEADS):
                    lanes = slice(h * DH, (h + 1) * DH)
                    o, st = _hgrn_factored(q_ref[rows, lanes], 1.0 - f[rows, lanes], v_ref[rows, lanes],
                                           b_all[rows, lanes], st_scr[h])
                    st_scr[h] = st
                    finish(rows, lanes, o)

            @pl.when(jnp.logical_not(total < HGRN_SAFE_LOG_DECAY))
            def _():
                exact_rows(r0, fc, True)

    @pl.when(c == pl.num_programs(1) - 1)
    def _():
        for h in range(B_HEADS):
            sout_ref[0, h] = st_scr[h].T


def _block_tri(tc, blk):
    r = np.arange(tc)
    return jnp.asarray(((r[:, None] // blk == r[None, :] // blk) & (r[None, :] <= r[:, None])).astype(np.float32))


def _hgrn(qb, f, ib, zs, s0, go, n, t, tc, sc, fc=None):
    assert t % tc == 0 and tc % sc == 0 and (fc is None or (tc % fc == 0 and fc % sc == 0))
    nc = t // tc
    rows = lambda i, c: (i * nc + c, 0)
    st = lambda i, c: (i, 0, 0, 0)
    const = lambda i, c: (0, 0)
    return pl.pallas_call(
        functools.partial(_hgrn_kernel, sc=sc, fc=fc),
        grid=(n, nc),
        in_specs=[pl.BlockSpec((tc, B_WIDTH), rows)] * 4 + [
            pl.BlockSpec((1, B_HEADS, DH, DH), st),
            pl.BlockSpec((1, DH), const),
            pl.BlockSpec((tc, tc), const),
        ],
        out_specs=[pl.BlockSpec((tc, B_WIDTH), rows), pl.BlockSpec((1, B_HEADS, DH, DH), st)],
        out_shape=[jax.ShapeDtypeStruct((n * t, B_WIDTH), f32), jax.ShapeDtypeStruct((n, B_HEADS, DH, DH), f32)],
        scratch_shapes=[pltpu.VMEM((B_HEADS, DH, DH), f32)],
        compiler_params=pltpu.CompilerParams(
            dimension_semantics=("arbitrary", "arbitrary"), vmem_limit_bytes=VMEM_LIMIT),
        name="hgrn",
    )(qb, f, ib, zs, s0, go, _block_tri(tc, sc if fc is None else fc))


def _rel_bucket_np(dist):
    n = np.maximum(dist, 0)
    exact = REL_BUCKETS // 2
    scale = np.float32((REL_BUCKETS - exact) / math.log(REL_MAX_DIST / exact))
    large = exact + (np.log(np.maximum(n, exact).astype(np.float32) / np.float32(exact)) * scale).astype(np.int32)
    return np.where(n < exact, n, np.minimum(large, REL_BUCKETS - 1)).astype(np.int32)


def _bucket_starts():
    buckets = _rel_bucket_np(np.arange(2 * REL_MAX_DIST))
    assert np.all(np.diff(buckets) >= 0) and buckets[-1] == REL_BUCKETS - 1
    return [int(np.argmax(buckets >= b)) for b in range(REL_BUCKETS)]


def _bias_table_kernel(rb_ref, o_ref, *, off, a0, a1, sub_far):
    h = pl.program_id(0)
    shape = o_ref.shape[1:]
    d = off + a0 * lax.broadcasted_iota(jnp.int32, shape, 0) + a1 * lax.broadcasted_iota(jnp.int32, shape, 1)
    starts = _bucket_starts()
    far = rb_ref[REL_BUCKETS - 1, h]
    v = jnp.full(shape, far, f32)
    for b in range(REL_BUCKETS - 2, -1, -1):
        v = jnp.where(d < starts[b + 1], rb_ref[b, h], v)
    o_ref[0] = v - far if sub_far else v


def _bias_table(rel_bias, shape, off, a0, a1, sub_far):
    return pl.pallas_call(
        functools.partial(_bias_table_kernel, off=off, a0=a0, a1=a1, sub_far=sub_far),
        grid=(A_HEADS,),
        in_specs=[pl.BlockSpec(memory_space=pltpu.SMEM)],
        out_specs=pl.BlockSpec((1,) + tuple(shape), lambda h: (h, 0, 0)),
        out_shape=jax.ShapeDtypeStruct((A_HEADS,) + tuple(shape), f32),
        compiler_params=pltpu.CompilerParams(dimension_semantics=("arbitrary",)),
        name="bias_table",
    )(rel_bias.astype(f32))


def _cmp_bias_t(rel_bias, t0, nt, n_chunk):
    return _bias_table(rel_bias, (n_chunk, nt), t0 - (CMP_BLOCK - 1), -CMP_STRIDE, 1, False)


def _near_tiles_t(rel_bias):
    return _bias_table(rel_bias, (QB, 2 * QB), 0, -1, 1, True)


def _compress_combine(lhs, wcat, pe2):
    out = jnp.dot(lhs, wcat, preferred_element_type=f32)
    pc = jnp.dot(pe2, wcat, preferred_element_type=f32)
    const = pc[0:1, :DH] + pc[1:2, DH:]
    r = lhs.shape[0]
    return out[:, :DH] + pltpu.roll(out[:, DH:], r - 1, 0) + const


def _compress_kernel(kv_ref, wk_ref, wv_ref, pek_ref, pev_ref, gk_ref, kc_ref, vct_ref):
    nchunk = kv_ref.shape[0] // (4 * CMP_STRIDE)

    def chunk_rows(cg):
        return jnp.concatenate(
            [kv_ref[pl.ds(4 * l + cg, nchunk, stride=4 * CMP_STRIDE), :].astype(bf16) for l in range(CMP_STRIDE)],
            axis=1)

    kc = _compress_combine(jnp.concatenate([chunk_rows(0), chunk_rows(1)], axis=0), wk_ref[...], pek_ref[...])
    vc = _compress_combine(jnp.concatenate([chunk_rows(2), chunk_rows(3)], axis=0), wv_ref[...], pev_ref[...])
    kc = _head_norm(kc, gk_ref[...])
    for g in range(A_KV_GROUPS):
        kc_ref[0, g] = kc[g * nchunk:(g + 1) * nchunk].astype(kc_ref.dtype)
        vct_ref[0, g] = vc[g * nchunk:(g + 1) * nchunk].T.astype(vct_ref.dtype)


def _prep_cmp_weights(w, pe):
    half = CMP_STRIDE * DH
    wcat = jnp.concatenate([w[:half], w[half:]], axis=1).astype(bf16)
    return wcat, pe.reshape(2, half).astype(bf16)


def _compress(kvc, wk, wv, pek, pev, gk, n, t):
    nchunk = t // CMP_STRIDE
    assert nchunk == DH, "the transposed v_c block is square"
    const = lambda i: (0, 0)
    return pl.pallas_call(
        _compress_kernel, grid=(n,),
        in_specs=[pl.BlockSpec((t * 4, DH), lambda i: (i, 0)),
                  pl.BlockSpec(wk.shape, const), pl.BlockSpec(wv.shape, const),
                  pl.BlockSpec(pek.shape, const), pl.BlockSpec(pev.shape, const), pl.BlockSpec((1, DH), const)],
        out_specs=[pl.BlockSpec((1, A_KV_GROUPS, nchunk, DH), lambda i: (i, 0, 0, 0))] * 2,
        out_shape=[jax.ShapeDtypeStruct((n, A_KV_GROUPS, nchunk, DH), bf16)] * 2,
        compiler_params=pltpu.CompilerParams(dimension_semantics=("arbitrary",), vmem_limit_bytes=VMEM_LIMIT),
        name="compress",
    )(kvc, wk, wv, pek, pev, gk)


def _softmax_tile(carry, s, mask, v_bf):
    m, l, acc = carry
    s = jnp.where(mask, s, NEG)
    m_new = jnp.maximum(m, jnp.max(s, axis=-1, keepdims=True))
    alpha = jnp.exp(m - m_new)
    p = jnp.where(mask, jnp.exp(s - m_new), 0.0)
    l = alpha * l + jnp.sum(p, axis=-1, keepdims=True)
    h, r, k = p.shape
    pv = jnp.dot(p.reshape(h * r, k).astype(bf16), v_bf, preferred_element_type=f32).reshape(h, r, DH)
    return m_new, l, alpha * acc + pv


def _softmax_finish(carry):
    m, l, acc = carry
    return acc / jnp.where(l > 0, l, 1.0)


def _select_blocks(score_t, tpos, n_blocks):
    nb = score_t.shape[0]
    j = lax.broadcasted_iota(jnp.int32, score_t.shape, 0)
    valid = (j * SLC_BLOCK <= tpos) & (j < n_blocks)
    cur = tpos >> 6
    forced = (j == 0) | (j == cur) | (j == cur - 1)
    val = jnp.where(valid, jnp.where(forced, 1e30, score_t), -1.0)
    rank = jnp.zeros(score_t.shape, jnp.int32)
    for i in range(n_blocks):
        vi = val[i:i + 1, :]
        beats = (vi > val) | ((vi == val) & (i < j))
        rank = rank + beats.astype(jnp.int32)
    return jnp.where((rank < N_SELECT) & valid, 1.0, 0.0)


QB = 128


NEG_M = -1e30
NEG_S = -2e30


def _nsa_prompt_kernel(q_ref, kc_ref, vct_ref, bc_ref, kvs_ref, kvw_ref, gate_ref, za_ref, bt_ref, mt_ref, et_ref,
                       o_ref, ks_scr, vts_scr, kw_scr, vtw_scr, km_scr):
    qb = pl.program_id(1)
    q0 = qb * QB
    nkt = vts_scr.shape[1]
    nq4 = A_HPG * QB
    groups = range(A_KV_GROUPS)

    @pl.when(qb == 0)
    def _():
        for g in groups:
            for kt in range(nkt):
                rows = slice(kt * QB, (kt + 1) * QB)
                base = kt * 4 * QB
                ks_scr[g, rows, :] = kvs_ref[pl.ds(base + g, QB, stride=4), :].astype(bf16)
                vts_scr[g, kt] = kvs_ref[pl.ds(base + 2 + g, QB, stride=4), :].T.astype(bf16)
                kw_scr[g, rows, :] = kvw_ref[pl.ds(base + g, QB, stride=4), :].astype(bf16)
                vtw_scr[g, kt] = kvw_ref[pl.ds(base + 2 + g, QB, stride=4), :].T.astype(bf16)

    qall = q_ref[...]
    qts = [jnp.concatenate([qall[:, (g * A_HPG + h) * DH:(g * A_HPG + h + 1) * DH].astype(f32).T
                            for h in range(A_HPG)], axis=1).astype(bf16) for g in groups]
    kk = lax.broadcasted_iota(jnp.int32, (QB, nq4), 0)
    tt = lax.broadcasted_iota(jnp.int32, (QB, nq4), 1) & (QB - 1)

    def heads(ref, g, *idx):
        return jnp.concatenate([ref[(g * A_HPG + h,) + idx] for h in range(A_HPG)], axis=1)

    o_c = []
    for g in groups:
        sc = jnp.dot(kc_ref[0, g], qts[g], preferred_element_type=f32) + heads(bc_ref, g)
        mask_c = q0 + tt >= CMP_STRIDE * kk + (CMP_BLOCK - 1)
        sc = jnp.where(mask_c, sc, NEG)
        mc = jnp.max(sc, axis=0, keepdims=True)
        ec = jnp.where(mask_c, jnp.exp(sc - mc), 0.0)
        lc = jnp.sum(ec, axis=0, keepdims=True)
        pc = ec / jnp.where(lc > 0, lc, 1.0)
        o_c.append(jnp.dot(vct_ref[0, g], pc.astype(bf16), preferred_element_type=f32))
        ps = pc[:, 0:QB] + pc[:, QB:2 * QB] + pc[:, 2 * QB:3 * QB] + pc[:, 3 * QB:4 * QB]
        score_t = jnp.dot(mt_ref[...], ps, precision=lax.Precision.HIGHEST, preferred_element_type=f32)
        nb = score_t.shape[0]
        sel_t = _select_blocks(score_t, q0 + lax.broadcasted_iota(jnp.int32, (nb, QB), 1), nb)
        km_scr[g] = jnp.dot(et_ref[...], sel_t.astype(bf16), preferred_element_type=f32)

    def tile(carry, qt, k_tile, vt_tile, mask, bias):
        m, l, acc = carry
        s = jnp.dot(k_tile, qt, preferred_element_type=f32)
        if bias is not None:
            s = s + bias
        s = jnp.where(mask, s, NEG_S)
        m_new = jnp.maximum(m, jnp.max(s, axis=0, keepdims=True))
        alpha = jnp.exp(m - m_new)
        p = jnp.exp(s - m_new)
        l = alpha * l + jnp.sum(p, axis=0, keepdims=True)
        return m_new, l, alpha * acc + jnp.dot(vt_tile, p.astype(bf16), preferred_element_type=f32)

    def init():
        return jnp.full((1, nq4), NEG_M, f32), jnp.zeros((1, nq4), f32), jnp.zeros((DH, nq4), f32)

    def finish(carry):
        m, l, acc = carry
        return acc / jnp.where(l > 0, l, 1.0)

    def key_rows(kt):
        return pl.ds(pl.multiple_of(kt * QB, QB), QB)

    def lanes4(x):
        return jnp.concatenate([x] * A_HPG, axis=1)

    def near_span(g, k_scr, vt_scr, rs, use_sel):
        ks, vts, masks, biases = [], [], [], []
        for r in rs:
            kt = jnp.maximum(qb - r, 0)
            ks.append(k_scr[g, key_rows(kt), :])
            vts.append(vt_scr[g, kt])
            mask = kk * 0 + (qb - r) >= 0
            if use_sel:
                mask = mask & (lanes4(km_scr[g, key_rows(kt), :]) > 0.5)
            if r == 0:
                mask = mask & (kk <= tt)
            if r == WINDOW // QB:
                mask = mask & (kk >= tt)
            masks.append(mask)
            biases.append(heads(bt_ref, g, slice(None), slice(r * QB, (r + 1) * QB)) if r < 2
                          else jnp.zeros((QB, nq4), f32))
        cat = lambda xs, axis: jnp.concatenate(xs, axis=axis)
        return cat(ks, 0), cat(vts, 1), cat(masks, 0), cat(biases, 0)

    n_far = jnp.maximum(qb - 1, 0)
    kk2 = lax.broadcasted_iota(jnp.int32, (2 * QB, nq4), 0)

    def far(i, carries):
        rows = pl.ds(pl.multiple_of(i * 2 * QB, 2 * QB), 2 * QB)
        in_range = (kk2 + i * 2 * QB) < n_far * QB
        out = []
        for g in groups:
            mask = (lanes4(km_scr[g, rows, :]) > 0.5) & in_range
            vt = jnp.concatenate([vts_scr[g, 2 * i], vts_scr[g, jnp.minimum(2 * i + 1, nkt - 1)]], axis=1)
            out.append(tile(carries[g], qts[g], ks_scr[g, rows, :], vt, mask, None))
        return tuple(out)

    carries = lax.fori_loop(0, (n_far + 1) // 2, far, tuple(init() for g in groups))
    o_s = [finish(tile(carries[g], qts[g], *near_span(g, ks_scr, vts_scr, (1, 0), True))) for g in groups]

    o_w = [finish(tile(init(), qts[g], *near_span(g, kw_scr, vtw_scr, (4, 3, 2, 1, 0), False))) for g in groups]

    za = za_ref[...]
    outs = []
    for g in groups:
        gate_t = gate_ref[:, g * DH:(g + 1) * DH].T
        for h in range(A_HPG):
            cols = slice(h * QB, (h + 1) * QB)

            def grow(br):
                return gate_t[br * A_HPG + h:br * A_HPG + h + 1, :]
            o = grow(0) * o_c[g][:, cols] + grow(1) * o_s[g][:, cols] + grow(2) * o_w[g][:, cols]
            outs.append(o.T * za[:, (g * A_HPG + h) * DH:(g * A_HPG + h + 1) * DH])
    o_ref[...] = jnp.concatenate(outs, axis=1)


def _cmp_to_slc_t(n_cmp_pad, n_slc_pad, n_cmp, n_slc):
    c0 = np.arange(n_cmp_pad)[None, :] * CMP_STRIDE
    s0 = np.arange(n_slc_pad)[:, None] * SLC_BLOCK
    ov = np.minimum(c0 + CMP_BLOCK, s0 + SLC_BLOCK) - np.maximum(c0, s0)
    m = np.maximum(ov, 0).astype(np.float32) / CMP_STRIDE
    m[:, n_cmp:] = 0
    m[n_slc:, :] = 0
    return m


def _nsa_prompt(qa, kc, vct, kvs, kvw, gate, za, rel_bias, n, t):
    nq = t // QB
    n_cmp = (t - CMP_BLOCK) // CMP_STRIDE + 1
    n_slc = -(-t // SLC_BLOCK)
    assert kc.shape[2] == QB and n_slc % 8 == 0
    bias_c = _cmp_bias_t(rel_bias, 0, t, QB)
    bt = _near_tiles_t(rel_bias)
    mt = jnp.asarray(_cmp_to_slc_t(QB, n_slc, n_cmp, n_slc))
    et = jnp.asarray((np.arange(t)[:, None] // SLC_BLOCK == np.arange(n_slc)[None, :]).astype(np.float32), dtype=bf16)
    qrow = lambda i, b: (i * nq + b, 0)
    seq = lambda i, b: (i, 0, 0, 0)
    ng = A_KV_GROUPS
    return pl.pallas_call(
        _nsa_prompt_kernel, grid=(n, nq),
        in_specs=[
            pl.BlockSpec((QB, A_WIDTH), qrow),
            pl.BlockSpec((1, ng, QB, DH), seq),
            pl.BlockSpec((1, ng, DH, QB), seq),
            pl.BlockSpec((A_HEADS, QB, QB), lambda i, b: (0, 0, b)),
            pl.BlockSpec((t * 4, DH), lambda i, b: (i, 0)),
            pl.BlockSpec((t * 4, DH), lambda i, b: (i, 0)),
            pl.BlockSpec((QB, ng * DH), qrow),
            pl.BlockSpec((QB, A_WIDTH), qrow),
            pl.BlockSpec((A_HEADS, QB, 2 * QB), lambda i, b: (0, 0, 0)),
            pl.BlockSpec(mt.shape, lambda i, b: (0, 0)),
            pl.BlockSpec(et.shape, lambda i, b: (0, 0)),
        ],
        out_specs=pl.BlockSpec((QB, A_WIDTH), qrow),
        out_shape=jax.ShapeDtypeStruct((n * t, A_WIDTH), f32),
        scratch_shapes=[pltpu.VMEM((ng, t, DH), bf16), pltpu.VMEM((ng, nq, DH, QB), bf16),
                        pltpu.VMEM((ng, t, DH), bf16), pltpu.VMEM((ng, nq, DH, QB), bf16),
                        pltpu.VMEM((ng, t, QB), f32)],
        compiler_params=pltpu.CompilerParams(
            dimension_semantics=("arbitrary", "arbitrary"), vmem_limit_bytes=VMEM_LIMIT),
        name="nsa_prompt",
    )(qa, kc, vct, bias_c, kvs, kvw, gate, za, bt, mt, et)


CMP_PITCH = 24


def _nsa_sample_kernel(pt_ref, *refs, n_pages, page, t_new, n_blocks):
    del pt_ref
    cmp_pages = refs[:n_pages]
    slc_pages = refs[n_pages:2 * n_pages]
    (q_ref, kvs_new_ref, kvw_new_ref, win_ref, wk_ref, wv_ref, pek_ref, pev_ref, gk_ref, bc_ref, bl_ref, bn_ref,
     gate_ref, za_ref, mt_ref, e_ref, o_ref, win_out_ref, xc_scr) = refs[2 * n_pages:]
    past = n_pages * page
    nchunk = past // CMP_STRIDE
    pad_new = QB
    rows_q = A_HPG * t_new

    cpp = page // CMP_STRIDE
    for p in range(n_pages):
        for cg in range(4):
            x = cmp_pages[p][0, pl.ds(cg, page, stride=4), :]
            for c in range(cpp):
                r0 = (p * cpp + c) * CMP_PITCH
                xc_scr[cg, r0:r0 + CMP_STRIDE, :] = x[c * CMP_STRIDE:(c + 1) * CMP_STRIDE]

    def chunk_rows(cg):
        return jnp.concatenate(
            [xc_scr[cg, pl.ds(l, nchunk, stride=CMP_PITCH), :].astype(bf16) for l in range(CMP_STRIDE)], axis=1)

    kc = _compress_combine(jnp.concatenate([chunk_rows(0), chunk_rows(1)], axis=0), wk_ref[...], pek_ref[...])
    vc = _compress_combine(jnp.concatenate([chunk_rows(2), chunk_rows(3)], axis=0), wv_ref[...], pev_ref[...])
    kc = _head_norm(kc, gk_ref[...]).astype(bf16)
    vc = vc.astype(bf16)

    tt = lax.broadcasted_iota(jnp.int32, (t_new, pad_new), 0)
    uu = lax.broadcasted_iota(jnp.int32, (t_new, pad_new), 1)
    new_mask = (uu <= tt)
    qall = q_ref[...]
    gate_all = gate_ref[...]
    za = za_ref[...]
    zeros_pad = jnp.zeros((pad_new - t_new, DH), bf16)

    def new_rows(ref, c):
        return jnp.concatenate([ref[pl.ds(c, t_new, stride=4), :].astype(bf16), zeros_pad], axis=0)

    def attend(q, k_all, v_all, bias_tail, mask):
        nk = k_all.shape[0]
        s = lax.dot_general(q, k_all, (((1,), (1,)), ((), ())), preferred_element_type=f32).reshape(A_HPG, t_new, nk)
        s = jnp.concatenate([s[:, :, :nk - 2 * QB], s[:, :, nk - 2 * QB:] + bias_tail], axis=-1)
        carry = (jnp.full((A_HPG, t_new, 1), NEG, f32), jnp.zeros((A_HPG, t_new, 1), f32),
                 jnp.zeros((A_HPG, t_new, DH), f32))
        return _softmax_finish(_softmax_tile(carry, s, mask[None], v_all))

    outs = []
    for g in range(A_KV_GROUPS):
        q = jnp.concatenate([qall[:, (g * A_HPG + h) * DH:(g * A_HPG + h + 1) * DH] for h in range(A_HPG)], axis=0)
        cmask = (past + lax.broadcasted_iota(jnp.int32, (t_new, nchunk), 0)
                 >= CMP_STRIDE * lax.broadcasted_iota(jnp.int32, (t_new, nchunk), 1) + (CMP_BLOCK - 1))[None]
        sc = lax.dot_general(q, kc[g * nchunk:(g + 1) * nchunk], (((1,), (1,)), ((), ())),
                             preferred_element_type=f32).reshape(A_HPG, t_new, nchunk) + bc_ref[g * A_HPG:(g + 1) * A_HPG]
        sc = jnp.where(cmask, sc, NEG)
        mc = jnp.max(sc, axis=-1, keepdims=True)
        ec = jnp.where(cmask, jnp.exp(sc - mc), 0.0)
        lc = jnp.sum(ec, axis=-1, keepdims=True)
        pc = ec / jnp.where(lc > 0, lc, 1.0)
        o_c = jnp.dot(pc.reshape(rows_q, nchunk).astype(bf16), vc[g * nchunk:(g + 1) * nchunk],
                      preferred_element_type=f32).reshape(A_HPG, t_new, DH)
        ps = pc[0] + pc[1] + pc[2] + pc[3]
        score_t = lax.dot_general(mt_ref[...], ps, (((1,), (1,)), ((), ())),
                                  precision=lax.Precision.HIGHEST, preferred_element_type=f32)
        tpos = past + lax.broadcasted_iota(jnp.int32, score_t.shape, 1)
        sel_t = _select_blocks(score_t, tpos, n_blocks)
        key_mask = lax.dot_general(sel_t.astype(bf16), e_ref[...], (((0,), (0,)), ((), ())),
                                   preferred_element_type=f32) > 0.5
        k_all = jnp.concatenate([slc_pages[p][0, pl.ds(g, page, stride=4), :].astype(bf16) for p in range(n_pages)]
                                + [new_rows(kvs_new_ref, g)], axis=0)
        v_all = jnp.concatenate([slc_pages[p][0, pl.ds(2 + g, page, stride=4), :].astype(bf16) for p in range(n_pages)]
                                + [new_rows(kvs_new_ref, 2 + g)], axis=0)
        bias_tail = jnp.concatenate([bl_ref[g * A_HPG:(g + 1) * A_HPG], bn_ref[g * A_HPG:(g + 1) * A_HPG]], axis=-1)
        o_s = attend(q, k_all, v_all, bias_tail, jnp.concatenate([key_mask, new_mask], axis=1))
        nw = win_ref.shape[1] // 4
        kw = jnp.concatenate([win_ref[0, pl.ds(g, nw, stride=4), :].astype(bf16), new_rows(kvw_new_ref, g)], axis=0)
        vw = jnp.concatenate([win_ref[0, pl.ds(2 + g, nw, stride=4), :].astype(bf16), new_rows(kvw_new_ref, 2 + g)],
                             axis=0)
        wmask = (lax.broadcasted_iota(jnp.int32, (t_new, nw), 1) >= lax.broadcasted_iota(jnp.int32, (t_new, nw), 0))
        o_w = attend(q, kw, vw, bias_tail, jnp.concatenate([wmask, new_mask], axis=1))
        gate = gate_all[:, g * DH:(g + 1) * DH]
        for h in range(A_HPG):
            def gcol(br):
                return gate[:, br * A_HPG + h:br * A_HPG + h + 1]
            o = gcol(0) * o_c[h] + gcol(1) * o_s[h] + gcol(2) * o_w[h]
            outs.append(o * za[:, (g * A_HPG + h) * DH:(g * A_HPG + h + 1) * DH])
    o_ref[...] = jnp.concatenate(outs, axis=1)

    nrow = win_ref.shape[1]
    win_out_ref[0, 0:nrow - 4 * t_new, :] = win_ref[0, 4 * t_new:nrow, :]
    win_out_ref[0, nrow - 4 * t_new:nrow, :] = kvw_new_ref[...]


def _nsa_sample(qa, kvs_new, kvw_new, cache_cmp, cache_slc, cache_win, page_table, wk, wv, pek, pev, gk,
                gate, za, rel_bias, t_new):
    n, n_pages = page_table.shape
    page = cache_cmp.shape[1] // 4
    past = n_pages * page
    wb = cache_win.shape[1] // 4
    assert wb == WINDOW and past % QB == 0 and past >= WINDOW
    nchunk = past // CMP_STRIDE
    n_cmp = (past + t_new - CMP_BLOCK) // CMP_STRIDE + 1
    assert n_cmp <= nchunk - 1 or n_cmp == nchunk - 1
    n_slc = -(-(past + t_new) // SLC_BLOCK)
    nb_pad = -(-n_slc // 8) * 8
    bias_c = _bias_table(rel_bias, (t_new, nchunk), past - (CMP_BLOCK - 1), 1, -CMP_STRIDE, False)
    b_last = _bias_table(rel_bias, (t_new, QB), QB, 1, -1, True)
    b_new = _bias_table(rel_bias, (t_new, QB), 0, 1, -1, True)
    mt = jnp.asarray(_cmp_to_slc_t(nchunk, nb_pad, n_cmp, n_slc))
    e = jnp.asarray((np.arange(past)[None, :] // SLC_BLOCK == np.arange(nb_pad)[:, None]).astype(np.float32), dtype=bf16)

    def page_spec(p):
        return pl.BlockSpec((1, page * 4, DH), lambda i, pt: (pt[i, p], 0, 0))

    rowblk = lambda w: pl.BlockSpec((t_new, w), lambda i, pt: (i, 0))
    full = lambda a: pl.BlockSpec(a.shape, lambda i, pt: (0,) * a.ndim)
    in_specs = ([page_spec(p) for p in range(n_pages)] * 2 + [
        rowblk(A_WIDTH),
        pl.BlockSpec((t_new * 4, DH), lambda i, pt: (i, 0)),
        pl.BlockSpec((t_new * 4, DH), lambda i, pt: (i, 0)),
        pl.BlockSpec((1, wb * 4, DH), lambda i, pt: (i, 0, 0)),
        full(wk), full(wv), full(pek), full(pev), full(gk), full(bias_c), full(b_last), full(b_new),
        rowblk(A_KV_GROUPS * DH), rowblk(A_WIDTH), full(mt), full(e)])
    grid_spec = pltpu.PrefetchScalarGridSpec(
        num_scalar_prefetch=1, grid=(n,), in_specs=in_specs,
        out_specs=[rowblk(A_WIDTH), pl.BlockSpec((1, wb * 4, DH), lambda i, pt: (i, 0, 0))],
        scratch_shapes=[pltpu.VMEM((4, nchunk * CMP_PITCH, DH), f32)])
    return pl.pallas_call(
        functools.partial(_nsa_sample_kernel, n_pages=n_pages, page=page, t_new=t_new, n_blocks=n_slc),
        grid_spec=grid_spec,
        out_shape=[jax.ShapeDtypeStruct((n * t_new, A_WIDTH), f32), jax.ShapeDtypeStruct(cache_win.shape, f32)],
        compiler_params=pltpu.CompilerParams(dimension_semantics=("arbitrary",), vmem_limit_bytes=VMEM_LIMIT),
        name="nsa_sample",
    )(page_table, *([cache_cmp] * n_pages), *([cache_slc] * n_pages), qa, kvs_new, kvw_new, cache_win,
      wk, wv, pek, pev, gk, bias_c, b_last, b_new, gate, za, mt, e)


def _outproj_kernel(x_ref, oa_ref, ob_ref, w_ref, y_ref):
    acc = jnp.dot(oa_ref[...].astype(bf16), w_ref[0:A_WIDTH, :], preferred_element_type=f32)
    acc = acc + jnp.dot(ob_ref[...].astype(bf16), w_ref[A_WIDTH:, :], preferred_element_type=f32)
    y_ref[...] = x_ref[...] + acc


def _outproj(x2d, oa, ob, w_out_bf, tm):
    m = x2d.shape[0]
    assert m % tm == 0
    row = lambda i: (i, 0)
    return pl.pallas_call(
        _outproj_kernel, grid=(m // tm,),
        in_specs=[pl.BlockSpec((tm, D_MODEL), row), pl.BlockSpec((tm, A_WIDTH), row), pl.BlockSpec((tm, B_WIDTH), row),
                  pl.BlockSpec(w_out_bf.shape, lambda i: (0, 0))],
        out_specs=pl.BlockSpec((tm, D_MODEL), row),
        out_shape=jax.ShapeDtypeStruct((m, D_MODEL), f32),
        compiler_params=pltpu.CompilerParams(dimension_semantics=("arbitrary",), vmem_limit_bytes=VMEM_LIMIT),
        name="outproj",
    )(x2d, oa, ob, w_out_bf)


def kernel(x_prompt, x_sample, cache_kv_cmp, cache_kv_slc, cache_kv_win, state_hgrn, page_table, g_norm, w_in, w_out,
           g_q, g_k_slc, g_k_win, g_k_cmp, w_cmp_k, w_cmp_v, pe_cmp_k, pe_cmp_v, rel_bias, lb_logits, g_o_hgrn):
    depth = w_in.shape[0]
    assert depth == 1, "single-layer trunk"
    nb, t, _ = x_prompt.shape
    ns, ts, _ = x_sample.shape
    row = lambda a: a.astype(f32)[None]
    lower = jnp.cumsum(jax.nn.softmax(lb_logits.astype(f32), axis=0), axis=0)[0]
    w_main, w_gate = _prep_proj_weights(w_in[0])
    w_out_bf = w_out[0].astype(bf16)
    wk, pek = _prep_cmp_weights(w_cmp_k[0], pe_cmp_k[0])
    wv, pev = _prep_cmp_weights(w_cmp_v[0], pe_cmp_v[0])
    proj_args = (row(g_norm[0]), w_main, w_gate, row(g_q[0]), row(g_k_slc[0]), row(g_k_win[0]), row(lower))
    kv6 = lambda a, n_, t_: a.reshape(1, n_, t_, 2, A_KV_GROUPS, DH)

    xp = x_prompt.reshape(nb * t, D_MODEL)
    qa, kvc, kvs, kvw, gate, za, qb, f, ib, zb = _proj(xp, *proj_args, tm=512)
    s0 = jnp.zeros((nb, B_HEADS, DH, DH), f32)
    ob, st_p = _hgrn(qb, f, ib, zb, s0, row(g_o_hgrn[0]), nb, t, tc=128, sc=16, fc=64)
    kc, vct = _compress(kvc, wk, wv, pek, pev, row(g_k_cmp[0]), nb, t)
    oa = _nsa_prompt(qa, kc, vct, kvs, kvw, gate, za, rel_bias, nb, t)
    y_p = _outproj(xp, oa, ob, w_out_bf, tm=256).reshape(nb, t, D_MODEL)
    wlen = min(WINDOW, t)
    win_p = kv6(kvw, nb, t)[:, :, t - wlen:]

    xs = x_sample.reshape(ns * ts, D_MODEL)
    qa, kvc_s, kvs_s, kvw_s, gate, za, qb, f, ib, zb = _proj(xs, *proj_args, tm=512)
    ob, st_s = _hgrn(qb, f, ib, zb, state_hgrn[0].astype(f32), row(g_o_hgrn[0]), ns, ts, tc=ts, sc=ts)
    pool = cache_kv_cmp.shape[1]
    page = cache_kv_cmp.shape[2]
    oa, win_s = _nsa_sample(
        qa, kvs_s, kvw_s, cache_kv_cmp[0].reshape(pool, page * 4, DH), cache_kv_slc[0].reshape(pool, page * 4, DH),
        cache_kv_win[0].reshape(ns, -1, DH), page_table, wk, wv, pek, pev, row(g_k_cmp[0]), gate, za, rel_bias, ts)
    y_s = _outproj(xs, oa, ob, w_out_bf, tm=256).reshape(ns, ts, D_MODEL)

    return (y_p, y_s, kv6(kvc, nb, t), kv6(kvs, nb, t), win_p, st_p[None].astype(x_prompt.dtype),
            kv6(kvc_s, ns, ts), kv6(kvs_s, ns, ts), kv6(win_s, ns, WINDOW), st_s[None].astype(state_hgrn.dtype))
```

```python
import functools
import math

import jax
import jax.numpy as jnp
import numpy as np
from jax import lax
from jax.experimental import pallas as pl
from jax.experimental.pallas import tpu as pltpu

f32 = jnp.float32
bf16 = jnp.bfloat16

D_MODEL = 2048
A_HEADS = 8
A_KV_GROUPS = 2
A_HPG = A_HEADS // A_KV_GROUPS
DH = 128
A_WIDTH = A_HEADS * DH
CMP_BLOCK = 32
CMP_STRIDE = 16
SLC_BLOCK = 64
N_SELECT = 16
WINDOW = 512
B_HEADS = 8
B_WIDTH = B_HEADS * DH
REL_BUCKETS = 32
REL_MAX_DIST = 128
EPS = 1e-6
NEG = -1e30

VMEM_LIMIT = 56 * 1024 * 1024


def _sigmoid(x):
    return 1.0 / (1.0 + jnp.exp(-x))


def _head_norm(a, g):
    return a * lax.rsqrt(jnp.mean(a * a, axis=-1, keepdims=True) + EPS) * g


PROJ_TN = 512
PROJ_NT = 15
PROJ_SEG_STARTS = (0, 2, 3, 4, 5, 7, 9, 11, 13)
PROJ_W_SLOTS = 3
PROJ_TM = 512


def _proj_kernel(x_ref, gn_ref, w_hbm, wg_ref, gq_ref, gks_ref, gkw_ref, lb_ref,
                 qa_ref, kvc_ref, kvs_ref, kvw_ref, gate_ref, za_ref, qb_ref, f_ref, ib_ref, zb_ref, tail_ref,
                 h_scr, w_buf, w_sem):
    tm = x_ref.shape[0]

    nslot = w_buf.shape[0]

    def w_copy(j):
        return pltpu.make_async_copy(w_hbm.at[j], w_buf.at[j % nslot], w_sem.at[j % nslot])

    for j in range(nslot - 1):
        w_copy(j).start()
    x = x_ref[...]
    h = x * lax.rsqrt(jnp.mean(x * x, axis=-1, keepdims=True) + EPS) * gn_ref[...]
    h_scr[...] = h.astype(bf16)
    gate_ref[...] = _sigmoid(jnp.dot(h_scr[...], wg_ref[...], preferred_element_type=f32))

    for j in range(PROJ_NT):
        if j + nslot - 1 < PROJ_NT:
            w_copy(j + nslot - 1).start()
        w_copy(j).wait()
        acc = jnp.dot(h_scr[...], w_buf[j % nslot], preferred_element_type=f32)
        seg_start = max(s for s in PROJ_SEG_STARTS if s <= j)
        lanes = slice((j - seg_start) * PROJ_TN, (j - seg_start + 1) * PROJ_TN)

        def cols(c):
            return acc[:, c * DH:(c + 1) * DH]

        if seg_start == 0:
            gq = gq_ref[...]
            qa_ref[:, lanes] = jnp.concatenate(
                [_head_norm(cols(c), gq) * (DH ** -0.5) for c in range(4)], axis=1).astype(qa_ref.dtype)
        elif seg_start in (2, 3, 4):
            ref, g_ref = {2: (kvc_ref, None), 3: (kvs_ref, gks_ref), 4: (kvw_ref, gkw_ref)}[seg_start]
            for c in range(4):
                normed = g_ref is not None and c < 2
                val = _head_norm(cols(c), g_ref[...]) if normed else cols(c)
                ref[pl.ds(c, tm, stride=4), :] = val
                if seg_start == 4:
                    tail_ref[pl.ds(c, tm, stride=4), :] = val
        elif seg_start == 5:
            za_ref[:, lanes] = acc * _sigmoid(acc)
        elif seg_start == 7:
            qb_ref[:, lanes] = acc
        elif seg_start == 9:
            lb = lb_ref[:, lanes]
            f_ref[:, lanes] = lb + (1.0 - lb) * _sigmoid(acc)
        elif seg_start == 11:
            ib_ref[:, lanes] = acc
        else:
            zb_ref[:, lanes] = acc * _sigmoid(acc)


def _proj(x2d, gn, w_tiles, w_gate, gq, gks, gkw, lb, tm, t_seq):
    m = x2d.shape[0]
    assert m % tm == 0 and t_seq % tm == 0 and w_tiles.shape == (PROJ_NT, D_MODEL, PROJ_TN)
    tiles_per_seq = t_seq // tm
    row = lambda i: (i, 0)
    const = lambda i: (0, 0)
    out_shape = [
        jax.ShapeDtypeStruct((m, A_WIDTH), bf16),
        jax.ShapeDtypeStruct((m * 4, DH), f32),
        jax.ShapeDtypeStruct((m * 4, DH), f32),
        jax.ShapeDtypeStruct((m * 4, DH), f32),
        jax.ShapeDtypeStruct((m, A_KV_GROUPS * DH), f32),
        jax.ShapeDtypeStruct((m, A_WIDTH), f32),
        jax.ShapeDtypeStruct((m, B_WIDTH), f32),
        jax.ShapeDtypeStruct((m, B_WIDTH), f32),
        jax.ShapeDtypeStruct((m, B_WIDTH), f32),
        jax.ShapeDtypeStruct((m, B_WIDTH), f32),
    ]
    out_specs = [pl.BlockSpec((tm * 4, DH) if s.shape[0] == m * 4 else (tm, s.shape[1]), row) for s in out_shape]
    out_shape.append(jax.ShapeDtypeStruct((m // tiles_per_seq * 4, DH), f32))
    out_specs.append(pl.BlockSpec((tm * 4, DH), lambda i: (i // tiles_per_seq, 0)))
    in_specs = [
        pl.BlockSpec((tm, D_MODEL), row),
        pl.BlockSpec((1, D_MODEL), const),
        pl.BlockSpec(memory_space=pl.ANY),
        pl.BlockSpec((D_MODEL, A_KV_GROUPS * DH), const),
        pl.BlockSpec((1, DH), const),
        pl.BlockSpec((1, DH), const),
        pl.BlockSpec((1, DH), const),
        pl.BlockSpec((1, B_WIDTH), const),
    ]
    return pl.pallas_call(
        _proj_kernel, grid=(m // tm,), in_specs=in_specs, out_specs=out_specs, out_shape=out_shape,
        scratch_shapes=[pltpu.VMEM((tm, D_MODEL), bf16), pltpu.VMEM((PROJ_W_SLOTS, D_MODEL, PROJ_TN), bf16),
                        pltpu.SemaphoreType.DMA((PROJ_W_SLOTS,))],
        compiler_params=pltpu.CompilerParams(dimension_semantics=("arbitrary",), vmem_limit_bytes=VMEM_LIMIT),
        name="proj",
    )(x2d, gn, w_tiles, w_gate, gq, gks, gkw, lb)


def _prep_proj_weights(w_in):
    a0 = A_WIDTH + 6 * A_KV_GROUPS * DH
    a1 = a0 + 3 * A_HEADS
    w_main = jnp.concatenate([w_in[:, :a0], w_in[:, a1:]], axis=1).astype(bf16)
    w_main = w_main.reshape(D_MODEL, PROJ_NT, PROJ_TN).transpose(1, 0, 2)
    wg = w_in[:, a0:a1].reshape(-1, 3, A_KV_GROUPS, A_HPG).transpose(0, 2, 1, 3).reshape(-1, A_KV_GROUPS, 3 * A_HPG)
    w_gate = jnp.pad(wg, ((0, 0), (0, 0), (0, DH - 3 * A_HPG))).reshape(-1, A_KV_GROUPS * DH).astype(bf16)
    return w_main, w_gate


HGRN_SAFE_LOG_DECAY = 80.0


def _hgrn_exact(q, k, v, b, st):
    sc = q.shape[0]
    t_idx = lax.broadcasted_iota(jnp.int32, (sc, DH), 0)
    o = lax.dot_general((q * jnp.exp(b)).astype(bf16), st.astype(bf16), (((1,), (1,)), ((), ())),
                        preferred_element_type=f32)
    for s in range(sc):
        e = jnp.exp(jnp.minimum(b - b[s:s + 1, :], 0.0))
        a = jnp.where(t_idx >= s, q * e * k[s:s + 1, :], 0.0)
        o = o + jnp.sum(a, axis=-1, keepdims=True) * v[s:s + 1, :]
    bl = b[sc - 1:sc, :]
    ut = lax.dot_general(v.astype(bf16), (k * jnp.exp(bl - b)).astype(bf16), (((0,), (0,)), ((), ())),
                         preferred_element_type=f32)
    return o, st * jnp.exp(bl) + ut


def _hgrn_factored(q, k, v, b, st):
    fc = q.shape[0]
    qd = (q * jnp.exp(b)).astype(bf16)
    att = lax.dot_general(qd, (k * jnp.exp(-b)).astype(bf16), (((1,), (1,)), ((), ())), preferred_element_type=f32)
    tri = lax.broadcasted_iota(jnp.int32, (fc, fc), 0) >= lax.broadcasted_iota(jnp.int32, (fc, fc), 1)
    att = jnp.where(tri, att, 0.0)
    o = jnp.dot(att.astype(bf16), v.astype(bf16), preferred_element_type=f32)
    o = o + lax.dot_general(qd, st.astype(bf16), (((1,), (1,)), ((), ())), preferred_element_type=f32)
    bl = b[fc - 1:fc, :]
    ut = lax.dot_general(v.astype(bf16), (k * jnp.exp(bl - b)).astype(bf16), (((0,), (0,)), ((), ())),
                         preferred_element_type=f32)
    return o, st * jnp.exp(bl) + ut


def _hgrn_kernel(q_ref, f_ref, v_ref, zs_ref, s0_ref, go_ref, tri_ref, o_ref, sout_ref, st_scr, *, sc, fc):
    c = pl.program_id(1)
    tc = q_ref.shape[0]
    ns = st_scr.shape[0]
    t_seq = tc // ns

    @pl.when(c == 0)
    def _():
        for j in range(ns):
            for h in range(B_HEADS):
                st_scr[j, h] = s0_ref[j, h].T

    f = f_ref[...]
    b_all = lax.dot_general(tri_ref[...], jnp.log(f), (((1,), (0,)), ((), ())),
                            precision=lax.Precision.HIGHEST, preferred_element_type=f32)
    go = go_ref[...]

    def finish(rows, lanes, o):
        o_ref[rows, lanes] = _head_norm(o, go) * zs_ref[rows, lanes]

    def exact_rows(r0, n_rows, rebase):
        for h in range(B_HEADS):
            lanes = slice(h * DH, (h + 1) * DH)
            outs = []
            for i in range(n_rows // sc):
                rows = slice(r0 + i * sc, r0 + (i + 1) * sc)
                b = b_all[rows, lanes]
                if rebase and i > 0:
                    b = b - b_all[r0 + i * sc - 1:r0 + i * sc, lanes]
                j = (r0 + i * sc) // t_seq
                o, st = _hgrn_exact(q_ref[rows, lanes], 1.0 - f[rows, lanes], v_ref[rows, lanes], b, st_scr[j, h])
                st_scr[j, h] = st
                outs.append(o)
            finish(slice(r0, r0 + n_rows), lanes, jnp.concatenate(outs, axis=0) if len(outs) > 1 else outs[0])

    if fc is None:
        exact_rows(0, tc, False)
    else:
        for i in range(tc // fc):
            r0 = i * fc
            rows = slice(r0, r0 + fc)
            total = jnp.max(-b_all[r0 + fc - 1:r0 + fc, :])

            @pl.when(total < HGRN_SAFE_LOG_DECAY)
            def _():
                for h in range(B_HEADS):
                    lanes = slice(h * DH, (h + 1) * DH)
                    o, st = _hgrn_factored(q_ref[rows, lanes], 1.0 - f[rows, lanes], v_ref[rows, lanes],
                                           b_all[rows, lanes], st_scr[r0 // t_seq, h])
                    st_scr[r0 // t_seq, h] = st
                    finish(rows, lanes, o)

            @pl.when(jnp.logical_not(total < HGRN_SAFE_LOG_DECAY))
            def _():
                exact_rows(r0, fc, True)

    @pl.when(c == pl.num_programs(1) - 1)
    def _():
        for j in range(ns):
            for h in range(B_HEADS):
                sout_ref[j, h] = st_scr[j, h].T


def _block_tri(tc, blk):
    r = np.arange(tc)
    return jnp.asarray(((r[:, None] // blk == r[None, :] // blk) & (r[None, :] <= r[:, None])).astype(np.float32))


def _hgrn(qb, f, ib, zs, s0, go, n, t, tc, sc, fc=None, ns=1):
    assert tc % sc == 0 and (fc is None or (tc % fc == 0 and fc % sc == 0))
    assert (ns == 1 and t % tc == 0) or (tc == ns * t and n % ns == 0 and fc is None and t % sc == 0)
    nc = max(t // tc, 1)
    rows = lambda i, c: (i * nc + c, 0)
    st = lambda i, c: (i, 0, 0, 0)
    const = lambda i, c: (0, 0)
    return pl.pallas_call(
        functools.partial(_hgrn_kernel, sc=sc, fc=fc),
        grid=(n // ns, nc),
        in_specs=[pl.BlockSpec((tc, B_WIDTH), rows)] * 4 + [
            pl.BlockSpec((ns, B_HEADS, DH, DH), st),
            pl.BlockSpec((1, DH), const),
            pl.BlockSpec((tc, tc), const),
        ],
        out_specs=[pl.BlockSpec((tc, B_WIDTH), rows), pl.BlockSpec((ns, B_HEADS, DH, DH), st)],
        out_shape=[jax.ShapeDtypeStruct((n * t, B_WIDTH), f32), jax.ShapeDtypeStruct((n, B_HEADS, DH, DH), f32)],
        scratch_shapes=[pltpu.VMEM((ns, B_HEADS, DH, DH), f32)],
        compiler_params=pltpu.CompilerParams(
            dimension_semantics=("arbitrary", "arbitrary"), vmem_limit_bytes=VMEM_LIMIT),
        name="hgrn",
    )(qb, f, ib, zs, s0, go, _block_tri(tc, sc if fc is None else fc))


def _rel_bucket_np(dist):
    n = np.maximum(dist, 0)
    exact = REL_BUCKETS // 2
    scale = np.float32((REL_BUCKETS - exact) / math.log(REL_MAX_DIST / exact))
    large = exact + (np.log(np.maximum(n, exact).astype(np.float32) / np.float32(exact)) * scale).astype(np.int32)
    return np.where(n < exact, n, np.minimum(large, REL_BUCKETS - 1)).astype(np.int32)


def _bucket_starts():
    buckets = _rel_bucket_np(np.arange(2 * REL_MAX_DIST))
    assert np.all(np.diff(buckets) >= 0) and buckets[-1] == REL_BUCKETS - 1
    return [int(np.argmax(buckets >= b)) for b in range(REL_BUCKETS)]


def _bias_table_kernel(rb_ref, o_ref, *, off, a0, a1, sub_far):
    h = pl.program_id(0)
    shape = o_ref.shape[1:]
    d = off + a0 * lax.broadcasted_iota(jnp.int32, shape, 0) + a1 * lax.broadcasted_iota(jnp.int32, shape, 1)
    starts = _bucket_starts()
    far = rb_ref[REL_BUCKETS - 1, h]
    v = jnp.full(shape, far, f32)
    for b in range(REL_BUCKETS - 2, -1, -1):
        v = jnp.where(d < starts[b + 1], rb_ref[b, h], v)
    o_ref[0] = v - far if sub_far else v


def _bias_table(rel_bias, shape, off, a0, a1, sub_far):
    return pl.pallas_call(
        functools.partial(_bias_table_kernel, off=off, a0=a0, a1=a1, sub_far=sub_far),
        grid=(A_HEADS,),
        in_specs=[pl.BlockSpec(memory_space=pltpu.SMEM)],
        out_specs=pl.BlockSpec((1,) + tuple(shape), lambda h: (h, 0, 0)),
        out_shape=jax.ShapeDtypeStruct((A_HEADS,) + tuple(shape), f32),
        compiler_params=pltpu.CompilerParams(dimension_semantics=("arbitrary",)),
        name="bias_table",
    )(rel_bias.astype(f32))


def _cmp_bias_t(rel_bias, t0, nt, n_chunk):
    return _bias_table(rel_bias, (n_chunk, nt), t0 - (CMP_BLOCK - 1), -CMP_STRIDE, 1, False)


def _near_tiles_t(rel_bias):
    return _bias_table(rel_bias, (QB, 2 * QB), 0, -1, 1, True)


def _compress_combine(lhs, wcat, pe2):
    out = jnp.dot(lhs, wcat, preferred_element_type=f32)
    pc = jnp.dot(pe2, wcat, preferred_element_type=f32)
    const = pc[0:1, :DH] + pc[1:2, DH:]
    r = lhs.shape[0]
    return out[:, :DH] + pltpu.roll(out[:, DH:], r - 1, 0) + const


def _compress_kernel(kv_ref, wk_ref, wv_ref, pek_ref, pev_ref, gk_ref, kc_ref, vct_ref):
    nchunk = kv_ref.shape[0] // (4 * CMP_STRIDE)

    def chunk_rows(cg):
        return jnp.concatenate(
            [kv_ref[pl.ds(4 * l + cg, nchunk, stride=4 * CMP_STRIDE), :].astype(bf16) for l in range(CMP_STRIDE)],
            axis=1)

    kc = _compress_combine(jnp.concatenate([chunk_rows(0), chunk_rows(1)], axis=0), wk_ref[...], pek_ref[...])
    vc = _compress_combine(jnp.concatenate([chunk_rows(2), chunk_rows(3)], axis=0), wv_ref[...], pev_ref[...])
    kc = _head_norm(kc, gk_ref[...])
    for g in range(A_KV_GROUPS):
        kc_ref[0, g] = kc[g * nchunk:(g + 1) * nchunk].astype(kc_ref.dtype)
        vct_ref[0, g] = vc[g * nchunk:(g + 1) * nchunk].T.astype(vct_ref.dtype)


def _prep_cmp_weights(w, pe):
    half = CMP_STRIDE * DH
    wcat = jnp.concatenate([w[:half], w[half:]], axis=1).astype(bf16)
    return wcat, pe.reshape(2, half).astype(bf16)


def _compress(kvc, wk, wv, pek, pev, gk, n, t):
    nchunk = t // CMP_STRIDE
    assert nchunk == DH, "the transposed v_c block is square"
    const = lambda i: (0, 0)
    return pl.pallas_call(
        _compress_kernel, grid=(n,),
        in_specs=[pl.BlockSpec((t * 4, DH), lambda i: (i, 0)),
                  pl.BlockSpec(wk.shape, const), pl.BlockSpec(wv.shape, const),
                  pl.BlockSpec(pek.shape, const), pl.BlockSpec(pev.shape, const), pl.BlockSpec((1, DH), const)],
        out_specs=[pl.BlockSpec((1, A_KV_GROUPS, nchunk, DH), lambda i: (i, 0, 0, 0))] * 2,
        out_shape=[jax.ShapeDtypeStruct((n, A_KV_GROUPS, nchunk, DH), bf16)] * 2,
        compiler_params=pltpu.CompilerParams(dimension_semantics=("arbitrary",), vmem_limit_bytes=VMEM_LIMIT),
        name="compress",
    )(kvc, wk, wv, pek, pev, gk)


def _softmax_tile(carry, s, mask, v_bf):
    m, l, acc = carry
    s = jnp.where(mask, s, NEG)
    m_new = jnp.maximum(m, jnp.max(s, axis=-1, keepdims=True))
    alpha = jnp.exp(m - m_new)
    p = jnp.where(mask, jnp.exp(s - m_new), 0.0)
    l = alpha * l + jnp.sum(p, axis=-1, keepdims=True)
    h, r, k = p.shape
    pv = jnp.dot(p.reshape(h * r, k).astype(bf16), v_bf, preferred_element_type=f32).reshape(h, r, DH)
    return m_new, l, alpha * acc + pv


def _softmax_finish(carry):
    m, l, acc = carry
    return acc / jnp.where(l > 0, l, 1.0)


def _select_blocks(score_t, tpos, n_blocks):
    nb = score_t.shape[0]
    j = lax.broadcasted_iota(jnp.int32, score_t.shape, 0)
    valid = (j * SLC_BLOCK <= tpos) & (j < n_blocks)
    cur = tpos >> 6
    forced = (j == 0) | (j == cur) | (j == cur - 1)
    val = jnp.where(valid, jnp.where(forced, 1e30, score_t), -1.0)
    rank = jnp.zeros(score_t.shape, jnp.int32)
    for i in range(n_blocks):
        vi = val[i:i + 1, :]
        beats = (vi > val) | ((vi == val) & (i < j))
        rank = rank + beats.astype(jnp.int32)
    return jnp.where((rank < N_SELECT) & valid, 1.0, 0.0)


QB = 128


NEG_M = -1e30
NEG_S = -2e30


def _nsa_prompt_kernel(q_ref, kc_ref, vct_ref, bc_ref, kvs_ref, kvw_ref, gate_ref, za_ref, bt_ref, mt_ref, et_ref,
                       o_ref, ks_scr, vts_scr, kw_scr, vtw_scr, km_scr):
    qb = pl.program_id(1)
    q0 = qb * QB
    nkt = vts_scr.shape[1]
    nq4 = A_HPG * QB
    groups = range(A_KV_GROUPS)

    @pl.when(qb == 0)
    def _():
        for g in groups:
            for kt in range(nkt):
                rows = slice(kt * QB, (kt + 1) * QB)
                base = kt * 4 * QB
                ks_scr[g, rows, :] = kvs_ref[pl.ds(base + g, QB, stride=4), :].astype(bf16)
                vts_scr[g, kt] = kvs_ref[pl.ds(base + 2 + g, QB, stride=4), :].T.astype(bf16)
                kw_scr[g, rows, :] = kvw_ref[pl.ds(base + g, QB, stride=4), :].astype(bf16)
                vtw_scr[g, kt] = kvw_ref[pl.ds(base + 2 + g, QB, stride=4), :].T.astype(bf16)

    qall = q_ref[...]
    qts = [jnp.concatenate([qall[:, (g * A_HPG + h) * DH:(g * A_HPG + h + 1) * DH].astype(f32).T
                            for h in range(A_HPG)], axis=1).astype(bf16) for g in groups]
    kk = lax.broadcasted_iota(jnp.int32, (QB, nq4), 0)
    tt = lax.broadcasted_iota(jnp.int32, (QB, nq4), 1) & (QB - 1)

    def heads(ref, g, *idx):
        return jnp.concatenate([ref[(g * A_HPG + h,) + idx] for h in range(A_HPG)], axis=1)

    o_c = []
    for g in groups:
        sc = jnp.dot(kc_ref[0, g], qts[g], preferred_element_type=f32) + heads(bc_ref, g)
        mask_c = q0 + tt >= CMP_STRIDE * kk + (CMP_BLOCK - 1)
        sc = jnp.where(mask_c, sc, NEG)
        mc = jnp.max(sc, axis=0, keepdims=True)
        ec = jnp.where(mask_c, jnp.exp(sc - mc), 0.0)
        lc = jnp.sum(ec, axis=0, keepdims=True)
        pc = ec / jnp.where(lc > 0, lc, 1.0)
        o_c.append(jnp.dot(vct_ref[0, g], pc.astype(bf16), preferred_element_type=f32))
        ps = pc[:, 0:QB] + pc[:, QB:2 * QB] + pc[:, 2 * QB:3 * QB] + pc[:, 3 * QB:4 * QB]
        score_t = jnp.dot(mt_ref[...], ps, precision=lax.Precision.HIGHEST, preferred_element_type=f32)
        nb = score_t.shape[0]
        sel_t = _select_blocks(score_t, q0 + lax.broadcasted_iota(jnp.int32, (nb, QB), 1), nb)
        km_scr[g] = jnp.dot(et_ref[...], sel_t.astype(bf16), preferred_element_type=f32)

    def tile(carry, qt, k_tile, vt_tile, mask, bias):
        m, l, acc = carry
        s = jnp.dot(k_tile, qt, preferred_element_type=f32)
        if bias is not None:
            s = s + bias
        s = jnp.where(mask, s, NEG_S)
        m_new = jnp.maximum(m, jnp.max(s, axis=0, keepdims=True))
        alpha = jnp.exp(m - m_new)
        p = jnp.exp(s - m_new)
        l = alpha * l + jnp.sum(p, axis=0, keepdims=True)
        return m_new, l, alpha * acc + jnp.dot(vt_tile, p.astype(bf16), preferred_element_type=f32)

    def init():
        return jnp.full((1, nq4), NEG_M, f32), jnp.zeros((1, nq4), f32), jnp.zeros((DH, nq4), f32)

    def finish(carry):
        m, l, acc = carry
        return acc / jnp.where(l > 0, l, 1.0)

    def key_rows(kt):
        return pl.ds(pl.multiple_of(kt * QB, QB), QB)

    def lanes4(x):
        return jnp.concatenate([x] * A_HPG, axis=1)

    def near_span(g, k_scr, vt_scr, rs, use_sel):
        ks, vts, masks, biases = [], [], [], []
        for r in rs:
            kt = jnp.maximum(qb - r, 0)
            ks.append(k_scr[g, key_rows(kt), :])
            vts.append(vt_scr[g, kt])
            mask = kk * 0 + (qb - r) >= 0
            if use_sel:
                mask = mask & (lanes4(km_scr[g, key_rows(kt), :]) > 0.5)
            if r == 0:
                mask = mask & (kk <= tt)
            if r == WINDOW // QB:
                mask = mask & (kk >= tt)
            masks.append(mask)
            biases.append(heads(bt_ref, g, slice(None), slice(r * QB, (r + 1) * QB)) if r < 2
                          else jnp.zeros((QB, nq4), f32))
        cat = lambda xs, axis: jnp.concatenate(xs, axis=axis)
        return cat(ks, 0), cat(vts, 1), cat(masks, 0), cat(biases, 0)

    o_w = [finish(tile(init(), qts[g], *near_span(g, kw_scr, vtw_scr, (4, 3, 2, 1, 0), False))) for g in groups]

    n_far = jnp.maximum(qb - 1, 0)
    kk2 = lax.broadcasted_iota(jnp.int32, (2 * QB, nq4), 0)

    def far(i, carries):
        rows = pl.ds(pl.multiple_of(i * 2 * QB, 2 * QB), 2 * QB)
        in_range = (kk2 + i * 2 * QB) < n_far * QB
        out = []
        for g in groups:
            mask = (lanes4(km_scr[g, rows, :]) > 0.5) & in_range
            vt = jnp.concatenate([vts_scr[g, 2 * i], vts_scr[g, jnp.minimum(2 * i + 1, nkt - 1)]], axis=1)
            out.append(tile(carries[g], qts[g], ks_scr[g, rows, :], vt, mask, None))
        return tuple(out)

    carries = lax.fori_loop(0, (n_far + 1) // 2, far, tuple(init() for g in groups))
    o_s = [finish(tile(carries[g], qts[g], *near_span(g, ks_scr, vts_scr, (1, 0), True))) for g in groups]

    za = za_ref[...]
    outs = []
    for g in groups:
        gate_t = gate_ref[:, g * DH:(g + 1) * DH].T
        for h in range(A_HPG):
            cols = slice(h * QB, (h + 1) * QB)

            def grow(br):
                return gate_t[br * A_HPG + h:br * A_HPG + h + 1, :]
            o = grow(0) * o_c[g][:, cols] + grow(1) * o_s[g][:, cols] + grow(2) * o_w[g][:, cols]
            outs.append(o.T * za[:, (g * A_HPG + h) * DH:(g * A_HPG + h + 1) * DH])
    o_ref[...] = jnp.concatenate(outs, axis=1)


def _cmp_to_slc_t(n_cmp_pad, n_slc_pad, n_cmp, n_slc):
    c0 = np.arange(n_cmp_pad)[None, :] * CMP_STRIDE
    s0 = np.arange(n_slc_pad)[:, None] * SLC_BLOCK
    ov = np.minimum(c0 + CMP_BLOCK, s0 + SLC_BLOCK) - np.maximum(c0, s0)
    m = np.maximum(ov, 0).astype(np.float32) / CMP_STRIDE
    m[:, n_cmp:] = 0
    m[n_slc:, :] = 0
    return m


def _nsa_prompt(qa, kc, vct, kvs, kvw, gate, za, rel_bias, n, t):
    nq = t // QB
    n_cmp = (t - CMP_BLOCK) // CMP_STRIDE + 1
    n_slc = -(-t // SLC_BLOCK)
    assert kc.shape[2] == QB and n_slc % 8 == 0
    bias_c = _cmp_bias_t(rel_bias, 0, t, QB)
    bt = _near_tiles_t(rel_bias)
    mt = jnp.asarray(_cmp_to_slc_t(QB, n_slc, n_cmp, n_slc))
    et = jnp.asarray((np.arange(t)[:, None] // SLC_BLOCK == np.arange(n_slc)[None, :]).astype(np.float32), dtype=bf16)
    qrow = lambda i, b: (i * nq + b, 0)
    seq = lambda i, b: (i, 0, 0, 0)
    ng = A_KV_GROUPS
    return pl.pallas_call(
        _nsa_prompt_kernel, grid=(n, nq),
        in_specs=[
            pl.BlockSpec((QB, A_WIDTH), qrow),
            pl.BlockSpec((1, ng, QB, DH), seq),
            pl.BlockSpec((1, ng, DH, QB), seq),
            pl.BlockSpec((A_HEADS, QB, QB), lambda i, b: (0, 0, b)),
            pl.BlockSpec((t * 4, DH), lambda i, b: (i, 0)),
            pl.BlockSpec((t * 4, DH), lambda i, b: (i, 0)),
            pl.BlockSpec((QB, ng * DH), qrow),
            pl.BlockSpec((QB, A_WIDTH), qrow),
            pl.BlockSpec((A_HEADS, QB, 2 * QB), lambda i, b: (0, 0, 0)),
            pl.BlockSpec(mt.shape, lambda i, b: (0, 0)),
            pl.BlockSpec(et.shape, lambda i, b: (0, 0)),
        ],
        out_specs=pl.BlockSpec((QB, A_WIDTH), qrow),
        out_shape=jax.ShapeDtypeStruct((n * t, A_WIDTH), f32),
        scratch_shapes=[pltpu.VMEM((ng, t, DH), bf16), pltpu.VMEM((ng, nq, DH, QB), bf16),
                        pltpu.VMEM((ng, t, DH), bf16), pltpu.VMEM((ng, nq, DH, QB), bf16),
                        pltpu.VMEM((ng, t, QB), f32)],
        compiler_params=pltpu.CompilerParams(
            dimension_semantics=("arbitrary", "arbitrary"), vmem_limit_bytes=VMEM_LIMIT),
        name="nsa_prompt",
    )(qa, kc, vct, bias_c, kvs, kvw, gate, za, bt, mt, et)


CMP_PITCH = 24


def _nsa_sample_kernel(pt_ref, *refs, n_pages, page, t_new, n_blocks):
    del pt_ref
    cmp_pages = refs[:n_pages]
    slc_pages = refs[n_pages:2 * n_pages]
    (q_ref, kvs_new_ref, kvw_new_ref, win_ref, wk_ref, wv_ref, pek_ref, pev_ref, gk_ref, bc_ref, bl_ref, bn_ref,
     gate_ref, za_ref, mt_ref, e_ref, o_ref, win_out_ref, xc_scr) = refs[2 * n_pages:]
    past = n_pages * page
    nchunk = past // CMP_STRIDE
    pad_new = QB
    ng = A_KV_GROUPS
    groups = range(ng)

    cpp = page // CMP_STRIDE
    for p in range(n_pages):
        for cg in range(4):
            x = cmp_pages[p][0, pl.ds(cg, page, stride=4), :]
            for c in range(cpp):
                r0 = (p * cpp + c) * CMP_PITCH
                xc_scr[cg, r0:r0 + CMP_STRIDE, :] = x[c * CMP_STRIDE:(c + 1) * CMP_STRIDE]

    def chunk_rows(cg):
        return jnp.concatenate(
            [xc_scr[cg, pl.ds(l, nchunk, stride=CMP_PITCH), :].astype(bf16) for l in range(CMP_STRIDE)], axis=1)

    kc = _compress_combine(jnp.concatenate([chunk_rows(0), chunk_rows(1)], axis=0), wk_ref[...], pek_ref[...])
    vc = _compress_combine(jnp.concatenate([chunk_rows(2), chunk_rows(3)], axis=0), wv_ref[...], pev_ref[...])
    kc = _head_norm(kc, gk_ref[...]).astype(bf16)
    vc = vc.astype(bf16)

    qall = q_ref[...]
    qs = [jnp.concatenate([qall[:, (g * A_HPG + h) * DH:(g * A_HPG + h + 1) * DH] for h in range(A_HPG)], axis=0)
          for g in groups]
    zeros_pad = jnp.zeros((pad_new - t_new, DH), bf16)

    def new_rows(ref, c):
        return jnp.concatenate([ref[pl.ds(c, t_new, stride=4), :].astype(bf16), zeros_pad], axis=0)

    def logits(ks):
        return jnp.stack([lax.dot_general(qs[g], ks[g], (((1,), (1,)), ((), ())), preferred_element_type=f32)
                          .reshape(A_HPG, t_new, ks[g].shape[0]) for g in groups], axis=0)

    def softmax_pv(s, mask, vs):
        s = jnp.where(mask, s, NEG)
        e = jnp.where(mask, jnp.exp(s - jnp.max(s, axis=-1, keepdims=True)), 0.0)
        l = jnp.sum(e, axis=-1, keepdims=True)
        p = e / jnp.where(l > 0, l, 1.0)
        o = jnp.stack([jnp.dot(p[g].reshape(A_HPG * t_new, -1).astype(bf16), vs[g], preferred_element_type=f32)
                       .reshape(A_HPG, t_new, DH) for g in groups], axis=0)
        return o, p

    def tail_bias(s):
        bias = jnp.concatenate([bl_ref[...], bn_ref[...]], axis=-1).reshape(ng, A_HPG, t_new, 2 * QB)
        nk = s.shape[-1]
        return jnp.concatenate([s[..., :nk - 2 * QB], s[..., nk - 2 * QB:] + bias], axis=-1)

    ti = lax.broadcasted_iota(jnp.int32, (t_new, nchunk), 0)
    ci = lax.broadcasted_iota(jnp.int32, (t_new, nchunk), 1)
    cmask = (past + ti >= CMP_STRIDE * ci + (CMP_BLOCK - 1))[None, None]
    sc = logits([kc[g * nchunk:(g + 1) * nchunk] for g in groups]) + bc_ref[...].reshape(ng, A_HPG, t_new, nchunk)
    o_c, pc = softmax_pv(sc, cmask, [vc[g * nchunk:(g + 1) * nchunk] for g in groups])

    ps = jnp.concatenate([pc[g, 0] + pc[g, 1] + pc[g, 2] + pc[g, 3] for g in groups], axis=0)
    score_t = lax.dot_general(mt_ref[...], ps, (((1,), (1,)), ((), ())),
                              precision=lax.Precision.HIGHEST, preferred_element_type=f32)
    lane = lax.broadcasted_iota(jnp.int32, score_t.shape, 1)
    tpos = past + jnp.where(lane >= t_new, lane - t_new, lane)
    sel_t = _select_blocks(score_t, tpos, n_blocks)
    key_mask = lax.dot_general(sel_t.astype(bf16), e_ref[...], (((0,), (0,)), ((), ())),
                               preferred_element_type=f32)

    tt = lax.broadcasted_iota(jnp.int32, (t_new, pad_new), 0)
    uu = lax.broadcasted_iota(jnp.int32, (t_new, pad_new), 1)
    new_mask = uu <= tt
    new_mask_f = jnp.where(new_mask, 1.0, 0.0)
    sel_mask = jnp.concatenate([key_mask, jnp.concatenate([new_mask_f] * ng, axis=0)], axis=1)
    sel_mask = sel_mask.reshape(ng, 1, t_new, past + pad_new) > 0.5
    ks = [jnp.concatenate([slc_pages[p][0, pl.ds(g, page, stride=4), :].astype(bf16) for p in range(n_pages)]
                          + [new_rows(kvs_new_ref, g)], axis=0) for g in groups]
    vs = [jnp.concatenate([slc_pages[p][0, pl.ds(2 + g, page, stride=4), :].astype(bf16) for p in range(n_pages)]
                          + [new_rows(kvs_new_ref, 2 + g)], axis=0) for g in groups]
    o_s, _ = softmax_pv(tail_bias(logits(ks)), sel_mask, vs)

    nw = win_ref.shape[1] // 4
    kw = [jnp.concatenate([win_ref[0, pl.ds(g, nw, stride=4), :].astype(bf16), new_rows(kvw_new_ref, g)], axis=0)
          for g in groups]
    vw = [jnp.concatenate([win_ref[0, pl.ds(2 + g, nw, stride=4), :].astype(bf16), new_rows(kvw_new_ref, 2 + g)], axis=0)
          for g in groups]
    wmask = lax.broadcasted_iota(jnp.int32, (t_new, nw), 1) >= lax.broadcasted_iota(jnp.int32, (t_new, nw), 0)
    o_w, _ = softmax_pv(tail_bias(logits(kw)), jnp.concatenate([wmask, new_mask], axis=1)[None, None], vw)

    gate_all = gate_ref[...]
    za = za_ref[...]
    outs = []
    for g in groups:
        gate = gate_all[:, g * DH:(g + 1) * DH]
        for h in range(A_HPG):
            def gcol(br):
                return gate[:, br * A_HPG + h:br * A_HPG + h + 1]
            o = gcol(0) * o_c[g, h] + gcol(1) * o_s[g, h] + gcol(2) * o_w[g, h]
            outs.append(o * za[:, (g * A_HPG + h) * DH:(g * A_HPG + h + 1) * DH])
    o_ref[...] = jnp.concatenate(outs, axis=1)

    nrow = win_ref.shape[1]
    win_out_ref[0, 0:nrow - 4 * t_new, :] = win_ref[0, 4 * t_new:nrow, :]
    win_out_ref[0, nrow - 4 * t_new:nrow, :] = kvw_new_ref[...]


def _nsa_sample(qa, kvs_new, kvw_new, cache_cmp, cache_slc, cache_win, page_table, wk, wv, pek, pev, gk,
                gate, za, rel_bias, t_new):
    n, n_pages = page_table.shape
    page = cache_cmp.shape[1] // 4
    past = n_pages * page
    wb = cache_win.shape[1] // 4
    assert wb == WINDOW and past % QB == 0 and past >= WINDOW
    nchunk = past // CMP_STRIDE
    n_cmp = (past + t_new - CMP_BLOCK) // CMP_STRIDE + 1
    assert n_cmp < nchunk
    n_slc = -(-(past + t_new) // SLC_BLOCK)
    nb_pad = -(-n_slc // 8) * 8
    bias_c = _bias_table(rel_bias, (t_new, nchunk), past - (CMP_BLOCK - 1), 1, -CMP_STRIDE, False)
    b_last = _bias_table(rel_bias, (t_new, QB), QB, 1, -1, True)
    b_new = _bias_table(rel_bias, (t_new, QB), 0, 1, -1, True)
    mt = jnp.asarray(_cmp_to_slc_t(nchunk, nb_pad, n_cmp, n_slc))
    e = jnp.asarray((np.arange(past)[None, :] // SLC_BLOCK == np.arange(nb_pad)[:, None]).astype(np.float32), dtype=bf16)

    def page_spec(p):
        return pl.BlockSpec((1, page * 4, DH), lambda i, pt: (pt[i, p], 0, 0))

    rowblk = lambda w: pl.BlockSpec((t_new, w), lambda i, pt: (i, 0))
    full = lambda a: pl.BlockSpec(a.shape, lambda i, pt: (0,) * a.ndim)
    in_specs = ([page_spec(p) for p in range(n_pages)] * 2 + [
        rowblk(A_WIDTH),
        pl.BlockSpec((t_new * 4, DH), lambda i, pt: (i, 0)),
        pl.BlockSpec((t_new * 4, DH), lambda i, pt: (i, 0)),
        pl.BlockSpec((1, wb * 4, DH), lambda i, pt: (i, 0, 0)),
        full(wk), full(wv), full(pek), full(pev), full(gk), full(bias_c), full(b_last), full(b_new),
        rowblk(A_KV_GROUPS * DH), rowblk(A_WIDTH), full(mt), full(e)])
    grid_spec = pltpu.PrefetchScalarGridSpec(
        num_scalar_prefetch=1, grid=(n,), in_specs=in_specs,
        out_specs=[rowblk(A_WIDTH), pl.BlockSpec((1, wb * 4, DH), lambda i, pt: (i, 0, 0))],
        scratch_shapes=[pltpu.VMEM((4, nchunk * CMP_PITCH, DH), f32)])
    return pl.pallas_call(
        functools.partial(_nsa_sample_kernel, n_pages=n_pages, page=page, t_new=t_new, n_blocks=n_slc),
        grid_spec=grid_spec,
        out_shape=[jax.ShapeDtypeStruct((n * t_new, A_WIDTH), f32), jax.ShapeDtypeStruct(cache_win.shape, f32)],
        compiler_params=pltpu.CompilerParams(dimension_semantics=("arbitrary",), vmem_limit_bytes=VMEM_LIMIT),
        name="nsa_sample",
    )(page_table, *([cache_cmp] * n_pages), *([cache_slc] * n_pages), qa, kvs_new, kvw_new, cache_win,
      wk, wv, pek, pev, gk, bias_c, b_last, b_new, gate, za, mt, e)


def _outproj_kernel(x_ref, oa_ref, ob_ref, w_ref, y_ref):
    acc = jnp.dot(oa_ref[...].astype(bf16), w_ref[0:A_WIDTH, :], preferred_element_type=f32)
    acc = acc + jnp.dot(ob_ref[...].astype(bf16), w_ref[A_WIDTH:, :], preferred_element_type=f32)
    y_ref[...] = x_ref[...] + acc


def _outproj(x2d, oa, ob, w_out_bf, tm):
    m = x2d.shape[0]
    assert m % tm == 0
    row = lambda i: (i, 0)
    return pl.pallas_call(
        _outproj_kernel, grid=(m // tm,),
        in_specs=[pl.BlockSpec((tm, D_MODEL), row), pl.BlockSpec((tm, A_WIDTH), row), pl.BlockSpec((tm, B_WIDTH), row),
                  pl.BlockSpec(w_out_bf.shape, lambda i: (0, 0))],
        out_specs=pl.BlockSpec((tm, D_MODEL), row),
        out_shape=jax.ShapeDtypeStruct((m, D_MODEL), f32),
        compiler_params=pltpu.CompilerParams(dimension_semantics=("arbitrary",), vmem_limit_bytes=VMEM_LIMIT),
        name="outproj",
    )(x2d, oa, ob, w_out_bf)


def kernel(x_prompt, x_sample, cache_kv_cmp, cache_kv_slc, cache_kv_win, state_hgrn, page_table, g_norm, w_in, w_out,
           g_q, g_k_slc, g_k_win, g_k_cmp, w_cmp_k, w_cmp_v, pe_cmp_k, pe_cmp_v, rel_bias, lb_logits, g_o_hgrn):
    depth = w_in.shape[0]
    assert depth == 1, "single-layer trunk"
    nb, t, _ = x_prompt.shape
    ns, ts, _ = x_sample.shape
    row = lambda a: a.astype(f32)[None]
    lower = jnp.cumsum(jax.nn.softmax(lb_logits.astype(f32), axis=0), axis=0)[0]
    w_main, w_gate = _prep_proj_weights(w_in[0])
    w_out_bf = w_out[0].astype(bf16)
    wk, pek = _prep_cmp_weights(w_cmp_k[0], pe_cmp_k[0])
    wv, pev = _prep_cmp_weights(w_cmp_v[0], pe_cmp_v[0])
    proj_args = (row(g_norm[0]), w_main, w_gate, row(g_q[0]), row(g_k_slc[0]), row(g_k_win[0]), row(lower))
    kv6 = lambda a, n_, t_: a.reshape(1, n_, t_, 2, A_KV_GROUPS, DH)

    xp = x_prompt.reshape(nb * t, D_MODEL)
    wlen = min(WINDOW, t)
    assert wlen == PROJ_TM, "the proj row tile doubles as the prompt's final window"
    qa, kvc, kvs, kvw, gate, za, qb, f, ib, zb, win_p = _proj(xp, *proj_args, tm=PROJ_TM, t_seq=t)
    s0 = jnp.zeros((nb, B_HEADS, DH, DH), f32)
    ob, st_p = _hgrn(qb, f, ib, zb, s0, row(g_o_hgrn[0]), nb, t, tc=128, sc=16, fc=64)
    kc, vct = _compress(kvc, wk, wv, pek, pev, row(g_k_cmp[0]), nb, t)
    oa = _nsa_prompt(qa, kc, vct, kvs, kvw, gate, za, rel_bias, nb, t)
    y_p = _outproj(xp, oa, ob, w_out_bf, tm=PROJ_TM).reshape(nb, t, D_MODEL)
    win_p = kv6(win_p, nb, wlen)

    xs = x_sample.reshape(ns * ts, D_MODEL)
    qa, kvc_s, kvs_s, kvw_s, gate, za, qb, f, ib, zb, _ = _proj(xs, *proj_args, tm=PROJ_TM, t_seq=PROJ_TM)
    seq_per_step = 8
    ob, st_s = _hgrn(qb, f, ib, zb, state_hgrn[0].astype(f32), row(g_o_hgrn[0]), ns, ts, tc=seq_per_step * ts, sc=ts,
                     ns=seq_per_step)
    pool = cache_kv_cmp.shape[1]
    page = cache_kv_cmp.shape[2]
    oa, win_s = _nsa_sample(
        qa, kvs_s, kvw_s, cache_kv_cmp[0].reshape(pool, page * 4, DH), cache_kv_slc[0].reshape(pool, page * 4, DH),
        cache_kv_win[0].reshape(ns, -1, DH), page_table, wk, wv, pek, pev, row(g_k_cmp[0]), gate, za, rel_bias, ts)
    y_s = _outproj(xs, oa, ob, w_out_bf, tm=PROJ_TM).reshape(ns, ts, D_MODEL)

    return (y_p, y_s, kv6(kvc, nb, t), kv6(kvs, nb, t), win_p, st_p[None].astype(x_prompt.dtype),
            kv6(kvc_s, ns, ts), kv6(kvs_s, ns, ts), kv6(win_s, ns, WINDOW), st_s[None].astype(state_hgrn.dtype))
```

```python
import functools
import math

import jax
import jax.numpy as jnp
import numpy as np
from jax import lax
from jax.experimental import pallas as pl
from jax.experimental.pallas import tpu as pltpu

f32 = jnp.float32
bf16 = jnp.bfloat16

D_MODEL = 2048
A_HEADS = 8
A_KV_GROUPS = 2
A_HPG = A_HEADS // A_KV_GROUPS
DH = 128
A_WIDTH = A_HEADS * DH
CMP_BLOCK = 32
CMP_STRIDE = 16
SLC_BLOCK = 64
N_SELECT = 16
WINDOW = 512
B_HEADS = 8
B_WIDTH = B_HEADS * DH
REL_BUCKETS = 32
REL_MAX_DIST = 128
EPS = 1e-6
NEG = -1e30

VMEM_LIMIT = 56 * 1024 * 1024


def _sigmoid(x):
    return 1.0 / (1.0 + jnp.exp(-x))


def _head_norm(a, g):
    return a * lax.rsqrt(jnp.mean(a * a, axis=-1, keepdims=True) + EPS) * g


def _dot_small_int_lhs(m_bf, x, dims):
    hi = x.astype(bf16)
    r1 = x - hi.astype(f32)
    mid = r1.astype(bf16)
    lo = (r1 - mid.astype(f32)).astype(bf16)
    d = lambda p: lax.dot_general(m_bf, p, dims, preferred_element_type=f32)
    return d(hi) + d(mid) + d(lo)


PROJ_TN = 512
PROJ_TM = 512
PROJ_SEG = {0: "qa", 1: "qa", 2: "kvc", 3: "kvs", 4: "kvw", 5: "za", 6: "za", 7: "qb", 8: "qb", 9: "f", 10: "f",
            11: "ib", 12: "ib", 13: "zb", 14: "zb"}
PROJ_SLABS = ((0, 5), (5, 11), (11, 15))


def _proj_tiles(h, w_ref, first_tile, n_tiles, tm, out, par):
    for jj in range(n_tiles):
        j = first_tile + jj
        acc = jnp.dot(h, w_ref[:, jj * PROJ_TN:(jj + 1) * PROJ_TN], preferred_element_type=f32)
        name = PROJ_SEG[j]
        k = j - min(t for t, s in PROJ_SEG.items() if s == name)
        lanes = slice(k * PROJ_TN, (k + 1) * PROJ_TN)

        def cols(c):
            return acc[:, c * DH:(c + 1) * DH]

        if name == "qa":
            gq = par["gq"][...]
            out["qa"][:, lanes] = jnp.concatenate(
                [_head_norm(cols(c), gq) * (DH ** -0.5) for c in range(4)], axis=1).astype(out["qa"].dtype)
        elif name in ("kvc", "kvs", "kvw"):
            g_ref = {"kvc": None, "kvs": par["gks"], "kvw": par["gkw"]}[name]
            for c in range(4):
                val = _head_norm(cols(c), g_ref[...]) if (g_ref is not None and c < 2) else cols(c)
                out[name][pl.ds(c, tm, stride=4), :] = val
                if name == "kvw":
                    out["tail"][pl.ds(c, tm, stride=4), :] = val
        elif name in ("za", "zb"):
            out[name][:, lanes] = acc * _sigmoid(acc)
        elif name == "f":
            lb = par["lb"][:, lanes]
            out["f"][:, lanes] = lb + (1.0 - lb) * _sigmoid(acc)
        else:
            out[name][:, lanes] = acc


def _proj_first_kernel(x_ref, gn_ref, w_ref, wg_ref, gq_ref, gks_ref, gkw_ref,
                       h_ref, qa_ref, kvc_ref, kvs_ref, kvw_ref, tail_ref, gate_ref):
    x = x_ref[...]
    h = (x * lax.rsqrt(jnp.mean(x * x, axis=-1, keepdims=True) + EPS) * gn_ref[...]).astype(bf16)
    h_ref[...] = h
    gate_ref[...] = _sigmoid(jnp.dot(h, wg_ref[...], preferred_element_type=f32))
    first, last = PROJ_SLABS[0]
    _proj_tiles(h, w_ref, first, last - first, x_ref.shape[0],
                dict(qa=qa_ref, kvc=kvc_ref, kvs=kvs_ref, kvw=kvw_ref, tail=tail_ref),
                dict(gq=gq_ref, gks=gks_ref, gkw=gkw_ref))


def _proj_rest_kernel(h_ref, w_ref, lb_ref, *out_refs, slab):
    first, last = PROJ_SLABS[slab]
    names = list(dict.fromkeys(PROJ_SEG[j] for j in range(first, last)))
    _proj_tiles(h_ref[...], w_ref, first, last - first, h_ref.shape[0], dict(zip(names, out_refs)), dict(lb=lb_ref))


def _proj(x2d, gn, w_slabs, w_gate, gq, gks, gkw, lb, tm, t_seq):
    m = x2d.shape[0]
    assert m % tm == 0 and t_seq % tm == 0
    tiles_per_seq = t_seq // tm
    row = lambda i: (i, 0)
    const = lambda i: (0, 0)
    params = pltpu.CompilerParams(dimension_semantics=("arbitrary",), vmem_limit_bytes=VMEM_LIMIT)
    kv_shape = jax.ShapeDtypeStruct((m * 4, DH), f32)
    wide = lambda dt: jax.ShapeDtypeStruct((m, A_WIDTH), dt)
    h, qa, kvc, kvs, kvw, tail, gate = pl.pallas_call(
        _proj_first_kernel, grid=(m // tm,),
        in_specs=[pl.BlockSpec((tm, D_MODEL), row), pl.BlockSpec((1, D_MODEL), const),
                  pl.BlockSpec(w_slabs[0].shape, const), pl.BlockSpec(w_gate.shape, const),
                  pl.BlockSpec((1, DH), const), pl.BlockSpec((1, DH), const), pl.BlockSpec((1, DH), const)],
        out_specs=[pl.BlockSpec((tm, D_MODEL), row), pl.BlockSpec((tm, A_WIDTH), row)]
        + [pl.BlockSpec((tm * 4, DH), row)] * 3
        + [pl.BlockSpec((tm * 4, DH), lambda i: (i // tiles_per_seq, 0)), pl.BlockSpec((tm, A_KV_GROUPS * DH), row)],
        out_shape=[jax.ShapeDtypeStruct((m, D_MODEL), bf16), wide(bf16), kv_shape, kv_shape, kv_shape,
                   jax.ShapeDtypeStruct((m // tiles_per_seq * 4, DH), f32),
                   jax.ShapeDtypeStruct((m, A_KV_GROUPS * DH), f32)],
        compiler_params=params, name="proj_a",
    )(x2d, gn, w_slabs[0], w_gate, gq, gks, gkw)

    def rest(slab, n_out):
        return pl.pallas_call(
            functools.partial(_proj_rest_kernel, slab=slab), grid=(m // tm,),
            in_specs=[pl.BlockSpec((tm, D_MODEL), row), pl.BlockSpec(w_slabs[slab].shape, const),
                      pl.BlockSpec((1, B_WIDTH), const)],
            out_specs=[pl.BlockSpec((tm, A_WIDTH), row)] * n_out, out_shape=[wide(f32)] * n_out,
            compiler_params=params, name="proj_" + "abc"[slab],
        )(h, w_slabs[slab], lb)

    za, qb, f = rest(1, 3)
    ib, zb = rest(2, 2)
    return qa, kvc, kvs, kvw, gate, za, qb, f, ib, zb, tail


def _prep_proj_weights(w_in):
    a0 = A_WIDTH + 6 * A_KV_GROUPS * DH
    a1 = a0 + 3 * A_HEADS
    col0 = lambda tile: tile * PROJ_TN if tile * PROJ_TN < a0 else a1 + tile * PROJ_TN - a0
    w_slabs = tuple(w_in[:, col0(first):col0(first) + (last - first) * PROJ_TN].astype(bf16)
                    for first, last in PROJ_SLABS)
    wg = w_in[:, a0:a1].reshape(-1, 3, A_KV_GROUPS, A_HPG).transpose(0, 2, 1, 3).reshape(-1, A_KV_GROUPS, 3 * A_HPG)
    w_gate = jnp.pad(wg, ((0, 0), (0, 0), (0, DH - 3 * A_HPG))).reshape(-1, A_KV_GROUPS * DH).astype(bf16)
    return w_slabs, w_gate


HGRN_SAFE_LOG_DECAY = 80.0


def _hgrn_exact(q, k, v, b, st):
    sc = q.shape[0]
    t_idx = lax.broadcasted_iota(jnp.int32, (sc, DH), 0)
    o = lax.dot_general((q * jnp.exp(b)).astype(bf16), st.astype(bf16), (((1,), (1,)), ((), ())),
                        preferred_element_type=f32)
    for s in range(sc):
        e = jnp.exp(jnp.minimum(b - b[s:s + 1, :], 0.0))
        a = jnp.where(t_idx >= s, q * e * k[s:s + 1, :], 0.0)
        o = o + jnp.sum(a, axis=-1, keepdims=True) * v[s:s + 1, :]
    bl = b[sc - 1:sc, :]
    ut = lax.dot_general(v.astype(bf16), (k * jnp.exp(bl - b)).astype(bf16), (((0,), (0,)), ((), ())),
                         preferred_element_type=f32)
    return o, st * jnp.exp(bl) + ut


def _hgrn_factored(q, k, v, b, st):
    fc = q.shape[0]
    qd = (q * jnp.exp(b)).astype(bf16)
    att = lax.dot_general(qd, (k * jnp.exp(-b)).astype(bf16), (((1,), (1,)), ((), ())), preferred_element_type=f32)
    tri = lax.broadcasted_iota(jnp.int32, (fc, fc), 0) >= lax.broadcasted_iota(jnp.int32, (fc, fc), 1)
    att = jnp.where(tri, att, 0.0)
    o = jnp.dot(att.astype(bf16), v.astype(bf16), preferred_element_type=f32)
    o = o + lax.dot_general(qd, st.astype(bf16), (((1,), (1,)), ((), ())), preferred_element_type=f32)
    bl = b[fc - 1:fc, :]
    ut = lax.dot_general(v.astype(bf16), (k * jnp.exp(bl - b)).astype(bf16), (((0,), (0,)), ((), ())),
                         preferred_element_type=f32)
    return o, st * jnp.exp(bl) + ut


def _hgrn_kernel(q_ref, f_ref, v_ref, zs_ref, s0_ref, go_ref, tri_ref, o_ref, sout_ref, st_scr, *, sc, fc):
    c = pl.program_id(1)
    tc = q_ref.shape[0]
    ns = st_scr.shape[0]
    t_seq = tc // ns

    @pl.when(c == 0)
    def _():
        for j in range(ns):
            for h in range(B_HEADS):
                st_scr[j, h] = s0_ref[j, h].T

    f = f_ref[...]
    b_all = _dot_small_int_lhs(tri_ref[...], jnp.log(f), (((1,), (0,)), ((), ())))
    go = go_ref[...]

    def finish(rows, lanes, o):
        o_ref[rows, lanes] = (_head_norm(o, go) * zs_ref[rows, lanes]).astype(o_ref.dtype)

    def exact_rows(r0, n_rows, rebase):
        for h in range(B_HEADS):
            lanes = slice(h * DH, (h + 1) * DH)
            outs = []
            for i in range(n_rows // sc):
                rows = slice(r0 + i * sc, r0 + (i + 1) * sc)
                b = b_all[rows, lanes]
                if rebase and i > 0:
                    b = b - b_all[r0 + i * sc - 1:r0 + i * sc, lanes]
                j = (r0 + i * sc) // t_seq
                o, st = _hgrn_exact(q_ref[rows, lanes], 1.0 - f[rows, lanes], v_ref[rows, lanes], b, st_scr[j, h])
                st_scr[j, h] = st
                outs.append(o)
            finish(slice(r0, r0 + n_rows), lanes, jnp.concatenate(outs, axis=0) if len(outs) > 1 else outs[0])

    if fc is None:
        exact_rows(0, tc, False)
    else:
        for i in range(tc // fc):
            r0 = i * fc
            rows = slice(r0, r0 + fc)
            total = jnp.max(-b_all[r0 + fc - 1:r0 + fc, :])

            @pl.when(total < HGRN_SAFE_LOG_DECAY)
            def _():
                for h in range(B_HEADS):
                    lanes = slice(h * DH, (h + 1) * DH)
                    o, st = _hgrn_factored(q_ref[rows, lanes], 1.0 - f[rows, lanes], v_ref[rows, lanes],
                                           b_all[rows, lanes], st_scr[r0 // t_seq, h])
                    st_scr[r0 // t_seq, h] = st
                    finish(rows, lanes, o)

            @pl.when(jnp.logical_not(total < HGRN_SAFE_LOG_DECAY))
            def _():
                exact_rows(r0, fc, True)

    @pl.when(c == pl.num_programs(1) - 1)
    def _():
        for j in range(ns):
            for h in range(B_HEADS):
                sout_ref[j, h] = st_scr[j, h].T


def _block_tri(tc, blk):
    r = np.arange(tc)
    tri = (r[:, None] // blk == r[None, :] // blk) & (r[None, :] <= r[:, None])
    return jnp.asarray(tri.astype(np.float32), dtype=bf16)


def _hgrn(qb, f, ib, zs, s0, go, n, t, tc, sc, fc=None, ns=1):
    assert tc % sc == 0 and (fc is None or (tc % fc == 0 and fc % sc == 0))
    assert (ns == 1 and t % tc == 0) or (tc == ns * t and n % ns == 0 and fc is None and t % sc == 0)
    nc = max(t // tc, 1)
    rows = lambda i, c: (i * nc + c, 0)
    st = lambda i, c: (i, 0, 0, 0)
    const = lambda i, c: (0, 0)
    return pl.pallas_call(
        functools.partial(_hgrn_kernel, sc=sc, fc=fc),
        grid=(n // ns, nc),
        in_specs=[pl.BlockSpec((tc, B_WIDTH), rows)] * 4 + [
            pl.BlockSpec((ns, B_HEADS, DH, DH), st),
            pl.BlockSpec((1, DH), const),
            pl.BlockSpec((tc, tc), const),
        ],
        out_specs=[pl.BlockSpec((tc, B_WIDTH), rows), pl.BlockSpec((ns, B_HEADS, DH, DH), st)],
        out_shape=[jax.ShapeDtypeStruct((n * t, B_WIDTH), bf16), jax.ShapeDtypeStruct((n, B_HEADS, DH, DH), f32)],
        scratch_shapes=[pltpu.VMEM((ns, B_HEADS, DH, DH), f32)],
        compiler_params=pltpu.CompilerParams(
            dimension_semantics=("arbitrary", "arbitrary"), vmem_limit_bytes=VMEM_LIMIT),
        name="hgrn",
    )(qb, f, ib, zs, s0, go, _block_tri(tc, sc if fc is None else fc))


def _rel_bucket_np(dist):
    n = np.maximum(dist, 0)
    exact = REL_BUCKETS // 2
    scale = np.float32((REL_BUCKETS - exact) / math.log(REL_MAX_DIST / exact))
    large = exact + (np.log(np.maximum(n, exact).astype(np.float32) / np.float32(exact)) * scale).astype(np.int32)
    return np.where(n < exact, n, np.minimum(large, REL_BUCKETS - 1)).astype(np.int32)


def _bucket_starts():
    buckets = _rel_bucket_np(np.arange(2 * REL_MAX_DIST))
    assert np.all(np.diff(buckets) >= 0) and buckets[-1] == REL_BUCKETS - 1
    return [int(np.argmax(buckets >= b)) for b in range(REL_BUCKETS)]


def _bias_table_kernel(rb_ref, o_ref, *, off, a0, a1, sub_far):
    h = pl.program_id(0)
    shape = o_ref.shape[1:]
    d = off + a0 * lax.broadcasted_iota(jnp.int32, shape, 0) + a1 * lax.broadcasted_iota(jnp.int32, shape, 1)
    starts = _bucket_starts()
    far = rb_ref[REL_BUCKETS - 1, h]
    v = jnp.full(shape, far, f32)
    for b in range(REL_BUCKETS - 2, -1, -1):
        v = jnp.where(d < starts[b + 1], rb_ref[b, h], v)
    o_ref[0] = v - far if sub_far else v


def _bias_table(rel_bias, shape, off, a0, a1, sub_far):
    return pl.pallas_call(
        functools.partial(_bias_table_kernel, off=off, a0=a0, a1=a1, sub_far=sub_far),
        grid=(A_HEADS,),
        in_specs=[pl.BlockSpec(memory_space=pltpu.SMEM)],
        out_specs=pl.BlockSpec((1,) + tuple(shape), lambda h: (h, 0, 0)),
        out_shape=jax.ShapeDtypeStruct((A_HEADS,) + tuple(shape), f32),
        compiler_params=pltpu.CompilerParams(dimension_semantics=("arbitrary",)),
        name="bias_table",
    )(rel_bias.astype(f32))


def _cmp_bias_t(rel_bias, t0, nt, n_chunk):
    return _bias_table(rel_bias, (n_chunk, nt), t0 - (CMP_BLOCK - 1), -CMP_STRIDE, 1, False)


def _near_tiles_t(rel_bias):
    return _bias_table(rel_bias, (QB, 2 * QB), 0, -1, 1, True)


def _compress_combine(lhs, wcat, pe2):
    out = jnp.dot(lhs, wcat, preferred_element_type=f32)
    pc = jnp.dot(pe2, wcat, preferred_element_type=f32)
    const = pc[0:1, :DH] + pc[1:2, DH:]
    r = lhs.shape[0]
    return out[:, :DH] + pltpu.roll(out[:, DH:], r - 1, 0) + const


def _compress_kernel(kv_ref, wk_ref, wv_ref, pek_ref, pev_ref, gk_ref, kc_ref, vct_ref):
    nchunk = kv_ref.shape[0] // (4 * CMP_STRIDE)

    def chunk_rows(cg):
        return jnp.concatenate(
            [kv_ref[pl.ds(4 * l + cg, nchunk, stride=4 * CMP_STRIDE), :].astype(bf16) for l in range(CMP_STRIDE)],
            axis=1)

    kc = _compress_combine(jnp.concatenate([chunk_rows(0), chunk_rows(1)], axis=0), wk_ref[...], pek_ref[...])
    vc = _compress_combine(jnp.concatenate([chunk_rows(2), chunk_rows(3)], axis=0), wv_ref[...], pev_ref[...])
    kc = _head_norm(kc, gk_ref[...])
    for g in range(A_KV_GROUPS):
        kc_ref[0, g] = kc[g * nchunk:(g + 1) * nchunk].astype(kc_ref.dtype)
        vct_ref[0, g] = vc[g * nchunk:(g + 1) * nchunk].T.astype(vct_ref.dtype)


def _prep_cmp_weights(w, pe):
    half = CMP_STRIDE * DH
    wcat = jnp.concatenate([w[:half], w[half:]], axis=1).astype(bf16)
    return wcat, pe.reshape(2, half).astype(bf16)


def _compress(kvc, wk, wv, pek, pev, gk, n, t):
    nchunk = t // CMP_STRIDE
    assert nchunk == DH, "the transposed v_c block is square"
    const = lambda i: (0, 0)
    return pl.pallas_call(
        _compress_kernel, grid=(n,),
        in_specs=[pl.BlockSpec((t * 4, DH), lambda i: (i, 0)),
                  pl.BlockSpec(wk.shape, const), pl.BlockSpec(wv.shape, const),
                  pl.BlockSpec(pek.shape, const), pl.BlockSpec(pev.shape, const), pl.BlockSpec((1, DH), const)],
        out_specs=[pl.BlockSpec((1, A_KV_GROUPS, nchunk, DH), lambda i: (i, 0, 0, 0))] * 2,
        out_shape=[jax.ShapeDtypeStruct((n, A_KV_GROUPS, nchunk, DH), bf16)] * 2,
        compiler_params=pltpu.CompilerParams(dimension_semantics=("arbitrary",), vmem_limit_bytes=VMEM_LIMIT),
        name="compress",
    )(kvc, wk, wv, pek, pev, gk)


def _softmax_tile(carry, s, mask, v_bf):
    m, l, acc = carry
    s = jnp.where(mask, s, NEG)
    m_new = jnp.maximum(m, jnp.max(s, axis=-1, keepdims=True))
    alpha = jnp.exp(m - m_new)
    p = jnp.where(mask, jnp.exp(s - m_new), 0.0)
    l = alpha * l + jnp.sum(p, axis=-1, keepdims=True)
    h, r, k = p.shape
    pv = jnp.dot(p.reshape(h * r, k).astype(bf16), v_bf, preferred_element_type=f32).reshape(h, r, DH)
    return m_new, l, alpha * acc + pv


def _softmax_finish(carry):
    m, l, acc = carry
    return acc / jnp.where(l > 0, l, 1.0)


def _select_blocks(score_t, tpos, n_blocks):
    nb = score_t.shape[0]
    j = lax.broadcasted_iota(jnp.int32, score_t.shape, 0)
    valid = (j * SLC_BLOCK <= tpos) & (j < n_blocks)
    cur = tpos >> 6
    forced = (j == 0) | (j == cur) | (j == cur - 1)
    val = jnp.where(valid, jnp.where(forced, 1e30, score_t), -1.0)
    rank = jnp.zeros(score_t.shape, jnp.int32)
    for i in range(n_blocks):
        vi = val[i:i + 1, :]
        beats = (vi > val) | ((vi == val) & (i < j))
        rank = rank + beats.astype(jnp.int32)
    return jnp.where((rank < N_SELECT) & valid, 1.0, 0.0)


QB = 128


NEG_M = -1e30
NEG_S = -2e30


def _nsa_prompt_kernel(q_ref, kc_ref, vct_ref, bc_ref, kvs_ref, kvw_ref, gate_ref, za_ref, bt_ref, mt_ref, et_ref,
                       o_ref, ks_scr, vts_scr, kw_scr, vtw_scr, km_scr):
    qb = pl.program_id(1)
    q0 = qb * QB
    nkt = vts_scr.shape[1]
    nq4 = A_HPG * QB
    groups = range(A_KV_GROUPS)

    @pl.when(qb == 0)
    def _():
        for g in groups:
            for kt in range(nkt):
                rows = slice(kt * QB, (kt + 1) * QB)
                base = kt * 4 * QB
                ks_scr[g, rows, :] = kvs_ref[pl.ds(base + g, QB, stride=4), :].astype(bf16)
                vts_scr[g, kt] = kvs_ref[pl.ds(base + 2 + g, QB, stride=4), :].T.astype(bf16)
                kw_scr[g, rows, :] = kvw_ref[pl.ds(base + g, QB, stride=4), :].astype(bf16)
                vtw_scr[g, kt] = kvw_ref[pl.ds(base + 2 + g, QB, stride=4), :].T.astype(bf16)

    qall = q_ref[...]
    qts = [jnp.concatenate([qall[:, (g * A_HPG + h) * DH:(g * A_HPG + h + 1) * DH].astype(f32).T
                            for h in range(A_HPG)], axis=1).astype(bf16) for g in groups]
    kk = lax.broadcasted_iota(jnp.int32, (QB, nq4), 0)
    tt = lax.broadcasted_iota(jnp.int32, (QB, nq4), 1) & (QB - 1)

    def heads(ref, g, *idx):
        return jnp.concatenate([ref[(g * A_HPG + h,) + idx] for h in range(A_HPG)], axis=1)

    o_c = []
    for g in groups:
        sc = jnp.dot(kc_ref[0, g], qts[g], preferred_element_type=f32) + heads(bc_ref, g)
        mask_c = q0 + tt >= CMP_STRIDE * kk + (CMP_BLOCK - 1)
        sc = jnp.where(mask_c, sc, NEG)
        mc = jnp.max(sc, axis=0, keepdims=True)
        ec = jnp.where(mask_c, jnp.exp(sc - mc), 0.0)
        lc = jnp.sum(ec, axis=0, keepdims=True)
        pc = ec / jnp.where(lc > 0, lc, 1.0)
        o_c.append(jnp.dot(vct_ref[0, g], pc.astype(bf16), preferred_element_type=f32))
        ps = pc[:, 0:QB] + pc[:, QB:2 * QB] + pc[:, 2 * QB:3 * QB] + pc[:, 3 * QB:4 * QB]
        score_t = _dot_small_int_lhs(mt_ref[...], ps, (((1,), (0,)), ((), ())))
        nb = score_t.shape[0]
        sel_t = _select_blocks(score_t, q0 + lax.broadcasted_iota(jnp.int32, (nb, QB), 1), nb)
        km_scr[g] = jnp.dot(et_ref[...], sel_t.astype(bf16), preferred_element_type=f32)

    def tile(carry, qt, k_tile, vt_tile, mask, bias):
        m, l, acc = carry
        s = jnp.dot(k_tile, qt, preferred_element_type=f32)
        if bias is not None:
            s = s + bias
        s = jnp.where(mask, s, NEG_S)
        m_new = jnp.maximum(m, jnp.max(s, axis=0, keepdims=True))
        alpha = jnp.exp(m - m_new)
        p = jnp.exp(s - m_new)
        l = alpha * l + jnp.sum(p, axis=0, keepdims=True)
        return m_new, l, alpha * acc + jnp.dot(vt_tile, p.astype(bf16), preferred_element_type=f32)

    def init():
        return jnp.full((1, nq4), NEG_M, f32), jnp.zeros((1, nq4), f32), jnp.zeros((DH, nq4), f32)

    def finish(carry):
        m, l, acc = carry
        return acc / jnp.where(l > 0, l, 1.0)

    def key_rows(kt):
        return pl.ds(pl.multiple_of(kt * QB, QB), QB)

    def lanes4(x):
        return jnp.concatenate([x] * A_HPG, axis=1)

    def near_span(g, k_scr, vt_scr, rs, use_sel):
        ks, vts, masks, biases = [], [], [], []
        for r in rs:
            kt = jnp.maximum(qb - r, 0)
            ks.append(k_scr[g, key_rows(kt), :])
            vts.append(vt_scr[g, kt])
            mask = kk * 0 + (qb - r) >= 0
            if use_sel:
                mask = mask & (lanes4(km_scr[g, key_rows(kt), :]) > 0.5)
            if r == 0:
                mask = mask & (kk <= tt)
            if r == WINDOW // QB:
                mask = mask & (kk >= tt)
            masks.append(mask)
            biases.append(heads(bt_ref, g, slice(None), slice(r * QB, (r + 1) * QB)) if r < 2
                          else jnp.zeros((QB, nq4), f32))
        cat = lambda xs, axis: jnp.concatenate(xs, axis=axis)
        return cat(ks, 0), cat(vts, 1), cat(masks, 0), cat(biases, 0)

    o_w = [finish(tile(init(), qts[g], *near_span(g, kw_scr, vtw_scr, (4, 3, 2, 1, 0), False))) for g in groups]

    n_far = jnp.maximum(qb - 1, 0)
    kk2 = lax.broadcasted_iota(jnp.int32, (2 * QB, nq4), 0)

    def far(i, carries):
        rows = pl.ds(pl.multiple_of(i * 2 * QB, 2 * QB), 2 * QB)
        in_range = (kk2 + i * 2 * QB) < n_far * QB
        out = []
        for g in groups:
            mask = (lanes4(km_scr[g, rows, :]) > 0.5) & in_range
            vt = jnp.concatenate([vts_scr[g, 2 * i], vts_scr[g, jnp.minimum(2 * i + 1, nkt - 1)]], axis=1)
            out.append(tile(carries[g], qts[g], ks_scr[g, rows, :], vt, mask, None))
        return tuple(out)

    carries = lax.fori_loop(0, (n_far + 1) // 2, far, tuple(init() for g in groups))
    o_s = [finish(tile(carries[g], qts[g], *near_span(g, ks_scr, vts_scr, (1, 0), True))) for g in groups]

    za = za_ref[...]
    outs = []
    for g in groups:
        gate_t = gate_ref[:, g * DH:(g + 1) * DH].T
        for h in range(A_HPG):
            cols = slice(h * QB, (h + 1) * QB)

            def grow(br):
                return gate_t[br * A_HPG + h:br * A_HPG + h + 1, :]
            o = grow(0) * o_c[g][:, cols] + grow(1) * o_s[g][:, cols] + grow(2) * o_w[g][:, cols]
            outs.append(o.T * za[:, (g * A_HPG + h) * DH:(g * A_HPG + h + 1) * DH])
    o_ref[...] = jnp.concatenate(outs, axis=1).astype(o_ref.dtype)


def _cmp_to_slc_t(n_cmp_pad, n_slc_pad, n_cmp, n_slc):
    c0 = np.arange(n_cmp_pad)[None, :] * CMP_STRIDE
    s0 = np.arange(n_slc_pad)[:, None] * SLC_BLOCK
    ov = np.minimum(c0 + CMP_BLOCK, s0 + SLC_BLOCK) - np.maximum(c0, s0)
    m = np.maximum(ov, 0).astype(np.float32) / CMP_STRIDE
    m[:, n_cmp:] = 0
    m[n_slc:, :] = 0
    return m


def _nsa_prompt(qa, kc, vct, kvs, kvw, gate, za, rel_bias, n, t):
    nq = t // QB
    n_cmp = (t - CMP_BLOCK) // CMP_STRIDE + 1
    n_slc = -(-t // SLC_BLOCK)
    assert kc.shape[2] == QB and n_slc % 8 == 0
    bias_c = _cmp_bias_t(rel_bias, 0, t, QB)
    bt = _near_tiles_t(rel_bias)
    mt = jnp.asarray(_cmp_to_slc_t(QB, n_slc, n_cmp, n_slc), dtype=bf16)
    et = jnp.asarray((np.arange(t)[:, None] // SLC_BLOCK == np.arange(n_slc)[None, :]).astype(np.float32), dtype=bf16)
    qrow = lambda i, b: (i * nq + b, 0)
    seq = lambda i, b: (i, 0, 0, 0)
    ng = A_KV_GROUPS
    return pl.pallas_call(
        _nsa_prompt_kernel, grid=(n, nq),
        in_specs=[
            pl.BlockSpec((QB, A_WIDTH), qrow),
            pl.BlockSpec((1, ng, QB, DH), seq),
            pl.BlockSpec((1, ng, DH, QB), seq),
            pl.BlockSpec((A_HEADS, QB, QB), lambda i, b: (0, 0, b)),
            pl.BlockSpec((t * 4, DH), lambda i, b: (i, 0)),
            pl.BlockSpec((t * 4, DH), lambda i, b: (i, 0)),
            pl.BlockSpec((QB, ng * DH), qrow),
            pl.BlockSpec((QB, A_WIDTH), qrow),
            pl.BlockSpec((A_HEADS, QB, 2 * QB), lambda i, b: (0, 0, 0)),
            pl.BlockSpec(mt.shape, lambda i, b: (0, 0)),
            pl.BlockSpec(et.shape, lambda i, b: (0, 0)),
        ],
        out_specs=pl.BlockSpec((QB, A_WIDTH), qrow),
        out_shape=jax.ShapeDtypeStruct((n * t, A_WIDTH), bf16),
        scratch_shapes=[pltpu.VMEM((ng, t, DH), bf16), pltpu.VMEM((ng, nq, DH, QB), bf16),
                        pltpu.VMEM((ng, t, DH), bf16), pltpu.VMEM((ng, nq, DH, QB), bf16),
                        pltpu.VMEM((ng, t, QB), f32)],
        compiler_params=pltpu.CompilerParams(
            dimension_semantics=("arbitrary", "arbitrary"), vmem_limit_bytes=VMEM_LIMIT),
        name="nsa_prompt",
    )(qa, kc, vct, bias_c, kvs, kvw, gate, za, bt, mt, et)


CMP_PITCH = 24


def _nsa_sample_kernel(pt_ref, *refs, n_pages, page, t_new, n_blocks):
    del pt_ref
    cmp_pages = refs[:n_pages]
    slc_pages = refs[n_pages:2 * n_pages]
    (q_ref, kvs_new_ref, kvw_new_ref, win_ref, wk_ref, wv_ref, pek_ref, pev_ref, gk_ref, bc_ref, bl_ref, bn_ref,
     gate_ref, za_ref, mt_ref, e_ref, o_ref, win_out_ref, xc_scr) = refs[2 * n_pages:]
    past = n_pages * page
    nchunk = past // CMP_STRIDE
    pad_new = QB
    ng = A_KV_GROUPS
    groups = range(ng)

    cpp = page // CMP_STRIDE
    for p in range(n_pages):
        for cg in range(4):
            x = cmp_pages[p][0, pl.ds(cg, page, stride=4), :]
            for c in range(cpp):
                r0 = (p * cpp + c) * CMP_PITCH
                xc_scr[cg, r0:r0 + CMP_STRIDE, :] = x[c * CMP_STRIDE:(c + 1) * CMP_STRIDE]

    def chunk_rows(cg):
        return jnp.concatenate(
            [xc_scr[cg, pl.ds(l, nchunk, stride=CMP_PITCH), :].astype(bf16) for l in range(CMP_STRIDE)], axis=1)

    kc = _compress_combine(jnp.concatenate([chunk_rows(0), chunk_rows(1)], axis=0), wk_ref[...], pek_ref[...])
    vc = _compress_combine(jnp.concatenate([chunk_rows(2), chunk_rows(3)], axis=0), wv_ref[...], pev_ref[...])
    kc = _head_norm(kc, gk_ref[...]).astype(bf16)
    vc = vc.astype(bf16)

    qall = q_ref[...]
    qs = [jnp.concatenate([qall[:, (g * A_HPG + h) * DH:(g * A_HPG + h + 1) * DH] for h in range(A_HPG)], axis=0)
          for g in groups]
    zeros_pad = jnp.zeros((pad_new - t_new, DH), bf16)

    def new_rows(ref, c):
        return jnp.concatenate([ref[pl.ds(c, t_new, stride=4), :].astype(bf16), zeros_pad], axis=0)

    def logits(ks):
        return jnp.stack([lax.dot_general(qs[g], ks[g], (((1,), (1,)), ((), ())), preferred_element_type=f32)
                          .reshape(A_HPG, t_new, ks[g].shape[0]) for g in groups], axis=0)

    def softmax_pv(s, mask, vs):
        s = jnp.where(mask, s, NEG)
        e = jnp.where(mask, jnp.exp(s - jnp.max(s, axis=-1, keepdims=True)), 0.0)
        l = jnp.sum(e, axis=-1, keepdims=True)
        p = e / jnp.where(l > 0, l, 1.0)
        o = jnp.stack([jnp.dot(p[g].reshape(A_HPG * t_new, -1).astype(bf16), vs[g], preferred_element_type=f32)
                       .reshape(A_HPG, t_new, DH) for g in groups], axis=0)
        return o, p

    def tail_bias(s):
        bias = jnp.concatenate([bl_ref[...], bn_ref[...]], axis=-1).reshape(ng, A_HPG, t_new, 2 * QB)
        nk = s.shape[-1]
        return jnp.concatenate([s[..., :nk - 2 * QB], s[..., nk - 2 * QB:] + bias], axis=-1)

    ti = lax.broadcasted_iota(jnp.int32, (t_new, nchunk), 0)
    ci = lax.broadcasted_iota(jnp.int32, (t_new, nchunk), 1)
    cmask = (past + ti >= CMP_STRIDE * ci + (CMP_BLOCK - 1))[None, None]
    sc = logits([kc[g * nchunk:(g + 1) * nchunk] for g in groups]) + bc_ref[...].reshape(ng, A_HPG, t_new, nchunk)
    o_c, pc = softmax_pv(sc, cmask, [vc[g * nchunk:(g + 1) * nchunk] for g in groups])

    ps = jnp.concatenate([pc[g, 0] + pc[g, 1] + pc[g, 2] + pc[g, 3] for g in groups], axis=0)
    score_t = _dot_small_int_lhs(mt_ref[...], ps, (((1,), (1,)), ((), ())))
    lane = lax.broadcasted_iota(jnp.int32, score_t.shape, 1)
    tpos = past + jnp.where(lane >= t_new, lane - t_new, lane)
    sel_t = _select_blocks(score_t, tpos, n_blocks)
    key_mask = lax.dot_general(sel_t.astype(bf16), e_ref[...], (((0,), (0,)), ((), ())),
                               preferred_element_type=f32)

    tt = lax.broadcasted_iota(jnp.int32, (t_new, pad_new), 0)
    uu = lax.broadcasted_iota(jnp.int32, (t_new, pad_new), 1)
    new_mask = uu <= tt
    new_mask_f = jnp.where(new_mask, 1.0, 0.0)
    sel_mask = jnp.concatenate([key_mask, jnp.concatenate([new_mask_f] * ng, axis=0)], axis=1)
    sel_mask = sel_mask.reshape(ng, 1, t_new, past + pad_new) > 0.5
    ks = [jnp.concatenate([slc_pages[p][0, pl.ds(g, page, stride=4), :].astype(bf16) for p in range(n_pages)]
                          + [new_rows(kvs_new_ref, g)], axis=0) for g in groups]
    vs = [jnp.concatenate([slc_pages[p][0, pl.ds(2 + g, page, stride=4), :].astype(bf16) for p in range(n_pages)]
                          + [new_rows(kvs_new_ref, 2 + g)], axis=0) for g in groups]
    o_s, _ = softmax_pv(tail_bias(logits(ks)), sel_mask, vs)

    nw = win_ref.shape[1] // 4
    kw = [jnp.concatenate([win_ref[0, pl.ds(g, nw, stride=4), :].astype(bf16), new_rows(kvw_new_ref, g)], axis=0)
          for g in groups]
    vw = [jnp.concatenate([win_ref[0, pl.ds(2 + g, nw, stride=4), :].astype(bf16), new_rows(kvw_new_ref, 2 + g)], axis=0)
          for g in groups]
    wmask = lax.broadcasted_iota(jnp.int32, (t_new, nw), 1) >= lax.broadcasted_iota(jnp.int32, (t_new, nw), 0)
    o_w, _ = softmax_pv(tail_bias(logits(kw)), jnp.concatenate([wmask, new_mask], axis=1)[None, None], vw)

    gate_all = gate_ref[...]
    za = za_ref[...]
    outs = []
    for g in groups:
        gate = gate_all[:, g * DH:(g + 1) * DH]
        for h in range(A_HPG):
            def gcol(br):
                return gate[:, br * A_HPG + h:br * A_HPG + h + 1]
            o = gcol(0) * o_c[g, h] + gcol(1) * o_s[g, h] + gcol(2) * o_w[g, h]
            outs.append(o * za[:, (g * A_HPG + h) * DH:(g * A_HPG + h + 1) * DH])
    o_ref[...] = jnp.concatenate(outs, axis=1).astype(o_ref.dtype)

    nrow = win_ref.shape[1]
    win_out_ref[0, 0:nrow - 4 * t_new, :] = win_ref[0, 4 * t_new:nrow, :]
    win_out_ref[0, nrow - 4 * t_new:nrow, :] = kvw_new_ref[...]


def _nsa_sample(qa, kvs_new, kvw_new, cache_cmp, cache_slc, cache_win, page_table, wk, wv, pek, pev, gk,
                gate, za, rel_bias, t_new):
    n, n_pages = page_table.shape
    page = cache_cmp.shape[1] // 4
    past = n_pages * page
    wb = cache_win.shape[1] // 4
    assert wb == WINDOW and past % QB == 0 and past >= WINDOW
    nchunk = past // CMP_STRIDE
    n_cmp = (past + t_new - CMP_BLOCK) // CMP_STRIDE + 1
    assert n_cmp < nchunk
    n_slc = -(-(past + t_new) // SLC_BLOCK)
    nb_pad = -(-n_slc // 8) * 8
    bias_c = _bias_table(rel_bias, (t_new, nchunk), past - (CMP_BLOCK - 1), 1, -CMP_STRIDE, False)
    b_last = _bias_table(rel_bias, (t_new, QB), QB, 1, -1, True)
    b_new = _bias_table(rel_bias, (t_new, QB), 0, 1, -1, True)
    mt = jnp.asarray(_cmp_to_slc_t(nchunk, nb_pad, n_cmp, n_slc), dtype=bf16)
    e = jnp.asarray((np.arange(past)[None, :] // SLC_BLOCK == np.arange(nb_pad)[:, None]).astype(np.float32), dtype=bf16)

    def page_spec(p):
        return pl.BlockSpec((1, page * 4, DH), lambda i, pt: (pt[i, p], 0, 0))

    rowblk = lambda w: pl.BlockSpec((t_new, w), lambda i, pt: (i, 0))
    full = lambda a: pl.BlockSpec(a.shape, lambda i, pt: (0,) * a.ndim)
    in_specs = ([page_spec(p) for p in range(n_pages)] * 2 + [
        rowblk(A_WIDTH),
        pl.BlockSpec((t_new * 4, DH), lambda i, pt: (i, 0)),
        pl.BlockSpec((t_new * 4, DH), lambda i, pt: (i, 0)),
        pl.BlockSpec((1, wb * 4, DH), lambda i, pt: (i, 0, 0)),
        full(wk), full(wv), full(pek), full(pev), full(gk), full(bias_c), full(b_last), full(b_new),
        rowblk(A_KV_GROUPS * DH), rowblk(A_WIDTH), full(mt), full(e)])
    grid_spec = pltpu.PrefetchScalarGridSpec(
        num_scalar_prefetch=1, grid=(n,), in_specs=in_specs,
        out_specs=[rowblk(A_WIDTH), pl.BlockSpec((1, wb * 4, DH), lambda i, pt: (i, 0, 0))],
        scratch_shapes=[pltpu.VMEM((4, nchunk * CMP_PITCH, DH), f32)])
    return pl.pallas_call(
        functools.partial(_nsa_sample_kernel, n_pages=n_pages, page=page, t_new=t_new, n_blocks=n_slc),
        grid_spec=grid_spec,
        out_shape=[jax.ShapeDtypeStruct((n * t_new, A_WIDTH), f32), jax.ShapeDtypeStruct(cache_win.shape, f32)],
        compiler_params=pltpu.CompilerParams(dimension_semantics=("arbitrary",), vmem_limit_bytes=VMEM_LIMIT),
        name="nsa_sample",
    )(page_table, *([cache_cmp] * n_pages), *([cache_slc] * n_pages), qa, kvs_new, kvw_new, cache_win,
      wk, wv, pek, pev, gk, bias_c, b_last, b_new, gate, za, mt, e)


def _outproj_kernel(x_ref, oa_ref, ob_ref, w_ref, y_ref):
    acc = jnp.dot(oa_ref[...].astype(bf16), w_ref[0:A_WIDTH, :], preferred_element_type=f32)
    acc = acc + jnp.dot(ob_ref[...].astype(bf16), w_ref[A_WIDTH:, :], preferred_element_type=f32)
    y_ref[...] = x_ref[...] + acc


def _outproj(x2d, oa, ob, w_out_bf, tm):
    m = x2d.shape[0]
    assert m % tm == 0
    row = lambda i: (i, 0)
    return pl.pallas_call(
        _outproj_kernel, grid=(m // tm,),
        in_specs=[pl.BlockSpec((tm, D_MODEL), row), pl.BlockSpec((tm, A_WIDTH), row), pl.BlockSpec((tm, B_WIDTH), row),
                  pl.BlockSpec(w_out_bf.shape, lambda i: (0, 0))],
        out_specs=pl.BlockSpec((tm, D_MODEL), row),
        out_shape=jax.ShapeDtypeStruct((m, D_MODEL), f32),
        compiler_params=pltpu.CompilerParams(dimension_semantics=("arbitrary",), vmem_limit_bytes=VMEM_LIMIT),
        name="outproj",
    )(x2d, oa, ob, w_out_bf)


def kernel(x_prompt, x_sample, cache_kv_cmp, cache_kv_slc, cache_kv_win, state_hgrn, page_table, g_norm, w_in, w_out,
           g_q, g_k_slc, g_k_win, g_k_cmp, w_cmp_k, w_cmp_v, pe_cmp_k, pe_cmp_v, rel_bias, lb_logits, g_o_hgrn):
    depth = w_in.shape[0]
    assert depth == 1, "single-layer trunk"
    nb, t, _ = x_prompt.shape
    ns, ts, _ = x_sample.shape
    row = lambda a: a.astype(f32)[None]
    lower = jnp.cumsum(jax.nn.softmax(lb_logits.astype(f32), axis=0), axis=0)[0]
    w_main, w_gate = _prep_proj_weights(w_in[0])
    w_out_bf = w_out[0].astype(bf16)
    wk, pek = _prep_cmp_weights(w_cmp_k[0], pe_cmp_k[0])
    wv, pev = _prep_cmp_weights(w_cmp_v[0], pe_cmp_v[0])
    proj_args = (row(g_norm[0]), w_main, w_gate, row(g_q[0]), row(g_k_slc[0]), row(g_k_win[0]), row(lower))
    kv6 = lambda a, n_, t_: a.reshape(1, n_, t_, 2, A_KV_GROUPS, DH)

    xp = x_prompt.reshape(nb * t, D_MODEL)
    wlen = min(WINDOW, t)
    assert wlen == PROJ_TM, "the proj row tile doubles as the prompt's final window"
    qa, kvc, kvs, kvw, gate, za, qb, f, ib, zb, win_p = _proj(xp, *proj_args, tm=PROJ_TM, t_seq=t)
    s0 = jnp.zeros((nb, B_HEADS, DH, DH), f32)
    ob, st_p = _hgrn(qb, f, ib, zb, s0, row(g_o_hgrn[0]), nb, t, tc=128, sc=16, fc=64)
    kc, vct = _compress(kvc, wk, wv, pek, pev, row(g_k_cmp[0]), nb, t)
    oa = _nsa_prompt(qa, kc, vct, kvs, kvw, gate, za, rel_bias, nb, t)
    y_p = _outproj(xp, oa, ob, w_out_bf, tm=PROJ_TM).reshape(nb, t, D_MODEL)
    win_p = kv6(win_p, nb, wlen)

    xs = x_sample.reshape(ns * ts, D_MODEL)
    qa, kvc_s, kvs_s, kvw_s, gate, za, qb, f, ib, zb, _ = _proj(xs, *proj_args, tm=PROJ_TM, t_seq=PROJ_TM)
    seq_per_step = 8
    ob, st_s = _hgrn(qb, f, ib, zb, state_hgrn[0].astype(f32), row(g_o_hgrn[0]), ns, ts, tc=seq_per_step * ts, sc=ts,
                     ns=seq_per_step)
    pool = cache_kv_cmp.shape[1]
    page = cache_kv_cmp.shape[2]
    oa, win_s = _nsa_sample(
        qa, kvs_s, kvw_s, cache_kv_cmp[0].reshape(pool, page * 4, DH), cache_kv_slc[0].reshape(pool, page * 4, DH),
        cache_kv_win[0].reshape(ns, -1, DH), page_table, wk, wv, pek, pev, row(g_k_cmp[0]), gate, za, rel_bias, ts)
    y_s = _outproj(xs, oa, ob, w_out_bf, tm=PROJ_TM).reshape(ns, ts, D_MODEL)

    return (y_p, y_s, kv6(kvc, nb, t), kv6(kvs, nb, t), win_p, st_p[None].astype(x_prompt.dtype),
            kv6(kvc_s, ns, ts), kv6(kvs_s, ns, ts), kv6(win_s, ns, WINDOW), st_s[None].astype(state_hgrn.dtype))
```

```python
import functools
import math

import jax
import jax.numpy as jnp
import numpy as np
from jax import lax
from jax.experimental import pallas as pl
from jax.experimental.pallas import tpu as pltpu

f32 = jnp.float32
bf16 = jnp.bfloat16

D_MODEL = 2048
A_HEADS = 8
A_KV_GROUPS = 2
A_HPG = A_HEADS // A_KV_GROUPS
DH = 128
A_WIDTH = A_HEADS * DH
CMP_BLOCK = 32
CMP_STRIDE = 16
SLC_BLOCK = 64
N_SELECT = 16
WINDOW = 512
B_HEADS = 8
B_WIDTH = B_HEADS * DH
REL_BUCKETS = 32
REL_MAX_DIST = 128
EPS = 1e-6
NEG = -1e30

VMEM_LIMIT = 56 * 1024 * 1024


def _sigmoid(x):
    return 1.0 / (1.0 + jnp.exp(-x))


def _head_norm(a, g):
    return a * lax.rsqrt(jnp.mean(a * a, axis=-1, keepdims=True) + EPS) * g


def _dot_small_int_lhs(m_bf, x, dims):
    hi = x.astype(bf16)
    r1 = x - hi.astype(f32)
    mid = r1.astype(bf16)
    lo = (r1 - mid.astype(f32)).astype(bf16)
    d = lambda p: lax.dot_general(m_bf, p, dims, preferred_element_type=f32)
    return d(hi) + d(mid) + d(lo)


PROJ_TN = 512
PROJ_TM = 512
PROJ_SEG = {0: "qa", 1: "qa", 2: "kvc", 3: "kvs", 4: "kvw", 5: "za", 6: "za", 7: "qb", 8: "qb", 9: "f", 10: "f",
            11: "ib", 12: "ib", 13: "zb", 14: "zb"}
PROJ_SLABS = ((0, 5), (5, 11), (11, 15))


def _proj_tiles(h, w_ref, first_tile, n_tiles, tm, out, par):
    for jj in range(n_tiles):
        j = first_tile + jj
        acc = jnp.dot(h, w_ref[:, jj * PROJ_TN:(jj + 1) * PROJ_TN], preferred_element_type=f32)
        name = PROJ_SEG[j]
        k = j - min(t for t, s in PROJ_SEG.items() if s == name)
        lanes = slice(k * PROJ_TN, (k + 1) * PROJ_TN)

        def cols(c):
            return acc[:, c * DH:(c + 1) * DH]

        if name == "qa":
            gq = par["gq"][...]
            out["qa"][:, lanes] = jnp.concatenate(
                [_head_norm(cols(c), gq) * (DH ** -0.5) for c in range(4)], axis=1).astype(out["qa"].dtype)
        elif name in ("kvc", "kvs", "kvw"):
            g_ref = {"kvc": None, "kvs": par["gks"], "kvw": par["gkw"]}[name]
            for c in range(4):
                val = _head_norm(cols(c), g_ref[...]) if (g_ref is not None and c < 2) else cols(c)
                out[name][pl.ds(c, tm, stride=4), :] = val
                if name == "kvw":
                    out["tail"][pl.ds(c, tm, stride=4), :] = val
        elif name in ("za", "zb"):
            out[name][:, lanes] = acc * _sigmoid(acc)
        elif name == "f":
            lb = par["lb"][:, lanes]
            out["f"][:, lanes] = lb + (1.0 - lb) * _sigmoid(acc)
        else:
            out[name][:, lanes] = acc


def _proj_first_kernel(x_ref, gn_ref, w_ref, wg_ref, gq_ref, gks_ref, gkw_ref,
                       h_ref, qa_ref, kvc_ref, kvs_ref, kvw_ref, tail_ref, gate_ref):
    x = x_ref[...]
    h = (x * lax.rsqrt(jnp.mean(x * x, axis=-1, keepdims=True) + EPS) * gn_ref[...]).astype(bf16)
    h_ref[...] = h
    gate_ref[...] = _sigmoid(jnp.dot(h, wg_ref[...], preferred_element_type=f32))
    first, last = PROJ_SLABS[0]
    _proj_tiles(h, w_ref, first, last - first, x_ref.shape[0],
                dict(qa=qa_ref, kvc=kvc_ref, kvs=kvs_ref, kvw=kvw_ref, tail=tail_ref),
                dict(gq=gq_ref, gks=gks_ref, gkw=gkw_ref))


def _proj_rest_kernel(h_ref, w_ref, lb_ref, *out_refs, slab):
    first, last = PROJ_SLABS[slab]
    names = list(dict.fromkeys(PROJ_SEG[j] for j in range(first, last)))
    _proj_tiles(h_ref[...], w_ref, first, last - first, h_ref.shape[0], dict(zip(names, out_refs)), dict(lb=lb_ref))


def _proj(x2d, gn, w_slabs, w_gate, gq, gks, gkw, lb, tm, t_seq):
    m = x2d.shape[0]
    assert m % tm == 0 and t_seq % tm == 0
    tiles_per_seq = t_seq // tm
    row = lambda i: (i, 0)
    const = lambda i: (0, 0)
    params = pltpu.CompilerParams(dimension_semantics=("arbitrary",), vmem_limit_bytes=VMEM_LIMIT)
    kv_shape = jax.ShapeDtypeStruct((m * 4, DH), f32)
    wide = lambda dt: jax.ShapeDtypeStruct((m, A_WIDTH), dt)
    h, qa, kvc, kvs, kvw, tail, gate = pl.pallas_call(
        _proj_first_kernel, grid=(m // tm,),
        in_specs=[pl.BlockSpec((tm, D_MODEL), row), pl.BlockSpec((1, D_MODEL), const),
                  pl.BlockSpec(w_slabs[0].shape, const), pl.BlockSpec(w_gate.shape, const),
                  pl.BlockSpec((1, DH), const), pl.BlockSpec((1, DH), const), pl.BlockSpec((1, DH), const)],
        out_specs=[pl.BlockSpec((tm, D_MODEL), row), pl.BlockSpec((tm, A_WIDTH), row)]
        + [pl.BlockSpec((tm * 4, DH), row)] * 3
        + [pl.BlockSpec((tm * 4, DH), lambda i: (i // tiles_per_seq, 0)), pl.BlockSpec((tm, A_KV_GROUPS * DH), row)],
        out_shape=[jax.ShapeDtypeStruct((m, D_MODEL), bf16), wide(bf16), kv_shape, kv_shape, kv_shape,
                   jax.ShapeDtypeStruct((m // tiles_per_seq * 4, DH), f32),
                   jax.ShapeDtypeStruct((m, A_KV_GROUPS * DH), f32)],
        compiler_params=params, name="proj_a",
    )(x2d, gn, w_slabs[0], w_gate, gq, gks, gkw)

    def rest(slab, n_out):
        return pl.pallas_call(
            functools.partial(_proj_rest_kernel, slab=slab), grid=(m // tm,),
            in_specs=[pl.BlockSpec((tm, D_MODEL), row), pl.BlockSpec(w_slabs[slab].shape, const),
                      pl.BlockSpec((1, B_WIDTH), const)],
            out_specs=[pl.BlockSpec((tm, A_WIDTH), row)] * n_out, out_shape=[wide(f32)] * n_out,
            compiler_params=params, name="proj_" + "abc"[slab],
        )(h, w_slabs[slab], lb)

    za, qb, f = rest(1, 3)
    ib, zb = rest(2, 2)
    return qa, kvc, kvs, kvw, gate, za, qb, f, ib, zb, tail


def _prep_proj_weights(w_in):
    a0 = A_WIDTH + 6 * A_KV_GROUPS * DH
    a1 = a0 + 3 * A_HEADS
    col0 = lambda tile: tile * PROJ_TN if tile * PROJ_TN < a0 else a1 + tile * PROJ_TN - a0
    w_slabs = tuple(w_in[:, col0(first):col0(first) + (last - first) * PROJ_TN].astype(bf16)
                    for first, last in PROJ_SLABS)
    wg = w_in[:, a0:a1].reshape(-1, 3, A_KV_GROUPS, A_HPG).transpose(0, 2, 1, 3).reshape(-1, A_KV_GROUPS, 3 * A_HPG)
    w_gate = jnp.pad(wg, ((0, 0), (0, 0), (0, DH - 3 * A_HPG))).reshape(-1, A_KV_GROUPS * DH).astype(bf16)
    return w_slabs, w_gate


HGRN_SAFE_LOG_DECAY = 80.0


def _hgrn_exact(q, k, v, b, st):
    sc = q.shape[0]
    t_idx = lax.broadcasted_iota(jnp.int32, (sc, DH), 0)
    o = lax.dot_general((q * jnp.exp(b)).astype(bf16), st.astype(bf16), (((1,), (1,)), ((), ())),
                        preferred_element_type=f32)
    for s in range(sc):
        e = jnp.exp(jnp.minimum(b - b[s:s + 1, :], 0.0))
        a = jnp.where(t_idx >= s, q * e * k[s:s + 1, :], 0.0)
        o = o + jnp.sum(a, axis=-1, keepdims=True) * v[s:s + 1, :]
    bl = b[sc - 1:sc, :]
    ut = lax.dot_general(v.astype(bf16), (k * jnp.exp(bl - b)).astype(bf16), (((0,), (0,)), ((), ())),
                         preferred_element_type=f32)
    return o, st * jnp.exp(bl) + ut


def _hgrn_factored(q, k, v, b, st):
    fc = q.shape[0]
    qd = (q * jnp.exp(b)).astype(bf16)
    att = lax.dot_general(qd, (k * jnp.exp(-b)).astype(bf16), (((1,), (1,)), ((), ())), preferred_element_type=f32)
    tri = lax.broadcasted_iota(jnp.int32, (fc, fc), 0) >= lax.broadcasted_iota(jnp.int32, (fc, fc), 1)
    att = jnp.where(tri, att, 0.0)
    o = jnp.dot(att.astype(bf16), v.astype(bf16), preferred_element_type=f32)
    o = o + lax.dot_general(qd, st.astype(bf16), (((1,), (1,)), ((), ())), preferred_element_type=f32)
    bl = b[fc - 1:fc, :]
    ut = lax.dot_general(v.astype(bf16), (k * jnp.exp(bl - b)).astype(bf16), (((0,), (0,)), ((), ())),
                         preferred_element_type=f32)
    return o, st * jnp.exp(bl) + ut


def _hgrn_kernel(q_ref, f_ref, v_ref, zs_ref, s0_ref, go_ref, tri_ref, o_ref, sout_ref, st_scr, *, sc, fc):
    c = pl.program_id(1)
    tc = q_ref.shape[0]
    ns = st_scr.shape[0]
    t_seq = tc // ns

    @pl.when(c == 0)
    def _():
        for j in range(ns):
            for h in range(B_HEADS):
                st_scr[j, h] = s0_ref[j, h].T

    f = f_ref[...]
    b_all = _dot_small_int_lhs(tri_ref[...], jnp.log(f), (((1,), (0,)), ((), ())))
    go = go_ref[...]

    def finish(rows, lanes, o):
        o_ref[rows, lanes] = (_head_norm(o, go) * zs_ref[rows, lanes]).astype(o_ref.dtype)

    def exact_rows(r0, n_rows, rebase):
        for h in range(B_HEADS):
            lanes = slice(h * DH, (h + 1) * DH)
            outs = []
            for i in range(n_rows // sc):
                rows = slice(r0 + i * sc, r0 + (i + 1) * sc)
                b = b_all[rows, lanes]
                if rebase and i > 0:
                    b = b - b_all[r0 + i * sc - 1:r0 + i * sc, lanes]
                j = (r0 + i * sc) // t_seq
                o, st = _hgrn_exact(q_ref[rows, lanes], 1.0 - f[rows, lanes], v_ref[rows, lanes], b, st_scr[j, h])
                st_scr[j, h] = st
                outs.append(o)
            finish(slice(r0, r0 + n_rows), lanes, jnp.concatenate(outs, axis=0) if len(outs) > 1 else outs[0])

    if fc is None:
        exact_rows(0, tc, False)
    else:
        for i in range(tc // fc):
            r0 = i * fc
            rows = slice(r0, r0 + fc)
            total = jnp.max(-b_all[r0 + fc - 1:r0 + fc, :])

            @pl.when(total < HGRN_SAFE_LOG_DECAY)
            def _():
                for h in range(B_HEADS):
                    lanes = slice(h * DH, (h + 1) * DH)
                    o, st = _hgrn_factored(q_ref[rows, lanes], 1.0 - f[rows, lanes], v_ref[rows, lanes],
                                           b_all[rows, lanes], st_scr[r0 // t_seq, h])
                    st_scr[r0 // t_seq, h] = st
                    finish(rows, lanes, o)

            @pl.when(jnp.logical_not(total < HGRN_SAFE_LOG_DECAY))
            def _():
                exact_rows(r0, fc, True)

    @pl.when(c == pl.num_programs(1) - 1)
    def _():
        for j in range(ns):
            for h in range(B_HEADS):
                sout_ref[j, h] = st_scr[j, h].T


def _block_tri(tc, blk):
    r = np.arange(tc)
    tri = (r[:, None] // blk == r[None, :] // blk) & (r[None, :] <= r[:, None])
    return jnp.asarray(tri.astype(np.float32), dtype=bf16)


def _hgrn(qb, f, ib, zs, s0, go, n, t, tc, sc, fc=None, ns=1):
    assert tc % sc == 0 and (fc is None or (tc % fc == 0 and fc % sc == 0))
    assert (ns == 1 and t % tc == 0) or (tc == ns * t and n % ns == 0 and fc is None and t % sc == 0)
    nc = max(t // tc, 1)
    rows = lambda i, c: (i * nc + c, 0)
    st = lambda i, c: (i, 0, 0, 0)
    const = lambda i, c: (0, 0)
    return pl.pallas_call(
        functools.partial(_hgrn_kernel, sc=sc, fc=fc),
        grid=(n // ns, nc),
        in_specs=[pl.BlockSpec((tc, B_WIDTH), rows)] * 4 + [
            pl.BlockSpec((ns, B_HEADS, DH, DH), st),
            pl.BlockSpec((1, DH), const),
            pl.BlockSpec((tc, tc), const),
        ],
        out_specs=[pl.BlockSpec((tc, B_WIDTH), rows), pl.BlockSpec((ns, B_HEADS, DH, DH), st)],
        out_shape=[jax.ShapeDtypeStruct((n * t, B_WIDTH), bf16), jax.ShapeDtypeStruct((n, B_HEADS, DH, DH), f32)],
        scratch_shapes=[pltpu.VMEM((ns, B_HEADS, DH, DH), f32)],
        compiler_params=pltpu.CompilerParams(
            dimension_semantics=("arbitrary", "arbitrary"), vmem_limit_bytes=VMEM_LIMIT),
        name="hgrn",
    )(qb, f, ib, zs, s0, go, _block_tri(tc, sc if fc is None else fc))


def _rel_bucket_np(dist):
    n = np.maximum(dist, 0)
    exact = REL_BUCKETS // 2
    scale = np.float32((REL_BUCKETS - exact) / math.log(REL_MAX_DIST / exact))
    large = exact + (np.log(np.maximum(n, exact).astype(np.float32) / np.float32(exact)) * scale).astype(np.int32)
    return np.where(n < exact, n, np.minimum(large, REL_BUCKETS - 1)).astype(np.int32)


def _bucket_starts():
    buckets = _rel_bucket_np(np.arange(2 * REL_MAX_DIST))
    assert np.all(np.diff(buckets) >= 0) and buckets[-1] == REL_BUCKETS - 1
    return [int(np.argmax(buckets >= b)) for b in range(REL_BUCKETS)]


def _bias_table_kernel(rb_ref, o_ref, *, off, a0, a1, sub_far):
    h = pl.program_id(0)
    shape = o_ref.shape[1:]
    d = off + a0 * lax.broadcasted_iota(jnp.int32, shape, 0) + a1 * lax.broadcasted_iota(jnp.int32, shape, 1)
    starts = _bucket_starts()
    far = rb_ref[REL_BUCKETS - 1, h]
    v = jnp.full(shape, far, f32)
    for b in range(REL_BUCKETS - 2, -1, -1):
        v = jnp.where(d < starts[b + 1], rb_ref[b, h], v)
    o_ref[0] = v - far if sub_far else v


def _bias_table(rel_bias, shape, off, a0, a1, sub_far):
    return pl.pallas_call(
        functools.partial(_bias_table_kernel, off=off, a0=a0, a1=a1, sub_far=sub_far),
        grid=(A_HEADS,),
        in_specs=[pl.BlockSpec(memory_space=pltpu.SMEM)],
        out_specs=pl.BlockSpec((1,) + tuple(shape), lambda h: (h, 0, 0)),
        out_shape=jax.ShapeDtypeStruct((A_HEADS,) + tuple(shape), f32),
        compiler_params=pltpu.CompilerParams(dimension_semantics=("arbitrary",)),
        name="bias_table",
    )(rel_bias.astype(f32))


def _cmp_bias_t(rel_bias, t0, nt, n_chunk):
    return _bias_table(rel_bias, (n_chunk, nt), t0 - (CMP_BLOCK - 1), -CMP_STRIDE, 1, False)


def _near_tiles_t(rel_bias):
    return _bias_table(rel_bias, (QB, 2 * QB), 0, -1, 1, True)


def _compress_combine(lhs, wcat, pe2):
    out = jnp.dot(lhs, wcat, preferred_element_type=f32)
    pc = jnp.dot(pe2, wcat, preferred_element_type=f32)
    const = pc[0:1, :DH] + pc[1:2, DH:]
    r = lhs.shape[0]
    return out[:, :DH] + pltpu.roll(out[:, DH:], r - 1, 0) + const


def _compress_kernel(kv_ref, wk_ref, wv_ref, pek_ref, pev_ref, gk_ref, kc_ref, vct_ref):
    nchunk = kv_ref.shape[0] // (4 * CMP_STRIDE)

    def chunk_rows(cg):
        return jnp.concatenate(
            [kv_ref[pl.ds(4 * l + cg, nchunk, stride=4 * CMP_STRIDE), :].astype(bf16) for l in range(CMP_STRIDE)],
            axis=1)

    kc = _compress_combine(jnp.concatenate([chunk_rows(0), chunk_rows(1)], axis=0), wk_ref[...], pek_ref[...])
    vc = _compress_combine(jnp.concatenate([chunk_rows(2), chunk_rows(3)], axis=0), wv_ref[...], pev_ref[...])
    kc = _head_norm(kc, gk_ref[...])
    for g in range(A_KV_GROUPS):
        kc_ref[0, g] = kc[g * nchunk:(g + 1) * nchunk].astype(kc_ref.dtype)
        vct_ref[0, g] = vc[g * nchunk:(g + 1) * nchunk].T.astype(vct_ref.dtype)


def _prep_cmp_weights(w, pe):
    half = CMP_STRIDE * DH
    wcat = jnp.concatenate([w[:half], w[half:]], axis=1).astype(bf16)
    return wcat, pe.reshape(2, half).astype(bf16)


def _compress(kvc, wk, wv, pek, pev, gk, n, t):
    nchunk = t // CMP_STRIDE
    assert nchunk == DH, "the transposed v_c block is square"
    const = lambda i: (0, 0)
    return pl.pallas_call(
        _compress_kernel, grid=(n,),
        in_specs=[pl.BlockSpec((t * 4, DH), lambda i: (i, 0)),
                  pl.BlockSpec(wk.shape, const), pl.BlockSpec(wv.shape, const),
                  pl.BlockSpec(pek.shape, const), pl.BlockSpec(pev.shape, const), pl.BlockSpec((1, DH), const)],
        out_specs=[pl.BlockSpec((1, A_KV_GROUPS, nchunk, DH), lambda i: (i, 0, 0, 0))] * 2,
        out_shape=[jax.ShapeDtypeStruct((n, A_KV_GROUPS, nchunk, DH), bf16)] * 2,
        compiler_params=pltpu.CompilerParams(dimension_semantics=("arbitrary",), vmem_limit_bytes=VMEM_LIMIT),
        name="compress",
    )(kvc, wk, wv, pek, pev, gk)


def _softmax_tile(carry, s, mask, v_bf):
    m, l, acc = carry
    s = jnp.where(mask, s, NEG)
    m_new = jnp.maximum(m, jnp.max(s, axis=-1, keepdims=True))
    alpha = jnp.exp(m - m_new)
    p = jnp.where(mask, jnp.exp(s - m_new), 0.0)
    l = alpha * l + jnp.sum(p, axis=-1, keepdims=True)
    h, r, k = p.shape
    pv = jnp.dot(p.reshape(h * r, k).astype(bf16), v_bf, preferred_element_type=f32).reshape(h, r, DH)
    return m_new, l, alpha * acc + pv


def _softmax_finish(carry):
    m, l, acc = carry
    return acc / jnp.where(l > 0, l, 1.0)


def _select_blocks(score_t, tpos, n_blocks):
    nb = score_t.shape[0]
    j = lax.broadcasted_iota(jnp.int32, score_t.shape, 0)
    valid = (j * SLC_BLOCK <= tpos) & (j < n_blocks)
    cur = tpos >> 6
    forced = (j == 0) | (j == cur) | (j == cur - 1)
    val = jnp.where(valid, jnp.where(forced, 1e30, score_t), -1.0)
    rank = jnp.zeros(score_t.shape, jnp.int32)
    for i in range(n_blocks):
        vi = val[i:i + 1, :]
        beats = (vi > val) | ((vi == val) & (i < j))
        rank = rank + beats.astype(jnp.int32)
    return jnp.where((rank < N_SELECT) & valid, 1.0, 0.0)


QB = 128
NSA_PROMPT_SEQS = 2


NEG_M = -1e30
NEG_S = -2e30


def _nsa_prompt_kernel(q_ref, kc_ref, vct_ref, bc_ref, kvs_ref, kvw_ref, gate_ref, za_ref, bt_ref, mt_ref, et_ref,
                       o_ref, ks_scr, vts_scr, kw_scr, vtw_scr, km_scr):
    qb = pl.program_id(1)
    q0 = qb * QB
    n_far = jnp.maximum(qb - 1, 0)
    nkt = vts_scr.shape[1]
    nq4 = A_HPG * QB
    nseq = q_ref.shape[0]
    chains = [(s, g) for s in range(nseq) for g in range(A_KV_GROUPS)]
    groups = range(len(chains))

    @pl.when(qb == 0)
    def _():
        for c, (s, g) in enumerate(chains):
            for kt in range(nkt):
                rows = slice(kt * QB, (kt + 1) * QB)
                base = kt * 4 * QB
                ks_scr[c, rows, :] = kvs_ref[s, pl.ds(base + g, QB, stride=4), :].astype(bf16)
                vts_scr[c, kt] = kvs_ref[s, pl.ds(base + 2 + g, QB, stride=4), :].T.astype(bf16)
                kw_scr[c, rows, :] = kvw_ref[s, pl.ds(base + g, QB, stride=4), :].astype(bf16)
                vtw_scr[c, kt] = kvw_ref[s, pl.ds(base + 2 + g, QB, stride=4), :].T.astype(bf16)

    qts = [jnp.concatenate([q_ref[s, :, (g * A_HPG + h) * DH:(g * A_HPG + h + 1) * DH].astype(f32).T
                            for h in range(A_HPG)], axis=1).astype(bf16) for s, g in chains]
    kk = lax.broadcasted_iota(jnp.int32, (QB, nq4), 0)
    tt = lax.broadcasted_iota(jnp.int32, (QB, nq4), 1) & (QB - 1)

    def heads(ref, g, *idx):
        return jnp.concatenate([ref[(g * A_HPG + h,) + idx] for h in range(A_HPG)], axis=1)

    o_c = []
    for c, (s, g) in enumerate(chains):
        sc = jnp.dot(kc_ref[s, g], qts[c], preferred_element_type=f32) + heads(bc_ref, g)
        mask_c = q0 + tt >= CMP_STRIDE * kk + (CMP_BLOCK - 1)
        sc = jnp.where(mask_c, sc, NEG)
        mc = jnp.max(sc, axis=0, keepdims=True)
        ec = jnp.where(mask_c, jnp.exp(sc - mc), 0.0)
        lc = jnp.sum(ec, axis=0, keepdims=True)
        pc = ec / jnp.where(lc > 0, lc, 1.0)
        o_c.append(jnp.dot(vct_ref[s, g], pc.astype(bf16), preferred_element_type=f32))
        ps = pc[:, 0:QB] + pc[:, QB:2 * QB] + pc[:, 2 * QB:3 * QB] + pc[:, 3 * QB:4 * QB]
        score_t = _dot_small_int_lhs(mt_ref[...], ps, (((1,), (0,)), ((), ())))
        nb = score_t.shape[0]
        sel_t = _select_blocks(score_t, q0 + lax.broadcasted_iota(jnp.int32, (nb, QB), 1), nb)
        for j in range(nb):
            km_scr[c, j * SLC_BLOCK:(j + 1) * SLC_BLOCK, :] = jnp.broadcast_to(sel_t[j:j + 1, :], (SLC_BLOCK, QB))

    def tile(carry, qt, k_tile, vt_tile, mask, bias):
        m, l, acc = carry
        s = jnp.dot(k_tile, qt, preferred_element_type=f32)
        if bias is not None:
            s = s + bias
        s = jnp.where(mask, s, NEG_S)
        m_new = jnp.maximum(m, jnp.max(s, axis=0, keepdims=True))
        alpha = jnp.exp(m - m_new)
        p = jnp.exp(s - m_new)
        l = alpha * l + jnp.sum(p, axis=0, keepdims=True)
        return m_new, l, alpha * acc + jnp.dot(vt_tile, p.astype(bf16), preferred_element_type=f32)

    def init():
        return jnp.full((1, nq4), NEG_M, f32), jnp.zeros((1, nq4), f32), jnp.zeros((DH, nq4), f32)

    def finish(carry):
        m, l, acc = carry
        return acc / jnp.where(l > 0, l, 1.0)

    def key_rows(kt):
        return pl.ds(pl.multiple_of(kt * QB, QB), QB)

    def lanes4(x):
        return jnp.concatenate([x] * A_HPG, axis=1)

    def near_span(g, k_scr, vt_scr, rs, use_sel):
        ks, vts, masks, biases = [], [], [], []
        for r in rs:
            kt = jnp.maximum(qb - r, 0)
            ks.append(k_scr[g, key_rows(kt), :])
            vts.append(vt_scr[g, kt])
            if use_sel:
                mask = lanes4(km_scr[g, key_rows(kt), :]) > 0.5
                if r == 0:
                    mask = mask & (kk <= tt)
            else:
                mask = (kk <= tt) if r == 0 else (kk >= tt) if r == WINDOW // QB else (kk >= 0)
            masks.append(mask)
            off = jnp.where(qb - r >= 0, 0.0, NEG_S)
            biases.append(heads(bt_ref, chains[g][1], slice(None), slice(r * QB, (r + 1) * QB)) + off if r < 2
                          else jnp.full((QB, nq4), off, f32))
        cat = lambda xs, axis: jnp.concatenate(xs, axis=axis)
        return cat(ks, 0), cat(vts, 1), cat(masks, 0), cat(biases, 0)

    o_w = [finish(tile(init(), qts[g], *near_span(g, kw_scr, vtw_scr, (4, 3, 2, 1, 0), False))) for g in groups]

    kk2 = lax.broadcasted_iota(jnp.int32, (2 * QB, nq4), 0)

    def far(i, carries):
        rows = pl.ds(pl.multiple_of(i * 2 * QB, 2 * QB), 2 * QB)
        in_range = (kk2 + i * 2 * QB) < n_far * QB
        out = []
        for g in groups:
            mask = (lanes4(km_scr[g, rows, :]) > 0.5) & in_range
            vt = jnp.concatenate([vts_scr[g, 2 * i], vts_scr[g, jnp.minimum(2 * i + 1, nkt - 1)]], axis=1)
            out.append(tile(carries[g], qts[g], ks_scr[g, rows, :], vt, mask, None))
        return tuple(out)

    carries = lax.fori_loop(0, (n_far + 1) // 2, far, tuple(init() for g in groups))
    o_s = [finish(tile(carries[g], qts[g], *near_span(g, ks_scr, vts_scr, (1, 0), True))) for g in groups]

    for s in range(nseq):
        za = za_ref[s]
        outs = []
        for g in range(A_KV_GROUPS):
            c = s * A_KV_GROUPS + g
            gate_t = gate_ref[s, :, g * DH:(g + 1) * DH].T
            for h in range(A_HPG):
                cols = slice(h * QB, (h + 1) * QB)

                def grow(br):
                    return gate_t[br * A_HPG + h:br * A_HPG + h + 1, :]
                o = grow(0) * o_c[c][:, cols] + grow(1) * o_s[c][:, cols] + grow(2) * o_w[c][:, cols]
                outs.append(o.T * za[:, (g * A_HPG + h) * DH:(g * A_HPG + h + 1) * DH])
        o_ref[s] = jnp.concatenate(outs, axis=1).astype(o_ref.dtype)


def _cmp_to_slc_t(n_cmp_pad, n_slc_pad, n_cmp, n_slc):
    c0 = np.arange(n_cmp_pad)[None, :] * CMP_STRIDE
    s0 = np.arange(n_slc_pad)[:, None] * SLC_BLOCK
    ov = np.minimum(c0 + CMP_BLOCK, s0 + SLC_BLOCK) - np.maximum(c0, s0)
    m = np.maximum(ov, 0).astype(np.float32) / CMP_STRIDE
    m[:, n_cmp:] = 0
    m[n_slc:, :] = 0
    return m


def _nsa_prompt(qa, kc, vct, kvs, kvw, gate, za, rel_bias, n, t):
    nq = t // QB
    n_cmp = (t - CMP_BLOCK) // CMP_STRIDE + 1
    n_slc = -(-t // SLC_BLOCK)
    assert kc.shape[2] == QB and n_slc % 8 == 0
    bias_c = _cmp_bias_t(rel_bias, 0, t, QB)
    bt = _near_tiles_t(rel_bias)
    mt = jnp.asarray(_cmp_to_slc_t(QB, n_slc, n_cmp, n_slc), dtype=bf16)
    et = jnp.asarray((np.arange(t)[:, None] // SLC_BLOCK == np.arange(n_slc)[None, :]).astype(np.float32), dtype=bf16)
    nseq = NSA_PROMPT_SEQS
    assert n % nseq == 0
    ng = A_KV_GROUPS
    nchain = nseq * ng
    qrow = lambda w: pl.BlockSpec((nseq, QB, w), lambda i, b: (i, b, 0))
    seq4 = lambda shape: pl.BlockSpec((nseq,) + shape, lambda i, b: (i, 0, 0, 0))
    seq3 = lambda shape: pl.BlockSpec((nseq,) + shape, lambda i, b: (i, 0, 0), pipeline_mode=pl.Buffered(1))
    by_seq = lambda a, w: a.reshape(n, -1, w)
    out = pl.pallas_call(
        _nsa_prompt_kernel, grid=(n // nseq, nq),
        in_specs=[
            qrow(A_WIDTH),
            seq4((ng, QB, DH)),
            seq4((ng, DH, QB)),
            pl.BlockSpec((A_HEADS, QB, QB), lambda i, b: (0, 0, b)),
            seq3((t * 4, DH)),
            seq3((t * 4, DH)),
            qrow(ng * DH),
            qrow(A_WIDTH),
            pl.BlockSpec((A_HEADS, QB, 2 * QB), lambda i, b: (0, 0, 0)),
            pl.BlockSpec(mt.shape, lambda i, b: (0, 0)),
            pl.BlockSpec(et.shape, lambda i, b: (0, 0)),
        ],
        out_specs=qrow(A_WIDTH),
        out_shape=jax.ShapeDtypeStruct((n, t, A_WIDTH), bf16),
        scratch_shapes=[pltpu.VMEM((nchain, t, DH), bf16), pltpu.VMEM((nchain, nq, DH, QB), bf16),
                        pltpu.VMEM((nchain, t, DH), bf16), pltpu.VMEM((nchain, nq, DH, QB), bf16),
                        pltpu.VMEM((nchain, t, QB), f32)],
        compiler_params=pltpu.CompilerParams(
            dimension_semantics=("arbitrary", "arbitrary"), vmem_limit_bytes=VMEM_LIMIT),
        name="nsa_prompt",
    )(by_seq(qa, A_WIDTH), kc, vct, bias_c, by_seq(kvs, DH), by_seq(kvw, DH), by_seq(gate, ng * DH),
      by_seq(za, A_WIDTH), bt, mt, et)
    return out.reshape(n * t, A_WIDTH)


CMP_PITCH = 24


def _nsa_sample_kernel(pt_ref, *refs, n_pages, page, t_new, n_blocks):
    del pt_ref
    cmp_pages = refs[:n_pages]
    slc_pages = refs[n_pages:2 * n_pages]
    (q_ref, kvs_new_ref, kvw_new_ref, win_ref, wk_ref, wv_ref, pek_ref, pev_ref, gk_ref, bc_ref, bl_ref, bn_ref,
     gate_ref, za_ref, mt_ref, e_ref, o_ref, win_out_ref, xc_scr) = refs[2 * n_pages:]
    past = n_pages * page
    nchunk = past // CMP_STRIDE
    pad_new = QB
    ng = A_KV_GROUPS
    groups = range(ng)

    cpp = page // CMP_STRIDE
    for p in range(n_pages):
        for cg in range(4):
            x = cmp_pages[p][0, pl.ds(cg, page, stride=4), :]
            for c in range(cpp):
                r0 = (p * cpp + c) * CMP_PITCH
                xc_scr[cg, r0:r0 + CMP_STRIDE, :] = x[c * CMP_STRIDE:(c + 1) * CMP_STRIDE]

    def chunk_rows(cg):
        return jnp.concatenate(
            [xc_scr[cg, pl.ds(l, nchunk, stride=CMP_PITCH), :].astype(bf16) for l in range(CMP_STRIDE)], axis=1)

    kc = _compress_combine(jnp.concatenate([chunk_rows(0), chunk_rows(1)], axis=0), wk_ref[...], pek_ref[...])
    vc = _compress_combine(jnp.concatenate([chunk_rows(2), chunk_rows(3)], axis=0), wv_ref[...], pev_ref[...])
    kc = _head_norm(kc, gk_ref[...]).astype(bf16)
    vc = vc.astype(bf16)

    qall = q_ref[...]
    qs = [jnp.concatenate([qall[:, (g * A_HPG + h) * DH:(g * A_HPG + h + 1) * DH] for h in range(A_HPG)], axis=0)
          for g in groups]
    zeros_pad = jnp.zeros((pad_new - t_new, DH), bf16)

    def new_rows(ref, c):
        return jnp.concatenate([ref[pl.ds(c, t_new, stride=4), :].astype(bf16), zeros_pad], axis=0)

    def logits(ks):
        return jnp.stack([lax.dot_general(qs[g], ks[g], (((1,), (1,)), ((), ())), preferred_element_type=f32)
                          .reshape(A_HPG, t_new, ks[g].shape[0]) for g in groups], axis=0)

    def softmax_many(items):
        kmax = max(s.shape[-1] for s, _ in items)
        padded = []
        for s, m in items:
            s = jnp.where(m, s, NEG)
            if s.shape[-1] < kmax:
                s = jnp.concatenate([s, jnp.full(s.shape[:-1] + (kmax - s.shape[-1],), NEG, f32)], axis=-1)
            padded.append(s)
        s_all = jnp.stack(padded, axis=0)
        valid = s_all > 0.5 * NEG
        e = jnp.where(valid, jnp.exp(s_all - jnp.max(s_all, axis=-1, keepdims=True)), 0.0)
        l = jnp.sum(e, axis=-1, keepdims=True)
        p = e / jnp.where(l > 0, l, 1.0)
        return [p[i][..., :s.shape[-1]] for i, (s, _) in enumerate(items)]

    def pv(p, vs):
        return jnp.stack([jnp.dot(p[g].reshape(A_HPG * t_new, -1).astype(bf16), vs[g], preferred_element_type=f32)
                          .reshape(A_HPG, t_new, DH) for g in groups], axis=0)

    def tail_bias(s):
        bias = jnp.concatenate([bl_ref[...], bn_ref[...]], axis=-1).reshape(ng, A_HPG, t_new, 2 * QB)
        nk = s.shape[-1]
        return jnp.concatenate([s[..., :nk - 2 * QB], s[..., nk - 2 * QB:] + bias], axis=-1)

    tt = lax.broadcasted_iota(jnp.int32, (t_new, pad_new), 0)
    uu = lax.broadcasted_iota(jnp.int32, (t_new, pad_new), 1)
    new_mask = uu <= tt

    ti = lax.broadcasted_iota(jnp.int32, (t_new, nchunk), 0)
    ci = lax.broadcasted_iota(jnp.int32, (t_new, nchunk), 1)
    cmask = (past + ti >= CMP_STRIDE * ci + (CMP_BLOCK - 1))[None, None]
    sc = logits([kc[g * nchunk:(g + 1) * nchunk] for g in groups]) + bc_ref[...].reshape(ng, A_HPG, t_new, nchunk)
    nw = win_ref.shape[1] // 4
    kw = [jnp.concatenate([win_ref[0, pl.ds(g, nw, stride=4), :].astype(bf16), new_rows(kvw_new_ref, g)], axis=0)
          for g in groups]
    vw = [jnp.concatenate([win_ref[0, pl.ds(2 + g, nw, stride=4), :].astype(bf16), new_rows(kvw_new_ref, 2 + g)], axis=0)
          for g in groups]
    wmask = lax.broadcasted_iota(jnp.int32, (t_new, nw), 1) >= lax.broadcasted_iota(jnp.int32, (t_new, nw), 0)
    pc, pw = softmax_many([(sc, cmask),
                           (tail_bias(logits(kw)), jnp.concatenate([wmask, new_mask], axis=1)[None, None])])
    o_c = pv(pc, [vc[g * nchunk:(g + 1) * nchunk] for g in groups])
    o_w = pv(pw, vw)

    ps = jnp.concatenate([pc[g, 0] + pc[g, 1] + pc[g, 2] + pc[g, 3] for g in groups], axis=0)
    score_t = _dot_small_int_lhs(mt_ref[...], ps, (((1,), (1,)), ((), ())))
    lane = lax.broadcasted_iota(jnp.int32, score_t.shape, 1)
    tpos = past + jnp.where(lane >= t_new, lane - t_new, lane)
    sel_t = _select_blocks(score_t, tpos, n_blocks)
    key_mask = lax.dot_general(sel_t.astype(bf16), e_ref[...], (((0,), (0,)), ((), ())),
                               preferred_element_type=f32)

    new_mask_f = jnp.where(new_mask, 1.0, 0.0)
    sel_mask = jnp.concatenate([key_mask, jnp.concatenate([new_mask_f] * ng, axis=0)], axis=1)
    sel_mask = sel_mask.reshape(ng, 1, t_new, past + pad_new) > 0.5
    ks = [jnp.concatenate([slc_pages[p][0, pl.ds(g, page, stride=4), :].astype(bf16) for p in range(n_pages)]
                          + [new_rows(kvs_new_ref, g)], axis=0) for g in groups]
    vs = [jnp.concatenate([slc_pages[p][0, pl.ds(2 + g, page, stride=4), :].astype(bf16) for p in range(n_pages)]
                          + [new_rows(kvs_new_ref, 2 + g)], axis=0) for g in groups]
    o_s = pv(softmax_many([(tail_bias(logits(ks)), sel_mask)])[0], vs)

    gate_all = gate_ref[...]
    za = za_ref[...]
    outs = []
    for g in groups:
        gate = gate_all[:, g * DH:(g + 1) * DH]
        for h in range(A_HPG):
            def gcol(br):
                return gate[:, br * A_HPG + h:br * A_HPG + h + 1]
            o = gcol(0) * o_c[g, h] + gcol(1) * o_s[g, h] + gcol(2) * o_w[g, h]
            outs.append(o * za[:, (g * A_HPG + h) * DH:(g * A_HPG + h + 1) * DH])
    o_ref[...] = jnp.concatenate(outs, axis=1).astype(o_ref.dtype)

    nrow = win_ref.shape[1]
    win_out_ref[0, 0:nrow - 4 * t_new, :] = win_ref[0, 4 * t_new:nrow, :]
    win_out_ref[0, nrow - 4 * t_new:nrow, :] = kvw_new_ref[...]


def _nsa_sample(qa, kvs_new, kvw_new, cache_cmp, cache_slc, cache_win, page_table, wk, wv, pek, pev, gk,
                gate, za, rel_bias, t_new):
    n, n_pages = page_table.shape
    page = cache_cmp.shape[1] // 4
    past = n_pages * page
    wb = cache_win.shape[1] // 4
    assert wb == WINDOW and past % QB == 0 and past >= WINDOW
    nchunk = past // CMP_STRIDE
    n_cmp = (past + t_new - CMP_BLOCK) // CMP_STRIDE + 1
    assert n_cmp < nchunk
    n_slc = -(-(past + t_new) // SLC_BLOCK)
    nb_pad = -(-n_slc // 8) * 8
    bias_c = _bias_table(rel_bias, (t_new, nchunk), past - (CMP_BLOCK - 1), 1, -CMP_STRIDE, False)
    b_last = _bias_table(rel_bias, (t_new, QB), QB, 1, -1, True)
    b_new = _bias_table(rel_bias, (t_new, QB), 0, 1, -1, True)
    mt = jnp.asarray(_cmp_to_slc_t(nchunk, nb_pad, n_cmp, n_slc), dtype=bf16)
    e = jnp.asarray((np.arange(past)[None, :] // SLC_BLOCK == np.arange(nb_pad)[:, None]).astype(np.float32), dtype=bf16)

    def page_spec(p):
        return pl.BlockSpec((1, page * 4, DH), lambda i, pt: (pt[i, p], 0, 0))

    rowblk = lambda w: pl.BlockSpec((t_new, w), lambda i, pt: (i, 0))
    full = lambda a: pl.BlockSpec(a.shape, lambda i, pt: (0,) * a.ndim)
    in_specs = ([page_spec(p) for p in range(n_pages)] * 2 + [
        rowblk(A_WIDTH),
        pl.BlockSpec((t_new * 4, DH), lambda i, pt: (i, 0)),
        pl.BlockSpec((t_new * 4, DH), lambda i, pt: (i, 0)),
        pl.BlockSpec((1, wb * 4, DH), lambda i, pt: (i, 0, 0)),
        full(wk), full(wv), full(pek), full(pev), full(gk), full(bias_c), full(b_last), full(b_new),
        rowblk(A_KV_GROUPS * DH), rowblk(A_WIDTH), full(mt), full(e)])
    grid_spec = pltpu.PrefetchScalarGridSpec(
        num_scalar_prefetch=1, grid=(n,), in_specs=in_specs,
        out_specs=[rowblk(A_WIDTH), pl.BlockSpec((1, wb * 4, DH), lambda i, pt: (i, 0, 0))],
        scratch_shapes=[pltpu.VMEM((4, nchunk * CMP_PITCH, DH), f32)])
    return pl.pallas_call(
        functools.partial(_nsa_sample_kernel, n_pages=n_pages, page=page, t_new=t_new, n_blocks=n_slc),
        grid_spec=grid_spec,
        out_shape=[jax.ShapeDtypeStruct((n * t_new, A_WIDTH), f32), jax.ShapeDtypeStruct(cache_win.shape, f32)],
        compiler_params=pltpu.CompilerParams(dimension_semantics=("arbitrary",), vmem_limit_bytes=VMEM_LIMIT),
        name="nsa_sample",
    )(page_table, *([cache_cmp] * n_pages), *([cache_slc] * n_pages), qa, kvs_new, kvw_new, cache_win,
      wk, wv, pek, pev, gk, bias_c, b_last, b_new, gate, za, mt, e)


def _outproj_kernel(x_ref, oa_ref, ob_ref, w_ref, y_ref):
    acc = jnp.dot(oa_ref[...].astype(bf16), w_ref[0:A_WIDTH, :], preferred_element_type=f32)
    acc = acc + jnp.dot(ob_ref[...].astype(bf16), w_ref[A_WIDTH:, :], preferred_element_type=f32)
    y_ref[...] = x_ref[...] + acc


def _outproj(x2d, oa, ob, w_out_bf, tm):
    m = x2d.shape[0]
    assert m % tm == 0
    row = lambda i: (i, 0)
    return pl.pallas_call(
        _outproj_kernel, grid=(m // tm,),
        in_specs=[pl.BlockSpec((tm, D_MODEL), row), pl.BlockSpec((tm, A_WIDTH), row), pl.BlockSpec((tm, B_WIDTH), row),
                  pl.BlockSpec(w_out_bf.shape, lambda i: (0, 0))],
        out_specs=pl.BlockSpec((tm, D_MODEL), row),
        out_shape=jax.ShapeDtypeStruct((m, D_MODEL), f32),
        compiler_params=pltpu.CompilerParams(dimension_semantics=("arbitrary",), vmem_limit_bytes=VMEM_LIMIT),
        name="outproj",
    )(x2d, oa, ob, w_out_bf)


def kernel(x_prompt, x_sample, cache_kv_cmp, cache_kv_slc, cache_kv_win, state_hgrn, page_table, g_norm, w_in, w_out,
           g_q, g_k_slc, g_k_win, g_k_cmp, w_cmp_k, w_cmp_v, pe_cmp_k, pe_cmp_v, rel_bias, lb_logits, g_o_hgrn):
    depth = w_in.shape[0]
    assert depth == 1, "single-layer trunk"
    nb, t, _ = x_prompt.shape
    ns, ts, _ = x_sample.shape
    row = lambda a: a.astype(f32)[None]
    lower = jnp.cumsum(jax.nn.softmax(lb_logits.astype(f32), axis=0), axis=0)[0]
    w_main, w_gate = _prep_proj_weights(w_in[0])
    w_out_bf = w_out[0].astype(bf16)
    wk, pek = _prep_cmp_weights(w_cmp_k[0], pe_cmp_k[0])
    wv, pev = _prep_cmp_weights(w_cmp_v[0], pe_cmp_v[0])
    proj_args = (row(g_norm[0]), w_main, w_gate, row(g_q[0]), row(g_k_slc[0]), row(g_k_win[0]), row(lower))
    kv6 = lambda a, n_, t_: a.reshape(1, n_, t_, 2, A_KV_GROUPS, DH)

    xp = x_prompt.reshape(nb * t, D_MODEL)
    wlen = min(WINDOW, t)
    assert wlen == PROJ_TM, "the proj row tile doubles as the prompt's final window"
    qa, kvc, kvs, kvw, gate, za, qb, f, ib, zb, win_p = _proj(xp, *proj_args, tm=PROJ_TM, t_seq=t)
    s0 = jnp.zeros((nb, B_HEADS, DH, DH), f32)
    ob, st_p = _hgrn(qb, f, ib, zb, s0, row(g_o_hgrn[0]), nb, t, tc=128, sc=16, fc=64)
    kc, vct = _compress(kvc, wk, wv, pek, pev, row(g_k_cmp[0]), nb, t)
    oa = _nsa_prompt(qa, kc, vct, kvs, kvw, gate, za, rel_bias, nb, t)
    y_p = _outproj(xp, oa, ob, w_out_bf, tm=PROJ_TM).reshape(nb, t, D_MODEL)
    win_p = kv6(win_p, nb, wlen)

    xs = x_sample.reshape(ns * ts, D_MODEL)
    qa, kvc_s, kvs_s, kvw_s, gate, za, qb, f, ib, zb, _ = _proj(xs, *proj_args, tm=PROJ_TM, t_seq=PROJ_TM)
    seq_per_step = 8
    ob, st_s = _hgrn(qb, f, ib, zb, state_hgrn[0].astype(f32), row(g_o_hgrn[0]), ns, ts, tc=seq_per_step * ts, sc=ts,
                     ns=seq_per_step)
    pool = cache_kv_cmp.shape[1]
    page = cache_kv_cmp.shape[2]
    oa, win_s = _nsa_sample(
        qa, kvs_s, kvw_s, cache_kv_cmp[0].reshape(pool, page * 4, DH), cache_kv_slc[0].reshape(pool, page * 4, DH),
        cache_kv_win[0].reshape(ns, -1, DH), page_table, wk, wv, pek, pev, row(g_k_cmp[0]), gate, za, rel_bias, ts)
    y_s = _outproj(xs, oa, ob, w_out_bf, tm=PROJ_TM).reshape(ns, ts, D_MODEL)

    return (y_p, y_s, kv6(kvc, nb, t), kv6(kvs, nb, t), win_p, st_p[None].astype(x_prompt.dtype),
            kv6(kvc_s, ns, ts), kv6(kvs_s, ns, ts), kv6(win_s, ns, WINDOW), st_s[None].astype(state_hgrn.dtype))
```

```python
import functools
import math

import jax
import jax.numpy as jnp
import numpy as np
from jax import lax
from jax.experimental import pallas as pl
from jax.experimental.pallas import tpu as pltpu

f32 = jnp.float32
bf16 = jnp.bfloat16

D_MODEL = 2048
A_HEADS = 8
A_KV_GROUPS = 2
A_HPG = A_HEADS // A_KV_GROUPS
DH = 128
A_WIDTH = A_HEADS * DH
CMP_BLOCK = 32
CMP_STRIDE = 16
SLC_BLOCK = 64
N_SELECT = 16
WINDOW = 512
B_HEADS = 8
B_WIDTH = B_HEADS * DH
REL_BUCKETS = 32
REL_MAX_DIST = 128
EPS = 1e-6
NEG = -1e30

VMEM_LIMIT = 56 * 1024 * 1024


def _sigmoid(x):
    return 1.0 / (1.0 + jnp.exp(-x))


def _head_norm(a, g):
    return a * lax.rsqrt(jnp.mean(a * a, axis=-1, keepdims=True) + EPS) * g


def _dot_small_int_lhs(m_bf, x, dims):
    hi = x.astype(bf16)
    r1 = x - hi.astype(f32)
    mid = r1.astype(bf16)
    lo = (r1 - mid.astype(f32)).astype(bf16)
    d = lambda p: lax.dot_general(m_bf, p, dims, preferred_element_type=f32)
    return d(hi) + d(mid) + d(lo)


PROJ_TN = 512
PROJ_TM = 512
PROJ_SEG = {0: "qa", 1: "qa", 2: "kvc", 3: "kvs", 4: "kvw", 5: "za", 6: "za", 7: "qb", 8: "qb", 9: "f", 10: "f",
            11: "ib", 12: "ib", 13: "zb", 14: "zb"}
PROJ_SLABS = ((0, 5), (5, 11), (11, 15))


def _proj_tiles(h, w_ref, first_tile, n_tiles, tm, out, par):
    for jj in range(n_tiles):
        j = first_tile + jj
        acc = jnp.dot(h, w_ref[:, jj * PROJ_TN:(jj + 1) * PROJ_TN], preferred_element_type=f32)
        name = PROJ_SEG[j]
        k = j - min(t for t, s in PROJ_SEG.items() if s == name)
        lanes = slice(k * PROJ_TN, (k + 1) * PROJ_TN)

        def cols(c):
            return acc[:, c * DH:(c + 1) * DH]

        if name == "qa":
            gq = par["gq"][...]
            out["qa"][:, lanes] = jnp.concatenate(
                [_head_norm(cols(c), gq) * (DH ** -0.5) for c in range(4)], axis=1).astype(out["qa"].dtype)
        elif name in ("kvc", "kvs", "kvw"):
            g_ref = {"kvc": None, "kvs": par["gks"], "kvw": par["gkw"]}[name]
            for c in range(4):
                val = _head_norm(cols(c), g_ref[...]) if (g_ref is not None and c < 2) else cols(c)
                out[name][pl.ds(c, tm, stride=4), :] = val
                if name == "kvw":
                    out["tail"][pl.ds(c, tm, stride=4), :] = val
        elif name in ("za", "zb"):
            out[name][:, lanes] = acc * _sigmoid(acc)
        elif name == "f":
            lb = par["lb"][:, lanes]
            out["f"][:, lanes] = lb + (1.0 - lb) * _sigmoid(acc)
        else:
            out[name][:, lanes] = acc


def _proj_first_kernel(x_ref, gn_ref, w_ref, wg_ref, gq_ref, gks_ref, gkw_ref,
                       h_ref, qa_ref, kvc_ref, kvs_ref, kvw_ref, tail_ref, gate_ref):
    x = x_ref[...]
    h = (x * lax.rsqrt(jnp.mean(x * x, axis=-1, keepdims=True) + EPS) * gn_ref[...]).astype(bf16)
    h_ref[...] = h
    gate_ref[...] = _sigmoid(jnp.dot(h, wg_ref[...], preferred_element_type=f32))
    first, last = PROJ_SLABS[0]
    _proj_tiles(h, w_ref, first, last - first, x_ref.shape[0],
                dict(qa=qa_ref, kvc=kvc_ref, kvs=kvs_ref, kvw=kvw_ref, tail=tail_ref),
                dict(gq=gq_ref, gks=gks_ref, gkw=gkw_ref))


def _proj_rest_kernel(h_ref, w_ref, lb_ref, *out_refs, slab):
    first, last = PROJ_SLABS[slab]
    names = list(dict.fromkeys(PROJ_SEG[j] for j in range(first, last)))
    _proj_tiles(h_ref[...], w_ref, first, last - first, h_ref.shape[0], dict(zip(names, out_refs)), dict(lb=lb_ref))


def _proj(x2d, gn, w_slabs, w_gate, gq, gks, gkw, lb, tm, t_seq):
    m = x2d.shape[0]
    assert m % tm == 0 and t_seq % tm == 0
    tiles_per_seq = t_seq // tm
    row = lambda i: (i, 0)
    const = lambda i: (0, 0)
    params = pltpu.CompilerParams(dimension_semantics=("arbitrary",), vmem_limit_bytes=VMEM_LIMIT)
    kv_shape = jax.ShapeDtypeStruct((m * 4, DH), f32)
    wide = lambda dt: jax.ShapeDtypeStruct((m, A_WIDTH), dt)
    h, qa, kvc, kvs, kvw, tail, gate = pl.pallas_call(
        _proj_first_kernel, grid=(m // tm,),
        in_specs=[pl.BlockSpec((tm, D_MODEL), row), pl.BlockSpec((1, D_MODEL), const),
                  pl.BlockSpec(w_slabs[0].shape, const), pl.BlockSpec(w_gate.shape, const),
                  pl.BlockSpec((1, DH), const), pl.BlockSpec((1, DH), const), pl.BlockSpec((1, DH), const)],
        out_specs=[pl.BlockSpec((tm, D_MODEL), row), pl.BlockSpec((tm, A_WIDTH), row)]
        + [pl.BlockSpec((tm * 4, DH), row)] * 3
        + [pl.BlockSpec((tm * 4, DH), lambda i: (i // tiles_per_seq, 0)), pl.BlockSpec((tm, A_KV_GROUPS * DH), row)],
        out_shape=[jax.ShapeDtypeStruct((m, D_MODEL), bf16), wide(bf16), kv_shape, kv_shape, kv_shape,
                   jax.ShapeDtypeStruct((m // tiles_per_seq * 4, DH), f32),
                   jax.ShapeDtypeStruct((m, A_KV_GROUPS * DH), f32)],
        compiler_params=params, name="proj_a",
    )(x2d, gn, w_slabs[0], w_gate, gq, gks, gkw)

    def rest(slab, n_out):
        return pl.pallas_call(
            functools.partial(_proj_rest_kernel, slab=slab), grid=(m // tm,),
            in_specs=[pl.BlockSpec((tm, D_MODEL), row), pl.BlockSpec(w_slabs[slab].shape, const),
                      pl.BlockSpec((1, B_WIDTH), const)],
            out_specs=[pl.BlockSpec((tm, A_WIDTH), row)] * n_out, out_shape=[wide(f32)] * n_out,
            compiler_params=params, name="proj_" + "abc"[slab],
        )(h, w_slabs[slab], lb)

    za, qb, f = rest(1, 3)
    ib, zb = rest(2, 2)
    return qa, kvc, kvs, kvw, gate, za, qb, f, ib, zb, tail


def _prep_proj_weights(w_in):
    a0 = A_WIDTH + 6 * A_KV_GROUPS * DH
    a1 = a0 + 3 * A_HEADS
    col0 = lambda tile: tile * PROJ_TN if tile * PROJ_TN < a0 else a1 + tile * PROJ_TN - a0
    spans = tuple((col0(first), (last - first) * PROJ_TN) for first, last in PROJ_SLABS)
    rows = 256

    def repack(w_ref, *out_refs):
        for (c0, width), o_ref in zip(spans, out_refs):
            o_ref[...] = w_ref[:, c0:c0 + width].astype(bf16)

    w_slabs = pl.pallas_call(
        repack, grid=(D_MODEL // rows,),
        in_specs=[pl.BlockSpec((rows, w_in.shape[1]), lambda i: (i, 0))],
        out_specs=[pl.BlockSpec((rows, width), lambda i: (i, 0)) for _, width in spans],
        out_shape=[jax.ShapeDtypeStruct((D_MODEL, width), bf16) for _, width in spans],
        compiler_params=pltpu.CompilerParams(dimension_semantics=("arbitrary",), vmem_limit_bytes=VMEM_LIMIT),
        name="repack_w",
    )(w_in.astype(f32))
    wg = w_in[:, a0:a1].reshape(-1, 3, A_KV_GROUPS, A_HPG).transpose(0, 2, 1, 3).reshape(-1, A_KV_GROUPS, 3 * A_HPG)
    w_gate = jnp.pad(wg, ((0, 0), (0, 0), (0, DH - 3 * A_HPG))).reshape(-1, A_KV_GROUPS * DH).astype(bf16)
    return w_slabs, w_gate


HGRN_SAFE_LOG_DECAY = 80.0


def _hgrn_exact(q, k, v, b, st):
    sc = q.shape[0]
    t_idx = lax.broadcasted_iota(jnp.int32, (sc, DH), 0)
    o = lax.dot_general((q * jnp.exp(b)).astype(bf16), st.astype(bf16), (((1,), (1,)), ((), ())),
                        preferred_element_type=f32)
    for s in range(sc):
        e = jnp.exp(jnp.minimum(b - b[s:s + 1, :], 0.0))
        a = jnp.where(t_idx >= s, q * e * k[s:s + 1, :], 0.0)
        o = o + jnp.sum(a, axis=-1, keepdims=True) * v[s:s + 1, :]
    bl = b[sc - 1:sc, :]
    ut = lax.dot_general(v.astype(bf16), (k * jnp.exp(bl - b)).astype(bf16), (((0,), (0,)), ((), ())),
                         preferred_element_type=f32)
    return o, st * jnp.exp(bl) + ut


def _hgrn_factored(q, k, v, b, st):
    fc = q.shape[0]
    qd = (q * jnp.exp(b)).astype(bf16)
    att = lax.dot_general(qd, (k * jnp.exp(-b)).astype(bf16), (((1,), (1,)), ((), ())), preferred_element_type=f32)
    tri = lax.broadcasted_iota(jnp.int32, (fc, fc), 0) >= lax.broadcasted_iota(jnp.int32, (fc, fc), 1)
    att = jnp.where(tri, att, 0.0)
    o = jnp.dot(att.astype(bf16), v.astype(bf16), preferred_element_type=f32)
    o = o + lax.dot_general(qd, st.astype(bf16), (((1,), (1,)), ((), ())), preferred_element_type=f32)
    bl = b[fc - 1:fc, :]
    ut = lax.dot_general(v.astype(bf16), (k * jnp.exp(bl - b)).astype(bf16), (((0,), (0,)), ((), ())),
                         preferred_element_type=f32)
    return o, st * jnp.exp(bl) + ut


def _hgrn_kernel(q_ref, f_ref, v_ref, zs_ref, s0_ref, go_ref, tri_ref, o_ref, sout_ref, st_scr, *, sc, fc):
    c = pl.program_id(1)
    tc = q_ref.shape[0]
    ns = st_scr.shape[0]
    t_seq = tc // ns

    @pl.when(c == 0)
    def _():
        for j in range(ns):
            for h in range(B_HEADS):
                st_scr[j, h] = s0_ref[j, h].T

    f = f_ref[...]
    b_all = _dot_small_int_lhs(tri_ref[...], jnp.log(f), (((1,), (0,)), ((), ())))
    go = go_ref[...]

    def finish(rows, lanes, o):
        o_ref[rows, lanes] = (_head_norm(o, go) * zs_ref[rows, lanes]).astype(o_ref.dtype)

    def exact_rows(r0, n_rows, rebase):
        for h in range(B_HEADS):
            lanes = slice(h * DH, (h + 1) * DH)
            outs = []
            for i in range(n_rows // sc):
                rows = slice(r0 + i * sc, r0 + (i + 1) * sc)
                b = b_all[rows, lanes]
                if rebase and i > 0:
                    b = b - b_all[r0 + i * sc - 1:r0 + i * sc, lanes]
                j = (r0 + i * sc) // t_seq
                o, st = _hgrn_exact(q_ref[rows, lanes], 1.0 - f[rows, lanes], v_ref[rows, lanes], b, st_scr[j, h])
                st_scr[j, h] = st
                outs.append(o)
            finish(slice(r0, r0 + n_rows), lanes, jnp.concatenate(outs, axis=0) if len(outs) > 1 else outs[0])

    if fc is None:
        exact_rows(0, tc, False)
    else:
        for i in range(tc // fc):
            r0 = i * fc
            rows = slice(r0, r0 + fc)
            total = jnp.max(-b_all[r0 + fc - 1:r0 + fc, :])

            @pl.when(total < HGRN_SAFE_LOG_DECAY)
            def _():
                for h in range(B_HEADS):
                    lanes = slice(h * DH, (h + 1) * DH)
                    o, st = _hgrn_factored(q_ref[rows, lanes], 1.0 - f[rows, lanes], v_ref[rows, lanes],
                                           b_all[rows, lanes], st_scr[r0 // t_seq, h])
                    st_scr[r0 // t_seq, h] = st
                    finish(rows, lanes, o)

            @pl.when(jnp.logical_not(total < HGRN_SAFE_LOG_DECAY))
            def _():
                exact_rows(r0, fc, True)

    @pl.when(c == pl.num_programs(1) - 1)
    def _():
        for j in range(ns):
            for h in range(B_HEADS):
                sout_ref[j, h] = st_scr[j, h].T


def _block_tri(tc, blk):
    r = np.arange(tc)
    tri = (r[:, None] // blk == r[None, :] // blk) & (r[None, :] <= r[:, None])
    return jnp.asarray(tri.astype(np.float32), dtype=bf16)


def _hgrn(qb, f, ib, zs, s0, go, n, t, tc, sc, fc=None, ns=1):
    assert tc % sc == 0 and (fc is None or (tc % fc == 0 and fc % sc == 0))
    assert (ns == 1 and t % tc == 0) or (tc == ns * t and n % ns == 0 and fc is None and t % sc == 0)
    nc = max(t // tc, 1)
    rows = lambda i, c: (i * nc + c, 0)
    st = lambda i, c: (i, 0, 0, 0)
    const = lambda i, c: (0, 0)
    return pl.pallas_call(
        functools.partial(_hgrn_kernel, sc=sc, fc=fc),
        grid=(n // ns, nc),
        in_specs=[pl.BlockSpec((tc, B_WIDTH), rows)] * 4 + [
            pl.BlockSpec((ns, B_HEADS, DH, DH), st),
            pl.BlockSpec((1, DH), const),
            pl.BlockSpec((tc, tc), const),
        ],
        out_specs=[pl.BlockSpec((tc, B_WIDTH), rows), pl.BlockSpec((ns, B_HEADS, DH, DH), st)],
        out_shape=[jax.ShapeDtypeStruct((n * t, B_WIDTH), bf16), jax.ShapeDtypeStruct((n, B_HEADS, DH, DH), f32)],
        scratch_shapes=[pltpu.VMEM((ns, B_HEADS, DH, DH), f32)],
        compiler_params=pltpu.CompilerParams(
            dimension_semantics=("arbitrary", "arbitrary"), vmem_limit_bytes=VMEM_LIMIT),
        name="hgrn",
    )(qb, f, ib, zs, s0, go, _block_tri(tc, sc if fc is None else fc))


def _rel_bucket_np(dist):
    n = np.maximum(dist, 0)
    exact = REL_BUCKETS // 2
    scale = np.float32((REL_BUCKETS - exact) / math.log(REL_MAX_DIST / exact))
    large = exact + (np.log(np.maximum(n, exact).astype(np.float32) / np.float32(exact)) * scale).astype(np.int32)
    return np.where(n < exact, n, np.minimum(large, REL_BUCKETS - 1)).astype(np.int32)


def _bucket_starts():
    buckets = _rel_bucket_np(np.arange(2 * REL_MAX_DIST))
    assert np.all(np.diff(buckets) >= 0) and buckets[-1] == REL_BUCKETS - 1
    return [int(np.argmax(buckets >= b)) for b in range(REL_BUCKETS)]


def _bias_table_kernel(rb_ref, o_ref, *, off, a0, a1, sub_far):
    h = pl.program_id(0)
    shape = o_ref.shape[1:]
    d = off + a0 * lax.broadcasted_iota(jnp.int32, shape, 0) + a1 * lax.broadcasted_iota(jnp.int32, shape, 1)
    starts = _bucket_starts()
    far = rb_ref[REL_BUCKETS - 1, h]
    v = jnp.full(shape, far, f32)
    for b in range(REL_BUCKETS - 2, -1, -1):
        v = jnp.where(d < starts[b + 1], rb_ref[b, h], v)
    o_ref[0] = v - far if sub_far else v


def _bias_table(rel_bias, shape, off, a0, a1, sub_far):
    return pl.pallas_call(
        functools.partial(_bias_table_kernel, off=off, a0=a0, a1=a1, sub_far=sub_far),
        grid=(A_HEADS,),
        in_specs=[pl.BlockSpec(memory_space=pltpu.SMEM)],
        out_specs=pl.BlockSpec((1,) + tuple(shape), lambda h: (h, 0, 0)),
        out_shape=jax.ShapeDtypeStruct((A_HEADS,) + tuple(shape), f32),
        compiler_params=pltpu.CompilerParams(dimension_semantics=("arbitrary",)),
        name="bias_table",
    )(rel_bias.astype(f32))


def _cmp_bias_t(rel_bias, t0, nt, n_chunk):
    return _bias_table(rel_bias, (n_chunk, nt), t0 - (CMP_BLOCK - 1), -CMP_STRIDE, 1, False)


def _near_tiles_t(rel_bias):
    return _bias_table(rel_bias, (QB, 2 * QB), 0, -1, 1, True)


def _compress_combine(lhs, wcat, pe2):
    out = jnp.dot(lhs, wcat, preferred_element_type=f32)
    pc = jnp.dot(pe2, wcat, preferred_element_type=f32)
    const = pc[0:1, :DH] + pc[1:2, DH:]
    r = lhs.shape[0]
    return out[:, :DH] + pltpu.roll(out[:, DH:], r - 1, 0) + const


def _compress_kernel(kv_ref, wk_ref, wv_ref, pek_ref, pev_ref, gk_ref, kc_ref, vct_ref):
    nchunk = kv_ref.shape[0] // (4 * CMP_STRIDE)

    def chunk_rows(cg):
        return jnp.concatenate(
            [kv_ref[pl.ds(4 * l + cg, nchunk, stride=4 * CMP_STRIDE), :].astype(bf16) for l in range(CMP_STRIDE)],
            axis=1)

    kc = _compress_combine(jnp.concatenate([chunk_rows(0), chunk_rows(1)], axis=0), wk_ref[...], pek_ref[...])
    vc = _compress_combine(jnp.concatenate([chunk_rows(2), chunk_rows(3)], axis=0), wv_ref[...], pev_ref[...])
    kc = _head_norm(kc, gk_ref[...])
    for g in range(A_KV_GROUPS):
        kc_ref[0, g] = kc[g * nchunk:(g + 1) * nchunk].astype(kc_ref.dtype)
        vct_ref[0, g] = vc[g * nchunk:(g + 1) * nchunk].T.astype(vct_ref.dtype)


def _prep_cmp_weights(w, pe):
    half = CMP_STRIDE * DH
    wcat = jnp.concatenate([w[:half], w[half:]], axis=1).astype(bf16)
    return wcat, pe.reshape(2, half).astype(bf16)


def _compress(kvc, wk, wv, pek, pev, gk, n, t):
    nchunk = t // CMP_STRIDE
    assert nchunk == DH, "the transposed v_c block is square"
    const = lambda i: (0, 0)
    return pl.pallas_call(
        _compress_kernel, grid=(n,),
        in_specs=[pl.BlockSpec((t * 4, DH), lambda i: (i, 0)),
                  pl.BlockSpec(wk.shape, const), pl.BlockSpec(wv.shape, const),
                  pl.BlockSpec(pek.shape, const), pl.BlockSpec(pev.shape, const), pl.BlockSpec((1, DH), const)],
        out_specs=[pl.BlockSpec((1, A_KV_GROUPS, nchunk, DH), lambda i: (i, 0, 0, 0))] * 2,
        out_shape=[jax.ShapeDtypeStruct((n, A_KV_GROUPS, nchunk, DH), bf16)] * 2,
        compiler_params=pltpu.CompilerParams(dimension_semantics=("arbitrary",), vmem_limit_bytes=VMEM_LIMIT),
        name="compress",
    )(kvc, wk, wv, pek, pev, gk)


def _softmax_tile(carry, s, mask, v_bf):
    m, l, acc = carry
    s = jnp.where(mask, s, NEG)
    m_new = jnp.maximum(m, jnp.max(s, axis=-1, keepdims=True))
    alpha = jnp.exp(m - m_new)
    p = jnp.where(mask, jnp.exp(s - m_new), 0.0)
    l = alpha * l + jnp.sum(p, axis=-1, keepdims=True)
    h, r, k = p.shape
    pv = jnp.dot(p.reshape(h * r, k).astype(bf16), v_bf, preferred_element_type=f32).reshape(h, r, DH)
    return m_new, l, alpha * acc + pv


def _softmax_finish(carry):
    m, l, acc = carry
    return acc / jnp.where(l > 0, l, 1.0)


def _select_blocks(score_t, tpos, n_blocks):
    nb = score_t.shape[0]
    j = lax.broadcasted_iota(jnp.int32, score_t.shape, 0)
    valid = (j * SLC_BLOCK <= tpos) & (j < n_blocks)
    cur = tpos >> 6
    forced = (j == 0) | (j == cur) | (j == cur - 1)
    val = jnp.where(valid, jnp.where(forced, 1e30, score_t), -1.0)
    rank = jnp.zeros(score_t.shape, jnp.int32)
    for i in range(n_blocks):
        vi = val[i:i + 1, :]
        beats = (vi > val) | ((vi == val) & (i < j))
        rank = rank + beats.astype(jnp.int32)
    return jnp.where((rank < N_SELECT) & valid, 1.0, 0.0)


QB = 128
NSA_PROMPT_SEQS = 2


NEG_M = -1e30
NEG_S = -2e30


def _nsa_prompt_kernel(q_ref, kc_ref, vct_ref, bc_ref, kvs_ref, kvw_ref, gate_ref, za_ref, bt_ref, mt_ref,
                       o_ref, ks_scr, vts_scr, kw_scr, vtw_scr, km_scr):
    qb = pl.program_id(1)
    q0 = qb * QB
    n_far = jnp.maximum(qb - 1, 0)
    nkt = vts_scr.shape[1]
    nq4 = A_HPG * QB
    nseq = q_ref.shape[0]
    chains = [(s, g) for s in range(nseq) for g in range(A_KV_GROUPS)]
    groups = range(len(chains))

    @pl.when(qb == 0)
    def _():
        for c, (s, g) in enumerate(chains):
            for kt in range(nkt):
                rows = slice(kt * QB, (kt + 1) * QB)
                base = kt * 4 * QB
                ks_scr[c, rows, :] = kvs_ref[s, pl.ds(base + g, QB, stride=4), :].astype(bf16)
                vts_scr[c, kt] = kvs_ref[s, pl.ds(base + 2 + g, QB, stride=4), :].T.astype(bf16)
                kw_scr[c, rows, :] = kvw_ref[s, pl.ds(base + g, QB, stride=4), :].astype(bf16)
                vtw_scr[c, kt] = kvw_ref[s, pl.ds(base + 2 + g, QB, stride=4), :].T.astype(bf16)

    qts = [jnp.concatenate([q_ref[s, :, (g * A_HPG + h) * DH:(g * A_HPG + h + 1) * DH].astype(f32).T
                            for h in range(A_HPG)], axis=1).astype(bf16) for s, g in chains]
    kk = lax.broadcasted_iota(jnp.int32, (QB, nq4), 0)
    tt = lax.broadcasted_iota(jnp.int32, (QB, nq4), 1) & (QB - 1)

    def heads(ref, g, *idx):
        return jnp.concatenate([ref[(g * A_HPG + h,) + idx] for h in range(A_HPG)], axis=1)

    o_c = []
    for c, (s, g) in enumerate(chains):
        sc = jnp.dot(kc_ref[s, g], qts[c], preferred_element_type=f32) + heads(bc_ref, g)
        mask_c = q0 + tt >= CMP_STRIDE * kk + (CMP_BLOCK - 1)
        sc = jnp.where(mask_c, sc, NEG)
        mc = jnp.max(sc, axis=0, keepdims=True)
        ec = jnp.where(mask_c, jnp.exp(sc - mc), 0.0)
        lc = jnp.sum(ec, axis=0, keepdims=True)
        pc = ec / jnp.where(lc > 0, lc, 1.0)
        o_c.append(jnp.dot(vct_ref[s, g], pc.astype(bf16), preferred_element_type=f32))
        ps = pc[:, 0:QB] + pc[:, QB:2 * QB] + pc[:, 2 * QB:3 * QB] + pc[:, 3 * QB:4 * QB]
        score_t = _dot_small_int_lhs(mt_ref[...], ps, (((1,), (0,)), ((), ())))
        nb = score_t.shape[0]
        sel_t = _select_blocks(score_t, q0 + lax.broadcasted_iota(jnp.int32, (nb, QB), 1), nb)
        for j in range(nb):
            km_scr[c, j * SLC_BLOCK:(j + 1) * SLC_BLOCK, :] = jnp.broadcast_to(sel_t[j:j + 1, :], (SLC_BLOCK, QB))

    def tile(carry, qt, k_tile, vt_tile, mask, bias):
        m, l, acc = carry
        s = jnp.dot(k_tile, qt, preferred_element_type=f32)
        if bias is not None:
            s = s + bias
        s = jnp.where(mask, s, NEG_S)
        m_new = jnp.maximum(m, jnp.max(s, axis=0, keepdims=True))
        alpha = jnp.exp(m - m_new)
        p = jnp.exp(s - m_new)
        l = alpha * l + jnp.sum(p, axis=0, keepdims=True)
        return m_new, l, alpha * acc + jnp.dot(vt_tile, p.astype(bf16), preferred_element_type=f32)

    def init():
        return jnp.full((1, nq4), NEG_M, f32), jnp.zeros((1, nq4), f32), jnp.zeros((DH, nq4), f32)

    def finish(carry):
        m, l, acc = carry
        return acc / jnp.where(l > 0, l, 1.0)

    def key_rows(kt):
        return pl.ds(pl.multiple_of(kt * QB, QB), QB)

    def lanes4(x):
        return jnp.concatenate([x] * A_HPG, axis=1)

    def near_span(g, k_scr, vt_scr, rs, use_sel):
        ks, vts, masks, biases = [], [], [], []
        for r in rs:
            kt = jnp.maximum(qb - r, 0)
            ks.append(k_scr[g, key_rows(kt), :])
            vts.append(vt_scr[g, kt])
            if use_sel:
                mask = lanes4(km_scr[g, key_rows(kt), :]) > 0.5
                if r == 0:
                    mask = mask & (kk <= tt)
            else:
                mask = (kk <= tt) if r == 0 else (kk >= tt) if r == WINDOW // QB else (kk >= 0)
            masks.append(mask)
            off = jnp.where(qb - r >= 0, 0.0, NEG_S)
            biases.append(heads(bt_ref, chains[g][1], slice(None), slice(r * QB, (r + 1) * QB)) + off if r < 2
                          else jnp.full((QB, nq4), off, f32))
        cat = lambda xs, axis: jnp.concatenate(xs, axis=axis)
        return cat(ks, 0), cat(vts, 1), cat(masks, 0), cat(biases, 0)

    o_w = [finish(tile(init(), qts[g], *near_span(g, kw_scr, vtw_scr, (4, 3, 2, 1, 0), False))) for g in groups]

    kk2 = lax.broadcasted_iota(jnp.int32, (2 * QB, nq4), 0)

    def far(i, carries):
        rows = pl.ds(pl.multiple_of(i * 2 * QB, 2 * QB), 2 * QB)
        in_range = (kk2 + i * 2 * QB) < n_far * QB
        out = []
        for g in groups:
            mask = (lanes4(km_scr[g, rows, :]) > 0.5) & in_range
            vt = jnp.concatenate([vts_scr[g, 2 * i], vts_scr[g, jnp.minimum(2 * i + 1, nkt - 1)]], axis=1)
            out.append(tile(carries[g], qts[g], ks_scr[g, rows, :], vt, mask, None))
        return tuple(out)

    carries = lax.fori_loop(0, (n_far + 1) // 2, far, tuple(init() for g in groups))
    o_s = [finish(tile(carries[g], qts[g], *near_span(g, ks_scr, vts_scr, (1, 0), True))) for g in groups]

    for s in range(nseq):
        za = za_ref[s]
        outs = []
        for g in range(A_KV_GROUPS):
            c = s * A_KV_GROUPS + g
            gate_t = gate_ref[s, :, g * DH:(g + 1) * DH].T
            for h in range(A_HPG):
                cols = slice(h * QB, (h + 1) * QB)

                def grow(br):
                    return gate_t[br * A_HPG + h:br * A_HPG + h + 1, :]
                o = grow(0) * o_c[c][:, cols] + grow(1) * o_s[c][:, cols] + grow(2) * o_w[c][:, cols]
                outs.append(o.T * za[:, (g * A_HPG + h) * DH:(g * A_HPG + h + 1) * DH])
        o_ref[s] = jnp.concatenate(outs, axis=1).astype(o_ref.dtype)


def _cmp_to_slc_t(n_cmp_pad, n_slc_pad, n_cmp, n_slc):
    c0 = np.arange(n_cmp_pad)[None, :] * CMP_STRIDE
    s0 = np.arange(n_slc_pad)[:, None] * SLC_BLOCK
    ov = np.minimum(c0 + CMP_BLOCK, s0 + SLC_BLOCK) - np.maximum(c0, s0)
    m = np.maximum(ov, 0).astype(np.float32) / CMP_STRIDE
    m[:, n_cmp:] = 0
    m[n_slc:, :] = 0
    return m


def _nsa_prompt(qa, kc, vct, kvs, kvw, gate, za, rel_bias, n, t):
    nq = t // QB
    n_cmp = (t - CMP_BLOCK) // CMP_STRIDE + 1
    n_slc = -(-t // SLC_BLOCK)
    assert kc.shape[2] == QB and n_slc % 8 == 0
    bias_c = _cmp_bias_t(rel_bias, 0, t, QB)
    bt = _near_tiles_t(rel_bias)
    mt = jnp.asarray(_cmp_to_slc_t(QB, n_slc, n_cmp, n_slc), dtype=bf16)
    nseq = NSA_PROMPT_SEQS
    assert n % nseq == 0
    ng = A_KV_GROUPS
    nchain = nseq * ng
    qrow = lambda w: pl.BlockSpec((nseq, QB, w), lambda i, b: (i, b, 0))
    seq4 = lambda shape: pl.BlockSpec((nseq,) + shape, lambda i, b: (i, 0, 0, 0))
    seq3 = lambda shape: pl.BlockSpec((nseq,) + shape, lambda i, b: (i, 0, 0), pipeline_mode=pl.Buffered(1))
    by_seq = lambda a, w: a.reshape(n, -1, w)
    out = pl.pallas_call(
        _nsa_prompt_kernel, grid=(n // nseq, nq),
        in_specs=[
            qrow(A_WIDTH),
            seq4((ng, QB, DH)),
            seq4((ng, DH, QB)),
            pl.BlockSpec((A_HEADS, QB, QB), lambda i, b: (0, 0, b)),
            seq3((t * 4, DH)),
            seq3((t * 4, DH)),
            qrow(ng * DH),
            qrow(A_WIDTH),
            pl.BlockSpec((A_HEADS, QB, 2 * QB), lambda i, b: (0, 0, 0)),
            pl.BlockSpec(mt.shape, lambda i, b: (0, 0)),
        ],
        out_specs=qrow(A_WIDTH),
        out_shape=jax.ShapeDtypeStruct((n, t, A_WIDTH), bf16),
        scratch_shapes=[pltpu.VMEM((nchain, t, DH), bf16), pltpu.VMEM((nchain, nq, DH, QB), bf16),
                        pltpu.VMEM((nchain, t, DH), bf16), pltpu.VMEM((nchain, nq, DH, QB), bf16),
                        pltpu.VMEM((nchain, t, QB), f32)],
        compiler_params=pltpu.CompilerParams(
            dimension_semantics=("arbitrary", "arbitrary"), vmem_limit_bytes=VMEM_LIMIT),
        name="nsa_prompt",
    )(by_seq(qa, A_WIDTH), kc, vct, bias_c, by_seq(kvs, DH), by_seq(kvw, DH), by_seq(gate, ng * DH),
      by_seq(za, A_WIDTH), bt, mt)
    return out.reshape(n * t, A_WIDTH)


CMP_PITCH = 24


def _nsa_sample_kernel(pt_ref, *refs, n_pages, page, t_new, n_blocks):
    del pt_ref
    cmp_pages = refs[:n_pages]
    slc_pages = refs[n_pages:2 * n_pages]
    (q_ref, kvs_new_ref, kvw_new_ref, win_ref, wk_ref, wv_ref, pek_ref, pev_ref, gk_ref, bc_ref, bl_ref, bn_ref,
     gate_ref, za_ref, mt_ref, e_ref, o_ref, win_out_ref, xc_scr) = refs[2 * n_pages:]
    past = n_pages * page
    nchunk = past // CMP_STRIDE
    pad_new = QB
    ng = A_KV_GROUPS
    groups = range(ng)

    cpp = page // CMP_STRIDE
    for p in range(n_pages):
        for cg in range(4):
            x = cmp_pages[p][0, pl.ds(cg, page, stride=4), :]
            for c in range(cpp):
                r0 = (p * cpp + c) * CMP_PITCH
                xc_scr[cg, r0:r0 + CMP_STRIDE, :] = x[c * CMP_STRIDE:(c + 1) * CMP_STRIDE]

    def chunk_rows(cg):
        return jnp.concatenate(
            [xc_scr[cg, pl.ds(l, nchunk, stride=CMP_PITCH), :].astype(bf16) for l in range(CMP_STRIDE)], axis=1)

    kc = _compress_combine(jnp.concatenate([chunk_rows(0), chunk_rows(1)], axis=0), wk_ref[...], pek_ref[...])
    vc = _compress_combine(jnp.concatenate([chunk_rows(2), chunk_rows(3)], axis=0), wv_ref[...], pev_ref[...])
    kc = _head_norm(kc, gk_ref[...]).astype(bf16)
    vc = vc.astype(bf16)

    qall = q_ref[...]
    qs = [jnp.concatenate([qall[:, (g * A_HPG + h) * DH:(g * A_HPG + h + 1) * DH] for h in range(A_HPG)], axis=0)
          for g in groups]
    zeros_pad = jnp.zeros((pad_new - t_new, DH), bf16)

    def new_rows(ref, c):
        return jnp.concatenate([ref[pl.ds(c, t_new, stride=4), :].astype(bf16), zeros_pad], axis=0)

    def logits(ks):
        return jnp.stack([lax.dot_general(qs[g], ks[g], (((1,), (1,)), ((), ())), preferred_element_type=f32)
                          .reshape(A_HPG, t_new, ks[g].shape[0]) for g in groups], axis=0)

    def softmax_many(items):
        kmax = max(s.shape[-1] for s, _ in items)
        padded = []
        for s, m in items:
            s = jnp.where(m, s, NEG)
            if s.shape[-1] < kmax:
                s = jnp.concatenate([s, jnp.full(s.shape[:-1] + (kmax - s.shape[-1],), NEG, f32)], axis=-1)
            padded.append(s)
        s_all = jnp.stack(padded, axis=0)
        valid = s_all > 0.5 * NEG
        e = jnp.where(valid, jnp.exp(s_all - jnp.max(s_all, axis=-1, keepdims=True)), 0.0)
        l = jnp.sum(e, axis=-1, keepdims=True)
        p = e / jnp.where(l > 0, l, 1.0)
        return [p[i][..., :s.shape[-1]] for i, (s, _) in enumerate(items)]

    def pv(p, vs):
        return jnp.stack([jnp.dot(p[g].reshape(A_HPG * t_new, -1).astype(bf16), vs[g], preferred_element_type=f32)
                          .reshape(A_HPG, t_new, DH) for g in groups], axis=0)

    def tail_bias(s):
        bias = jnp.concatenate([bl_ref[...], bn_ref[...]], axis=-1).reshape(ng, A_HPG, t_new, 2 * QB)
        nk = s.shape[-1]
        return jnp.concatenate([s[..., :nk - 2 * QB], s[..., nk - 2 * QB:] + bias], axis=-1)

    tt = lax.broadcasted_iota(jnp.int32, (t_new, pad_new), 0)
    uu = lax.broadcasted_iota(jnp.int32, (t_new, pad_new), 1)
    new_mask = uu <= tt

    ti = lax.broadcasted_iota(jnp.int32, (t_new, nchunk), 0)
    ci = lax.broadcasted_iota(jnp.int32, (t_new, nchunk), 1)
    cmask = (past + ti >= CMP_STRIDE * ci + (CMP_BLOCK - 1))[None, None]
    sc = logits([kc[g * nchunk:(g + 1) * nchunk] for g in groups]) + bc_ref[...].reshape(ng, A_HPG, t_new, nchunk)
    nw = win_ref.shape[1] // 4
    kw = [jnp.concatenate([win_ref[0, pl.ds(g, nw, stride=4), :].astype(bf16), new_rows(kvw_new_ref, g)], axis=0)
          for g in groups]
    vw = [jnp.concatenate([win_ref[0, pl.ds(2 + g, nw, stride=4), :].astype(bf16), new_rows(kvw_new_ref, 2 + g)], axis=0)
          for g in groups]
    wmask = lax.broadcasted_iota(jnp.int32, (t_new, nw), 1) >= lax.broadcasted_iota(jnp.int32, (t_new, nw), 0)
    pc, pw = softmax_many([(sc, cmask),
                           (tail_bias(logits(kw)), jnp.concatenate([wmask, new_mask], axis=1)[None, None])])
    o_c = pv(pc, [vc[g * nchunk:(g + 1) * nchunk] for g in groups])
    o_w = pv(pw, vw)

    ps = jnp.concatenate([pc[g, 0] + pc[g, 1] + pc[g, 2] + pc[g, 3] for g in groups], axis=0)
    score_t = _dot_small_int_lhs(mt_ref[...], ps, (((1,), (1,)), ((), ())))
    lane = lax.broadcasted_iota(jnp.int32, score_t.shape, 1)
    tpos = past + jnp.where(lane >= t_new, lane - t_new, lane)
    sel_t = _select_blocks(score_t, tpos, n_blocks)
    key_mask = lax.dot_general(sel_t.astype(bf16), e_ref[...], (((0,), (0,)), ((), ())),
                               preferred_element_type=f32)

    new_mask_f = jnp.where(new_mask, 1.0, 0.0)
    sel_mask = jnp.concatenate([key_mask, jnp.concatenate([new_mask_f] * ng, axis=0)], axis=1)
    sel_mask = sel_mask.reshape(ng, 1, t_new, past + pad_new) > 0.5
    ks = [jnp.concatenate([slc_pages[p][0, pl.ds(g, page, stride=4), :].astype(bf16) for p in range(n_pages)]
                          + [new_rows(kvs_new_ref, g)], axis=0) for g in groups]
    vs = [jnp.concatenate([slc_pages[p][0, pl.ds(2 + g, page, stride=4), :].astype(bf16) for p in range(n_pages)]
                          + [new_rows(kvs_new_ref, 2 + g)], axis=0) for g in groups]
    o_s = pv(softmax_many([(tail_bias(logits(ks)), sel_mask)])[0], vs)

    gate_all = gate_ref[...]
    za = za_ref[...]
    outs = []
    for g in groups:
        gate = gate_all[:, g * DH:(g + 1) * DH]
        for h in range(A_HPG):
            def gcol(br):
                return gate[:, br * A_HPG + h:br * A_HPG + h + 1]
            o = gcol(0) * o_c[g, h] + gcol(1) * o_s[g, h] + gcol(2) * o_w[g, h]
            outs.append(o * za[:, (g * A_HPG + h) * DH:(g * A_HPG + h + 1) * DH])
    o_ref[...] = jnp.concatenate(outs, axis=1).astype(o_ref.dtype)

    nrow = win_ref.shape[1]
    win_out_ref[0, 0:nrow - 4 * t_new, :] = win_ref[0, 4 * t_new:nrow, :]
    win_out_ref[0, nrow - 4 * t_new:nrow, :] = kvw_new_ref[...]


def _nsa_sample(qa, kvs_new, kvw_new, cache_cmp, cache_slc, cache_win, page_table, wk, wv, pek, pev, gk,
                gate, za, rel_bias, t_new):
    n, n_pages = page_table.shape
    page = cache_cmp.shape[1] // 4
    past = n_pages * page
    wb = cache_win.shape[1] // 4
    assert wb == WINDOW and past % QB == 0 and past >= WINDOW
    nchunk = past // CMP_STRIDE
    n_cmp = (past + t_new - CMP_BLOCK) // CMP_STRIDE + 1
    assert n_cmp < nchunk
    n_slc = -(-(past + t_new) // SLC_BLOCK)
    nb_pad = -(-n_slc // 8) * 8
    bias_c = _bias_table(rel_bias, (t_new, nchunk), past - (CMP_BLOCK - 1), 1, -CMP_STRIDE, False)
    b_last = _bias_table(rel_bias, (t_new, QB), QB, 1, -1, True)
    b_new = _bias_table(rel_bias, (t_new, QB), 0, 1, -1, True)
    mt = jnp.asarray(_cmp_to_slc_t(nchunk, nb_pad, n_cmp, n_slc), dtype=bf16)
    e = jnp.asarray((np.arange(past)[None, :] // SLC_BLOCK == np.arange(nb_pad)[:, None]).astype(np.float32), dtype=bf16)

    def page_spec(p):
        return pl.BlockSpec((1, page * 4, DH), lambda i, pt: (pt[i, p], 0, 0))

    rowblk = lambda w: pl.BlockSpec((t_new, w), lambda i, pt: (i, 0))
    full = lambda a: pl.BlockSpec(a.shape, lambda i, pt: (0,) * a.ndim)
    in_specs = ([page_spec(p) for p in range(n_pages)] * 2 + [
        rowblk(A_WIDTH),
        pl.BlockSpec((t_new * 4, DH), lambda i, pt: (i, 0)),
        pl.BlockSpec((t_new * 4, DH), lambda i, pt: (i, 0)),
        pl.BlockSpec((1, wb * 4, DH), lambda i, pt: (i, 0, 0)),
        full(wk), full(wv), full(pek), full(pev), full(gk), full(bias_c), full(b_last), full(b_new),
        rowblk(A_KV_GROUPS * DH), rowblk(A_WIDTH), full(mt), full(e)])
    grid_spec = pltpu.PrefetchScalarGridSpec(
        num_scalar_prefetch=1, grid=(n,), in_specs=in_specs,
        out_specs=[rowblk(A_WIDTH), pl.BlockSpec((1, wb * 4, DH), lambda i, pt: (i, 0, 0))],
        scratch_shapes=[pltpu.VMEM((4, nchunk * CMP_PITCH, DH), f32)])
    return pl.pallas_call(
        functools.partial(_nsa_sample_kernel, n_pages=n_pages, page=page, t_new=t_new, n_blocks=n_slc),
        grid_spec=grid_spec,
        out_shape=[jax.ShapeDtypeStruct((n * t_new, A_WIDTH), f32), jax.ShapeDtypeStruct(cache_win.shape, f32)],
        compiler_params=pltpu.CompilerParams(dimension_semantics=("arbitrary",), vmem_limit_bytes=VMEM_LIMIT),
        name="nsa_sample",
    )(page_table, *([cache_cmp] * n_pages), *([cache_slc] * n_pages), qa, kvs_new, kvw_new, cache_win,
      wk, wv, pek, pev, gk, bias_c, b_last, b_new, gate, za, mt, e)


def _outproj_kernel(x_ref, oa_ref, ob_ref, w_ref, y_ref):
    acc = jnp.dot(oa_ref[...].astype(bf16), w_ref[0:A_WIDTH, :], preferred_element_type=f32)
    acc = acc + jnp.dot(ob_ref[...].astype(bf16), w_ref[A_WIDTH:, :], preferred_element_type=f32)
    y_ref[...] = x_ref[...] + acc


def _outproj(x2d, oa, ob, w_out_bf, tm):
    m = x2d.shape[0]
    assert m % tm == 0
    row = lambda i: (i, 0)
    return pl.pallas_call(
        _outproj_kernel, grid=(m // tm,),
        in_specs=[pl.BlockSpec((tm, D_MODEL), row), pl.BlockSpec((tm, A_WIDTH), row), pl.BlockSpec((tm, B_WIDTH), row),
                  pl.BlockSpec(w_out_bf.shape, lambda i: (0, 0))],
        out_specs=pl.BlockSpec((tm, D_MODEL), row),
        out_shape=jax.ShapeDtypeStruct((m, D_MODEL), f32),
        compiler_params=pltpu.CompilerParams(dimension_semantics=("arbitrary",), vmem_limit_bytes=VMEM_LIMIT),
        name="outproj",
    )(x2d, oa, ob, w_out_bf)


def kernel(x_prompt, x_sample, cache_kv_cmp, cache_kv_slc, cache_kv_win, state_hgrn, page_table, g_norm, w_in, w_out,
           g_q, g_k_slc, g_k_win, g_k_cmp, w_cmp_k, w_cmp_v, pe_cmp_k, pe_cmp_v, rel_bias, lb_logits, g_o_hgrn):
    depth = w_in.shape[0]
    assert depth == 1, "single-layer trunk"
    nb, t, _ = x_prompt.shape
    ns, ts, _ = x_sample.shape
    row = lambda a: a.astype(f32)[None]
    lower = jnp.cumsum(jax.nn.softmax(lb_logits.astype(f32), axis=0), axis=0)[0]
    w_main, w_gate = _prep_proj_weights(w_in[0])
    w_out_bf = w_out[0].astype(bf16)
    wk, pek = _prep_cmp_weights(w_cmp_k[0], pe_cmp_k[0])
    wv, pev = _prep_cmp_weights(w_cmp_v[0], pe_cmp_v[0])
    proj_args = (row(g_norm[0]), w_main, w_gate, row(g_q[0]), row(g_k_slc[0]), row(g_k_win[0]), row(lower))
    kv6 = lambda a, n_, t_: a.reshape(1, n_, t_, 2, A_KV_GROUPS, DH)

    xp = x_prompt.reshape(nb * t, D_MODEL)
    wlen = min(WINDOW, t)
    assert wlen == PROJ_TM, "the proj row tile doubles as the prompt's final window"
    qa, kvc, kvs, kvw, gate, za, qb, f, ib, zb, win_p = _proj(xp, *proj_args, tm=PROJ_TM, t_seq=t)
    s0 = jnp.zeros((nb, B_HEADS, DH, DH), f32)
    ob, st_p = _hgrn(qb, f, ib, zb, s0, row(g_o_hgrn[0]), nb, t, tc=128, sc=16, fc=64)
    kc, vct = _compress(kvc, wk, wv, pek, pev, row(g_k_cmp[0]), nb, t)
    oa = _nsa_prompt(qa, kc, vct, kvs, kvw, gate, za, rel_bias, nb, t)
    y_p = _outproj(xp, oa, ob, w_out_bf, tm=PROJ_TM).reshape(nb, t, D_MODEL)
    win_p = kv6(win_p, nb, wlen)

    xs = x_sample.reshape(ns * ts, D_MODEL)
    qa, kvc_s, kvs_s, kvw_s, gate, za, qb, f, ib, zb, _ = _proj(xs, *proj_args, tm=PROJ_TM, t_seq=PROJ_TM)
    seq_per_step = 8
    ob, st_s = _hgrn(qb, f, ib, zb, state_hgrn[0].astype(f32), row(g_o_hgrn[0]), ns, ts, tc=seq_per_step * ts, sc=ts,
                     ns=seq_per_step)
    pool = cache_kv_cmp.shape[1]
    page = cache_kv_cmp.shape[2]
    oa, win_s = _nsa_sample(
        qa, kvs_s, kvw_s, cache_kv_cmp[0].reshape(pool, page * 4, DH), cache_kv_slc[0].reshape(pool, page * 4, DH),
        cache_kv_win[0].reshape(ns, -1, DH), page_table, wk, wv, pek, pev, row(g_k_cmp[0]), gate, za, rel_bias, ts)
    y_s = _outproj(xs, oa, ob, w_out_bf, tm=PROJ_TM).reshape(ns, ts, D_MODEL)

    return (y_p, y_s, kv6(kvc, nb, t), kv6(kvs, nb, t), win_p, st_p[None].astype(x_prompt.dtype),
            kv6(kvc_s, ns, ts), kv6(kvs_s, ns, ts), kv6(win_s, ns, WINDOW), st_s[None].astype(state_hgrn.dtype))
```

```python
import functools
import math

import jax
import jax.numpy as jnp
import numpy as np
from jax import lax
from jax.experimental import pallas as pl
from jax.experimental.pallas import tpu as pltpu

f32 = jnp.float32
bf16 = jnp.bfloat16

D_MODEL = 2048
A_HEADS = 8
A_KV_GROUPS = 2
A_HPG = A_HEADS // A_KV_GROUPS
DH = 128
A_WIDTH = A_HEADS * DH
CMP_BLOCK = 32
CMP_STRIDE = 16
SLC_BLOCK = 64
N_SELECT = 16
WINDOW = 512
B_HEADS = 8
B_WIDTH = B_HEADS * DH
REL_BUCKETS = 32
REL_MAX_DIST = 128
EPS = 1e-6
NEG = -1e30

VMEM_LIMIT = 56 * 1024 * 1024


def _sigmoid(x):
    return 1.0 / (1.0 + jnp.exp(-x))


def _head_norm(a, g):
    return a * lax.rsqrt(jnp.mean(a * a, axis=-1, keepdims=True) + EPS) * g


def _dot_small_int_lhs(m_bf, x, dims):
    hi = x.astype(bf16)
    r1 = x - hi.astype(f32)
    mid = r1.astype(bf16)
    lo = (r1 - mid.astype(f32)).astype(bf16)
    d = lambda p: lax.dot_general(m_bf, p, dims, preferred_element_type=f32)
    return d(hi) + d(mid) + d(lo)


PROJ_TN = 512
PROJ_TM = 512
PROJ_SEG = {0: "qa", 1: "qa", 2: "kvc", 3: "kvs", 4: "kvw", 5: "za", 6: "za", 7: "qb", 8: "qb", 9: "f", 10: "f",
            11: "ib", 12: "ib", 13: "zb", 14: "zb"}
PROJ_SLABS = ((0, 5), (5, 11), (11, 15))


def _proj_tiles(h, w_ref, first_tile, n_tiles, tm, out, par):
    for jj in range(n_tiles):
        j = first_tile + jj
        acc = lax.dot_general(h, w_ref[jj * PROJ_TN:(jj + 1) * PROJ_TN, :], (((1,), (1,)), ((), ())),
                              preferred_element_type=f32)
        name = PROJ_SEG[j]
        k = j - min(t for t, s in PROJ_SEG.items() if s == name)
        lanes = slice(k * PROJ_TN, (k + 1) * PROJ_TN)

        def cols(c):
            return acc[:, c * DH:(c + 1) * DH]

        if name == "qa":
            gq = par["gq"][...]
            out["qa"][:, lanes] = jnp.concatenate(
                [_head_norm(cols(c), gq) * (DH ** -0.5) for c in range(4)], axis=1).astype(out["qa"].dtype)
        elif name in ("kvc", "kvs", "kvw"):
            g_ref = {"kvc": None, "kvs": par["gks"], "kvw": par["gkw"]}[name]
            for c in range(4):
                val = _head_norm(cols(c), g_ref[...]) if (g_ref is not None and c < 2) else cols(c)
                out[name][pl.ds(c, tm, stride=4), :] = val
                if name == "kvw":
                    out["tail"][pl.ds(c, tm, stride=4), :] = val
        elif name in ("za", "zb"):
            out[name][:, lanes] = acc * _sigmoid(acc)
        elif name == "f":
            lb = par["lb"][:, lanes]
            out["f"][:, lanes] = lb + (1.0 - lb) * _sigmoid(acc)
        else:
            out[name][:, lanes] = acc


def _proj_first_kernel(x_ref, gn_ref, w_ref, wg_ref, gq_ref, gks_ref, gkw_ref,
                       h_ref, qa_ref, kvc_ref, kvs_ref, kvw_ref, tail_ref, gate_ref):
    x = x_ref[...]
    h = (x * lax.rsqrt(jnp.mean(x * x, axis=-1, keepdims=True) + EPS) * gn_ref[...]).astype(bf16)
    h_ref[...] = h
    gate_ref[...] = _sigmoid(lax.dot_general(h, wg_ref[...], (((1,), (1,)), ((), ())), preferred_element_type=f32))
    first, last = PROJ_SLABS[0]
    _proj_tiles(h, w_ref, first, last - first, x_ref.shape[0],
                dict(qa=qa_ref, kvc=kvc_ref, kvs=kvs_ref, kvw=kvw_ref, tail=tail_ref),
                dict(gq=gq_ref, gks=gks_ref, gkw=gkw_ref))


def _proj_rest_kernel(h_ref, w_ref, lb_ref, *out_refs, slab):
    first, last = PROJ_SLABS[slab]
    names = list(dict.fromkeys(PROJ_SEG[j] for j in range(first, last)))
    _proj_tiles(h_ref[...], w_ref, first, last - first, h_ref.shape[0], dict(zip(names, out_refs)), dict(lb=lb_ref))


def _proj(x2d, gn, w_slabs, w_gate, gq, gks, gkw, lb, tm, t_seq):
    m = x2d.shape[0]
    assert m % tm == 0 and t_seq % tm == 0
    tiles_per_seq = t_seq // tm
    row = lambda i: (i, 0)
    const = lambda i: (0, 0)
    params = pltpu.CompilerParams(dimension_semantics=("arbitrary",), vmem_limit_bytes=VMEM_LIMIT)
    kv_shape = jax.ShapeDtypeStruct((m * 4, DH), f32)
    wide = lambda dt: jax.ShapeDtypeStruct((m, A_WIDTH), dt)
    h, qa, kvc, kvs, kvw, tail, gate = pl.pallas_call(
        _proj_first_kernel, grid=(m // tm,),
        in_specs=[pl.BlockSpec((tm, D_MODEL), row), pl.BlockSpec((1, D_MODEL), const),
                  pl.BlockSpec(w_slabs[0].shape, const), pl.BlockSpec(w_gate.shape, const),
                  pl.BlockSpec((1, DH), const), pl.BlockSpec((1, DH), const), pl.BlockSpec((1, DH), const)],
        out_specs=[pl.BlockSpec((tm, D_MODEL), row), pl.BlockSpec((tm, A_WIDTH), row)]
        + [pl.BlockSpec((tm * 4, DH), row)] * 3
        + [pl.BlockSpec((tm * 4, DH), lambda i: (i // tiles_per_seq, 0)), pl.BlockSpec((tm, A_KV_GROUPS * DH), row)],
        out_shape=[jax.ShapeDtypeStruct((m, D_MODEL), bf16), wide(bf16), kv_shape, kv_shape, kv_shape,
                   jax.ShapeDtypeStruct((m // tiles_per_seq * 4, DH), f32),
                   jax.ShapeDtypeStruct((m, A_KV_GROUPS * DH), f32)],
        compiler_params=params, name="proj_a",
    )(x2d, gn, w_slabs[0], w_gate, gq, gks, gkw)

    def rest(slab, n_out):
        return pl.pallas_call(
            functools.partial(_proj_rest_kernel, slab=slab), grid=(m // tm,),
            in_specs=[pl.BlockSpec((tm, D_MODEL), row), pl.BlockSpec(w_slabs[slab].shape, const),
                      pl.BlockSpec((1, B_WIDTH), const)],
            out_specs=[pl.BlockSpec((tm, A_WIDTH), row)] * n_out, out_shape=[wide(f32)] * n_out,
            compiler_params=params, name="proj_" + "abc"[slab],
        )(h, w_slabs[slab], lb)

    za, qb, f = rest(1, 3)
    ib, zb = rest(2, 2)
    return qa, kvc, kvs, kvw, gate, za, qb, f, ib, zb, tail


def _prep_proj_weights(w_in):
    a0 = A_WIDTH + 6 * A_KV_GROUPS * DH
    a1 = a0 + 3 * A_HEADS
    w_t = w_in.T
    col0 = lambda tile: tile * PROJ_TN if tile * PROJ_TN < a0 else a1 + tile * PROJ_TN - a0
    w_slabs = tuple(w_t[col0(first):col0(first) + (last - first) * PROJ_TN].astype(bf16) for first, last in PROJ_SLABS)
    wg = w_t[a0:a1].reshape(3, A_KV_GROUPS, A_HPG, -1).transpose(1, 0, 2, 3).reshape(A_KV_GROUPS, 3 * A_HPG, -1)
    w_gate = jnp.pad(wg, ((0, 0), (0, DH - 3 * A_HPG), (0, 0))).reshape(A_KV_GROUPS * DH, -1).astype(bf16)
    return w_slabs, w_gate


HGRN_SAFE_LOG_DECAY = 80.0


def _hgrn_exact(q, k, v, b, st):
    sc = q.shape[0]
    t_idx = lax.broadcasted_iota(jnp.int32, (sc, DH), 0)
    o = lax.dot_general((q * jnp.exp(b)).astype(bf16), st.astype(bf16), (((1,), (1,)), ((), ())),
                        preferred_element_type=f32)
    for s in range(sc):
        e = jnp.exp(jnp.minimum(b - b[s:s + 1, :], 0.0))
        a = jnp.where(t_idx >= s, q * e * k[s:s + 1, :], 0.0)
        o = o + jnp.sum(a, axis=-1, keepdims=True) * v[s:s + 1, :]
    bl = b[sc - 1:sc, :]
    ut = lax.dot_general(v.astype(bf16), (k * jnp.exp(bl - b)).astype(bf16), (((0,), (0,)), ((), ())),
                         preferred_element_type=f32)
    return o, st * jnp.exp(bl) + ut


def _hgrn_factored(q, k, v, b, st):
    fc = q.shape[0]
    qd = (q * jnp.exp(b)).astype(bf16)
    att = lax.dot_general(qd, (k * jnp.exp(-b)).astype(bf16), (((1,), (1,)), ((), ())), preferred_element_type=f32)
    tri = lax.broadcasted_iota(jnp.int32, (fc, fc), 0) >= lax.broadcasted_iota(jnp.int32, (fc, fc), 1)
    att = jnp.where(tri, att, 0.0)
    o = jnp.dot(att.astype(bf16), v.astype(bf16), preferred_element_type=f32)
    o = o + lax.dot_general(qd, st.astype(bf16), (((1,), (1,)), ((), ())), preferred_element_type=f32)
    bl = b[fc - 1:fc, :]
    ut = lax.dot_general(v.astype(bf16), (k * jnp.exp(bl - b)).astype(bf16), (((0,), (0,)), ((), ())),
                         preferred_element_type=f32)
    return o, st * jnp.exp(bl) + ut


def _hgrn_kernel(q_ref, f_ref, v_ref, zs_ref, s0_ref, go_ref, tri_ref, o_ref, sout_ref, st_scr, *, sc, fc):
    c = pl.program_id(1)
    tc = q_ref.shape[0]
    ns = st_scr.shape[0]
    t_seq = tc // ns

    @pl.when(c == 0)
    def _():
        for j in range(ns):
            for h in range(B_HEADS):
                st_scr[j, h] = s0_ref[j, h].T

    f = f_ref[...]
    b_all = _dot_small_int_lhs(tri_ref[...], jnp.log(f), (((1,), (0,)), ((), ())))
    go = go_ref[...]

    def finish(rows, lanes, o):
        o_ref[rows, lanes] = (_head_norm(o, go) * zs_ref[rows, lanes]).astype(o_ref.dtype)

    def exact_rows(r0, n_rows, rebase):
        for h in range(B_HEADS):
            lanes = slice(h * DH, (h + 1) * DH)
            outs = []
            for i in range(n_rows // sc):
                rows = slice(r0 + i * sc, r0 + (i + 1) * sc)
                b = b_all[rows, lanes]
                if rebase and i > 0:
                    b = b - b_all[r0 + i * sc - 1:r0 + i * sc, lanes]
                j = (r0 + i * sc) // t_seq
                o, st = _hgrn_exact(q_ref[rows, lanes], 1.0 - f[rows, lanes], v_ref[rows, lanes], b, st_scr[j, h])
                st_scr[j, h] = st
                outs.append(o)
            finish(slice(r0, r0 + n_rows), lanes, jnp.concatenate(outs, axis=0) if len(outs) > 1 else outs[0])

    if fc is None:
        exact_rows(0, tc, False)
    else:
        for i in range(tc // fc):
            r0 = i * fc
            rows = slice(r0, r0 + fc)
            total = jnp.max(-b_all[r0 + fc - 1:r0 + fc, :])

            @pl.when(total < HGRN_SAFE_LOG_DECAY)
            def _():
                for h in range(B_HEADS):
                    lanes = slice(h * DH, (h + 1) * DH)
                    o, st = _hgrn_factored(q_ref[rows, lanes], 1.0 - f[rows, lanes], v_ref[rows, lanes],
                                           b_all[rows, lanes], st_scr[r0 // t_seq, h])
                    st_scr[r0 // t_seq, h] = st
                    finish(rows, lanes, o)

            @pl.when(jnp.logical_not(total < HGRN_SAFE_LOG_DECAY))
            def _():
                exact_rows(r0, fc, True)

    @pl.when(c == pl.num_programs(1) - 1)
    def _():
        for j in range(ns):
            for h in range(B_HEADS):
                sout_ref[j, h] = st_scr[j, h].T


def _block_tri(tc, blk):
    r = np.arange(tc)
    tri = (r[:, None] // blk == r[None, :] // blk) & (r[None, :] <= r[:, None])
    return jnp.asarray(tri.astype(np.float32), dtype=bf16)


def _hgrn(qb, f, ib, zs, s0, go, n, t, tc, sc, fc=None, ns=1):
    assert tc % sc == 0 and (fc is None or (tc % fc == 0 and fc % sc == 0))
    assert (ns == 1 and t % tc == 0) or (tc == ns * t and n % ns == 0 and fc is None and t % sc == 0)
    nc = max(t // tc, 1)
    rows = lambda i, c: (i * nc + c, 0)
    st = lambda i, c: (i, 0, 0, 0)
    const = lambda i, c: (0, 0)
    return pl.pallas_call(
        functools.partial(_hgrn_kernel, sc=sc, fc=fc),
        grid=(n // ns, nc),
        in_specs=[pl.BlockSpec((tc, B_WIDTH), rows)] * 4 + [
            pl.BlockSpec((ns, B_HEADS, DH, DH), st),
            pl.BlockSpec((1, DH), const),
            pl.BlockSpec((tc, tc), const),
        ],
        out_specs=[pl.BlockSpec((tc, B_WIDTH), rows), pl.BlockSpec((ns, B_HEADS, DH, DH), st)],
        out_shape=[jax.ShapeDtypeStruct((n * t, B_WIDTH), bf16), jax.ShapeDtypeStruct((n, B_HEADS, DH, DH), f32)],
        scratch_shapes=[pltpu.VMEM((ns, B_HEADS, DH, DH), f32)],
        compiler_params=pltpu.CompilerParams(
            dimension_semantics=("arbitrary", "arbitrary"), vmem_limit_bytes=VMEM_LIMIT),
        name="hgrn",
    )(qb, f, ib, zs, s0, go, _block_tri(tc, sc if fc is None else fc))


def _rel_bucket_np(dist):
    n = np.maximum(dist, 0)
    exact = REL_BUCKETS // 2
    scale = np.float32((REL_BUCKETS - exact) / math.log(REL_MAX_DIST / exact))
    large = exact + (np.log(np.maximum(n, exact).astype(np.float32) / np.float32(exact)) * scale).astype(np.int32)
    return np.where(n < exact, n, np.minimum(large, REL_BUCKETS - 1)).astype(np.int32)


def _bucket_starts():
    buckets = _rel_bucket_np(np.arange(2 * REL_MAX_DIST))
    assert np.all(np.diff(buckets) >= 0) and buckets[-1] == REL_BUCKETS - 1
    return [int(np.argmax(buckets >= b)) for b in range(REL_BUCKETS)]


def _bias_table_kernel(rb_ref, o_ref, *, off, a0, a1, sub_far):
    h = pl.program_id(0)
    shape = o_ref.shape[1:]
    d = off + a0 * lax.broadcasted_iota(jnp.int32, shape, 0) + a1 * lax.broadcasted_iota(jnp.int32, shape, 1)
    starts = _bucket_starts()
    far = rb_ref[REL_BUCKETS - 1, h]
    v = jnp.full(shape, far, f32)
    for b in range(REL_BUCKETS - 2, -1, -1):
        v = jnp.where(d < starts[b + 1], rb_ref[b, h], v)
    o_ref[0] = v - far if sub_far else v


def _bias_table(rel_bias, shape, off, a0, a1, sub_far):
    return pl.pallas_call(
        functools.partial(_bias_table_kernel, off=off, a0=a0, a1=a1, sub_far=sub_far),
        grid=(A_HEADS,),
        in_specs=[pl.BlockSpec(memory_space=pltpu.SMEM)],
        out_specs=pl.BlockSpec((1,) + tuple(shape), lambda h: (h, 0, 0)),
        out_shape=jax.ShapeDtypeStruct((A_HEADS,) + tuple(shape), f32),
        compiler_params=pltpu.CompilerParams(dimension_semantics=("arbitrary",)),
        name="bias_table",
    )(rel_bias.astype(f32))


def _cmp_bias_t(rel_bias, t0, nt, n_chunk):
    return _bias_table(rel_bias, (n_chunk, nt), t0 - (CMP_BLOCK - 1), -CMP_STRIDE, 1, False)


def _near_tiles_t(rel_bias):
    return _bias_table(rel_bias, (QB, 2 * QB), 0, -1, 1, True)


def _compress_combine(lhs, wcat, pe2):
    out = jnp.dot(lhs, wcat, preferred_element_type=f32)
    pc = jnp.dot(pe2, wcat, preferred_element_type=f32)
    const = pc[0:1, :DH] + pc[1:2, DH:]
    r = lhs.shape[0]
    return out[:, :DH] + pltpu.roll(out[:, DH:], r - 1, 0) + const


def _compress_kernel(kv_ref, wk_ref, wv_ref, pek_ref, pev_ref, gk_ref, kc_ref, vct_ref):
    nchunk = kv_ref.shape[0] // (4 * CMP_STRIDE)

    def chunk_rows(cg):
        return jnp.concatenate(
            [kv_ref[pl.ds(4 * l + cg, nchunk, stride=4 * CMP_STRIDE), :].astype(bf16) for l in range(CMP_STRIDE)],
            axis=1)

    kc = _compress_combine(jnp.concatenate([chunk_rows(0), chunk_rows(1)], axis=0), wk_ref[...], pek_ref[...])
    vc = _compress_combine(jnp.concatenate([chunk_rows(2), chunk_rows(3)], axis=0), wv_ref[...], pev_ref[...])
    kc = _head_norm(kc, gk_ref[...])
    for g in range(A_KV_GROUPS):
        kc_ref[0, g] = kc[g * nchunk:(g + 1) * nchunk].astype(kc_ref.dtype)
        vct_ref[0, g] = vc[g * nchunk:(g + 1) * nchunk].T.astype(vct_ref.dtype)


def _prep_cmp_weights(w, pe):
    half = CMP_STRIDE * DH
    wcat = jnp.concatenate([w[:half], w[half:]], axis=1).astype(bf16)
    return wcat, pe.reshape(2, half).astype(bf16)


def _compress(kvc, wk, wv, pek, pev, gk, n, t):
    nchunk = t // CMP_STRIDE
    assert nchunk == DH, "the transposed v_c block is square"
    const = lambda i: (0, 0)
    return pl.pallas_call(
        _compress_kernel, grid=(n,),
        in_specs=[pl.BlockSpec((t * 4, DH), lambda i: (i, 0)),
                  pl.BlockSpec(wk.shape, const), pl.BlockSpec(wv.shape, const),
                  pl.BlockSpec(pek.shape, const), pl.BlockSpec(pev.shape, const), pl.BlockSpec((1, DH), const)],
        out_specs=[pl.BlockSpec((1, A_KV_GROUPS, nchunk, DH), lambda i: (i, 0, 0, 0))] * 2,
        out_shape=[jax.ShapeDtypeStruct((n, A_KV_GROUPS, nchunk, DH), bf16)] * 2,
        compiler_params=pltpu.CompilerParams(dimension_semantics=("arbitrary",), vmem_limit_bytes=VMEM_LIMIT),
        name="compress",
    )(kvc, wk, wv, pek, pev, gk)


def _softmax_tile(carry, s, mask, v_bf):
    m, l, acc = carry
    s = jnp.where(mask, s, NEG)
    m_new = jnp.maximum(m, jnp.max(s, axis=-1, keepdims=True))
    alpha = jnp.exp(m - m_new)
    p = jnp.where(mask, jnp.exp(s - m_new), 0.0)
    l = alpha * l + jnp.sum(p, axis=-1, keepdims=True)
    h, r, k = p.shape
    pv = jnp.dot(p.reshape(h * r, k).astype(bf16), v_bf, preferred_element_type=f32).reshape(h, r, DH)
    return m_new, l, alpha * acc + pv


def _softmax_finish(carry):
    m, l, acc = carry
    return acc / jnp.where(l > 0, l, 1.0)


def _select_blocks(score_t, tpos, n_blocks):
    nb = score_t.shape[0]
    j = lax.broadcasted_iota(jnp.int32, score_t.shape, 0)
    valid = (j * SLC_BLOCK <= tpos) & (j < n_blocks)
    cur = tpos >> 6
    forced = (j == 0) | (j == cur) | (j == cur - 1)
    val = jnp.where(valid, jnp.where(forced, 1e30, score_t), -1.0)
    rank = jnp.zeros(score_t.shape, jnp.int32)
    for i in range(n_blocks):
        vi = val[i:i + 1, :]
        beats = (vi > val) | ((vi == val) & (i < j))
        rank = rank + beats.astype(jnp.int32)
    return jnp.where((rank < N_SELECT) & valid, 1.0, 0.0)


QB = 128
NSA_PROMPT_SEQS = 2


NEG_M = -1e30
NEG_S = -2e30


def _nsa_prompt_kernel(q_ref, kc_ref, vct_ref, bc_ref, kvs_ref, kvw_ref, gate_ref, za_ref, bt_ref, mt_ref,
                       o_ref, ks_scr, vts_scr, kw_scr, vtw_scr, km_scr):
    qb = pl.program_id(1)
    q0 = qb * QB
    n_far = jnp.maximum(qb - 1, 0)
    nkt = vts_scr.shape[1]
    nq4 = A_HPG * QB
    nseq = q_ref.shape[0]
    chains = [(s, g) for s in range(nseq) for g in range(A_KV_GROUPS)]
    groups = range(len(chains))

    @pl.when(qb == 0)
    def _():
        for c, (s, g) in enumerate(chains):
            for kt in range(nkt):
                rows = slice(kt * QB, (kt + 1) * QB)
                base = kt * 4 * QB
                ks_scr[c, rows, :] = kvs_ref[s, pl.ds(base + g, QB, stride=4), :].astype(bf16)
                vts_scr[c, kt] = kvs_ref[s, pl.ds(base + 2 + g, QB, stride=4), :].T.astype(bf16)
                kw_scr[c, rows, :] = kvw_ref[s, pl.ds(base + g, QB, stride=4), :].astype(bf16)
                vtw_scr[c, kt] = kvw_ref[s, pl.ds(base + 2 + g, QB, stride=4), :].T.astype(bf16)

    qts = [jnp.concatenate([q_ref[s, :, (g * A_HPG + h) * DH:(g * A_HPG + h + 1) * DH].astype(f32).T
                            for h in range(A_HPG)], axis=1).astype(bf16) for s, g in chains]
    kk = lax.broadcasted_iota(jnp.int32, (QB, nq4), 0)
    tt = lax.broadcasted_iota(jnp.int32, (QB, nq4), 1) & (QB - 1)

    def heads(ref, g, *idx):
        return jnp.concatenate([ref[(g * A_HPG + h,) + idx] for h in range(A_HPG)], axis=1)

    o_c = []
    for c, (s, g) in enumerate(chains):
        sc = jnp.dot(kc_ref[s, g], qts[c], preferred_element_type=f32) + heads(bc_ref, g)
        mask_c = q0 + tt >= CMP_STRIDE * kk + (CMP_BLOCK - 1)
        sc = jnp.where(mask_c, sc, NEG)
        mc = jnp.max(sc, axis=0, keepdims=True)
        ec = jnp.where(mask_c, jnp.exp(sc - mc), 0.0)
        lc = jnp.sum(ec, axis=0, keepdims=True)
        pc = ec / jnp.where(lc > 0, lc, 1.0)
        o_c.append(jnp.dot(vct_ref[s, g], pc.astype(bf16), preferred_element_type=f32))
        ps = pc[:, 0:QB] + pc[:, QB:2 * QB] + pc[:, 2 * QB:3 * QB] + pc[:, 3 * QB:4 * QB]
        score_t = _dot_small_int_lhs(mt_ref[...], ps, (((1,), (0,)), ((), ())))
        nb = score_t.shape[0]
        sel_t = _select_blocks(score_t, q0 + lax.broadcasted_iota(jnp.int32, (nb, QB), 1), nb)
        for j in range(nb):
            km_scr[c, j * SLC_BLOCK:(j + 1) * SLC_BLOCK, :] = jnp.broadcast_to(sel_t[j:j + 1, :], (SLC_BLOCK, QB))

    def tile(carry, qt, k_tile, vt_tile, mask, bias):
        m, l, acc = carry
        s = jnp.dot(k_tile, qt, preferred_element_type=f32)
        if bias is not None:
            s = s + bias
        s = jnp.where(mask, s, NEG_S)
        m_new = jnp.maximum(m, jnp.max(s, axis=0, keepdims=True))
        alpha = jnp.exp(m - m_new)
        p = jnp.exp(s - m_new)
        l = alpha * l + jnp.sum(p, axis=0, keepdims=True)
        return m_new, l, alpha * acc + jnp.dot(vt_tile, p.astype(bf16), preferred_element_type=f32)

    def init():
        return jnp.full((1, nq4), NEG_M, f32), jnp.zeros((1, nq4), f32), jnp.zeros((DH, nq4), f32)

    def finish(carry):
        m, l, acc = carry
        return acc / jnp.where(l > 0, l, 1.0)

    def key_rows(kt):
        return pl.ds(pl.multiple_of(kt * QB, QB), QB)

    def lanes4(x):
        return jnp.concatenate([x] * A_HPG, axis=1)

    def near_span(g, k_scr, vt_scr, rs, use_sel):
        ks, vts, masks, biases = [], [], [], []
        for r in rs:
            kt = jnp.maximum(qb - r, 0)
            ks.append(k_scr[g, key_rows(kt), :])
            vts.append(vt_scr[g, kt])
            if use_sel:
                mask = lanes4(km_scr[g, key_rows(kt), :]) > 0.5
                if r == 0:
                    mask = mask & (kk <= tt)
            else:
                mask = (kk <= tt) if r == 0 else (kk >= tt) if r == WINDOW // QB else (kk >= 0)
            masks.append(mask)
            off = jnp.where(qb - r >= 0, 0.0, NEG_S)
            biases.append(heads(bt_ref, chains[g][1], slice(None), slice(r * QB, (r + 1) * QB)) + off if r < 2
                          else jnp.full((QB, nq4), off, f32))
        cat = lambda xs, axis: jnp.concatenate(xs, axis=axis)
        return cat(ks, 0), cat(vts, 1), cat(masks, 0), cat(biases, 0)

    o_w = [finish(tile(init(), qts[g], *near_span(g, kw_scr, vtw_scr, (4, 3, 2, 1, 0), False))) for g in groups]

    kk2 = lax.broadcasted_iota(jnp.int32, (2 * QB, nq4), 0)

    def far(i, carries):
        rows = pl.ds(pl.multiple_of(i * 2 * QB, 2 * QB), 2 * QB)
        in_range = (kk2 + i * 2 * QB) < n_far * QB
        out = []
        for g in groups:
            mask = (lanes4(km_scr[g, rows, :]) > 0.5) & in_range
            vt = jnp.concatenate([vts_scr[g, 2 * i], vts_scr[g, jnp.minimum(2 * i + 1, nkt - 1)]], axis=1)
            out.append(tile(carries[g], qts[g], ks_scr[g, rows, :], vt, mask, None))
        return tuple(out)

    carries = lax.fori_loop(0, (n_far + 1) // 2, far, tuple(init() for g in groups))
    o_s = [finish(tile(carries[g], qts[g], *near_span(g, ks_scr, vts_scr, (1, 0), True))) for g in groups]

    for s in range(nseq):
        za = za_ref[s]
        outs = []
        for g in range(A_KV_GROUPS):
            c = s * A_KV_GROUPS + g
            gate_t = gate_ref[s, :, g * DH:(g + 1) * DH].T
            for h in range(A_HPG):
                cols = slice(h * QB, (h + 1) * QB)

                def grow(br):
                    return gate_t[br * A_HPG + h:br * A_HPG + h + 1, :]
                o = grow(0) * o_c[c][:, cols] + grow(1) * o_s[c][:, cols] + grow(2) * o_w[c][:, cols]
                outs.append(o.T * za[:, (g * A_HPG + h) * DH:(g * A_HPG + h + 1) * DH])
        o_ref[s] = jnp.concatenate(outs, axis=1).astype(o_ref.dtype)


def _cmp_to_slc_t(n_cmp_pad, n_slc_pad, n_cmp, n_slc):
    c0 = np.arange(n_cmp_pad)[None, :] * CMP_STRIDE
    s0 = np.arange(n_slc_pad)[:, None] * SLC_BLOCK
    ov = np.minimum(c0 + CMP_BLOCK, s0 + SLC_BLOCK) - np.maximum(c0, s0)
    m = np.maximum(ov, 0).astype(np.float32) / CMP_STRIDE
    m[:, n_cmp:] = 0
    m[n_slc:, :] = 0
    return m


def _nsa_prompt(qa, kc, vct, kvs, kvw, gate, za, rel_bias, n, t):
    nq = t // QB
    n_cmp = (t - CMP_BLOCK) // CMP_STRIDE + 1
    n_slc = -(-t // SLC_BLOCK)
    assert kc.shape[2] == QB and n_slc % 8 == 0
    bias_c = _cmp_bias_t(rel_bias, 0, t, QB)
    bt = _near_tiles_t(rel_bias)
    mt = jnp.asarray(_cmp_to_slc_t(QB, n_slc, n_cmp, n_slc), dtype=bf16)
    nseq = NSA_PROMPT_SEQS
    assert n % nseq == 0
    ng = A_KV_GROUPS
    nchain = nseq * ng
    qrow = lambda w: pl.BlockSpec((nseq, QB, w), lambda i, b: (i, b, 0))
    seq4 = lambda shape: pl.BlockSpec((nseq,) + shape, lambda i, b: (i, 0, 0, 0))
    seq3 = lambda shape: pl.BlockSpec((nseq,) + shape, lambda i, b: (i, 0, 0), pipeline_mode=pl.Buffered(1))
    by_seq = lambda a, w: a.reshape(n, -1, w)
    out = pl.pallas_call(
        _nsa_prompt_kernel, grid=(n // nseq, nq),
        in_specs=[
            qrow(A_WIDTH),
            seq4((ng, QB, DH)),
            seq4((ng, DH, QB)),
            pl.BlockSpec((A_HEADS, QB, QB), lambda i, b: (0, 0, b)),
            seq3((t * 4, DH)),
            seq3((t * 4, DH)),
            qrow(ng * DH),
            qrow(A_WIDTH),
            pl.BlockSpec((A_HEADS, QB, 2 * QB), lambda i, b: (0, 0, 0)),
            pl.BlockSpec(mt.shape, lambda i, b: (0, 0)),
        ],
        out_specs=qrow(A_WIDTH),
        out_shape=jax.ShapeDtypeStruct((n, t, A_WIDTH), bf16),
        scratch_shapes=[pltpu.VMEM((nchain, t, DH), bf16), pltpu.VMEM((nchain, nq, DH, QB), bf16),
                        pltpu.VMEM((nchain, t, DH), bf16), pltpu.VMEM((nchain, nq, DH, QB), bf16),
                        pltpu.VMEM((nchain, t, QB), f32)],
        compiler_params=pltpu.CompilerParams(
            dimension_semantics=("arbitrary", "arbitrary"), vmem_limit_bytes=VMEM_LIMIT),
        name="nsa_prompt",
    )(by_seq(qa, A_WIDTH), kc, vct, bias_c, by_seq(kvs, DH), by_seq(kvw, DH), by_seq(gate, ng * DH),
      by_seq(za, A_WIDTH), bt, mt)
    return out.reshape(n * t, A_WIDTH)


CMP_PITCH = 24


def _nsa_sample_kernel(pt_ref, *refs, n_pages, page, t_new, n_blocks):
    del pt_ref
    cmp_pages = refs[:n_pages]
    slc_pages = refs[n_pages:2 * n_pages]
    (q_ref, kvs_new_ref, kvw_new_ref, win_ref, wk_ref, wv_ref, pek_ref, pev_ref, gk_ref, bc_ref, bl_ref, bn_ref,
     gate_ref, za_ref, mt_ref, e_ref, o_ref, win_out_ref, xc_scr) = refs[2 * n_pages:]
    past = n_pages * page
    nchunk = past // CMP_STRIDE
    pad_new = QB
    ng = A_KV_GROUPS
    groups = range(ng)

    cpp = page // CMP_STRIDE
    for p in range(n_pages):
        for cg in range(4):
            x = cmp_pages[p][0, pl.ds(cg, page, stride=4), :]
            for c in range(cpp):
                r0 = (p * cpp + c) * CMP_PITCH
                xc_scr[cg, r0:r0 + CMP_STRIDE, :] = x[c * CMP_STRIDE:(c + 1) * CMP_STRIDE]

    def chunk_rows(cg):
        return jnp.concatenate(
            [xc_scr[cg, pl.ds(l, nchunk, stride=CMP_PITCH), :].astype(bf16) for l in range(CMP_STRIDE)], axis=1)

    kc = _compress_combine(jnp.concatenate([chunk_rows(0), chunk_rows(1)], axis=0), wk_ref[...], pek_ref[...])
    vc = _compress_combine(jnp.concatenate([chunk_rows(2), chunk_rows(3)], axis=0), wv_ref[...], pev_ref[...])
    kc = _head_norm(kc, gk_ref[...]).astype(bf16)
    vc = vc.astype(bf16)

    qall = q_ref[...]
    qs = [jnp.concatenate([qall[:, (g * A_HPG + h) * DH:(g * A_HPG + h + 1) * DH] for h in range(A_HPG)], axis=0)
          for g in groups]
    zeros_pad = jnp.zeros((pad_new - t_new, DH), bf16)

    def new_rows(ref, c):
        return jnp.concatenate([ref[pl.ds(c, t_new, stride=4), :].astype(bf16), zeros_pad], axis=0)

    def logits(ks):
        return jnp.stack([lax.dot_general(qs[g], ks[g], (((1,), (1,)), ((), ())), preferred_element_type=f32)
                          .reshape(A_HPG, t_new, ks[g].shape[0]) for g in groups], axis=0)

    def softmax_many(items):
        kmax = max(s.shape[-1] for s, _ in items)
        padded = []
        for s, m in items:
            s = jnp.where(m, s, NEG)
            if s.shape[-1] < kmax:
                s = jnp.concatenate([s, jnp.full(s.shape[:-1] + (kmax - s.shape[-1],), NEG, f32)], axis=-1)
            padded.append(s)
        s_all = jnp.stack(padded, axis=0)
        valid = s_all > 0.5 * NEG
        e = jnp.where(valid, jnp.exp(s_all - jnp.max(s_all, axis=-1, keepdims=True)), 0.0)
        l = jnp.sum(e, axis=-1, keepdims=True)
        p = e / jnp.where(l > 0, l, 1.0)
        return [p[i][..., :s.shape[-1]] for i, (s, _) in enumerate(items)]

    def pv(p, vs):
        return jnp.stack([jnp.dot(p[g].reshape(A_HPG * t_new, -1).astype(bf16), vs[g], preferred_element_type=f32)
                          .reshape(A_HPG, t_new, DH) for g in groups], axis=0)

    def tail_bias(s):
        bias = jnp.concatenate([bl_ref[...], bn_ref[...]], axis=-1).reshape(ng, A_HPG, t_new, 2 * QB)
        nk = s.shape[-1]
        return jnp.concatenate([s[..., :nk - 2 * QB], s[..., nk - 2 * QB:] + bias], axis=-1)

    tt = lax.broadcasted_iota(jnp.int32, (t_new, pad_new), 0)
    uu = lax.broadcasted_iota(jnp.int32, (t_new, pad_new), 1)
    new_mask = uu <= tt

    ti = lax.broadcasted_iota(jnp.int32, (t_new, nchunk), 0)
    ci = lax.broadcasted_iota(jnp.int32, (t_new, nchunk), 1)
    cmask = (past + ti >= CMP_STRIDE * ci + (CMP_BLOCK - 1))[None, None]
    sc = logits([kc[g * nchunk:(g + 1) * nchunk] for g in groups]) + bc_ref[...].reshape(ng, A_HPG, t_new, nchunk)
    nw = win_ref.shape[1] // 4
    kw = [jnp.concatenate([win_ref[0, pl.ds(g, nw, stride=4), :].astype(bf16), new_rows(kvw_new_ref, g)], axis=0)
          for g in groups]
    vw = [jnp.concatenate([win_ref[0, pl.ds(2 + g, nw, stride=4), :].astype(bf16), new_rows(kvw_new_ref, 2 + g)], axis=0)
          for g in groups]
    wmask = lax.broadcasted_iota(jnp.int32, (t_new, nw), 1) >= lax.broadcasted_iota(jnp.int32, (t_new, nw), 0)
    pc, pw = softmax_many([(sc, cmask),
                           (tail_bias(logits(kw)), jnp.concatenate([wmask, new_mask], axis=1)[None, None])])
    o_c = pv(pc, [vc[g * nchunk:(g + 1) * nchunk] for g in groups])
    o_w = pv(pw, vw)

    ps = jnp.concatenate([pc[g, 0] + pc[g, 1] + pc[g, 2] + pc[g, 3] for g in groups], axis=0)
    score_t = _dot_small_int_lhs(mt_ref[...], ps, (((1,), (1,)), ((), ())))
    lane = lax.broadcasted_iota(jnp.int32, score_t.shape, 1)
    tpos = past + jnp.where(lane >= t_new, lane - t_new, lane)
    sel_t = _select_blocks(score_t, tpos, n_blocks)
    key_mask = lax.dot_general(sel_t.astype(bf16), e_ref[...], (((0,), (0,)), ((), ())),
                               preferred_element_type=f32)

    new_mask_f = jnp.where(new_mask, 1.0, 0.0)
    sel_mask = jnp.concatenate([key_mask, jnp.concatenate([new_mask_f] * ng, axis=0)], axis=1)
    sel_mask = sel_mask.reshape(ng, 1, t_new, past + pad_new) > 0.5
    ks = [jnp.concatenate([slc_pages[p][0, pl.ds(g, page, stride=4), :].astype(bf16) for p in range(n_pages)]
                          + [new_rows(kvs_new_ref, g)], axis=0) for g in groups]
    vs = [jnp.concatenate([slc_pages[p][0, pl.ds(2 + g, page, stride=4), :].astype(bf16) for p in range(n_pages)]
                          + [new_rows(kvs_new_ref, 2 + g)], axis=0) for g in groups]
    o_s = pv(softmax_many([(tail_bias(logits(ks)), sel_mask)])[0], vs)

    gate_all = gate_ref[...]
    za = za_ref[...]
    outs = []
    for g in groups:
        gate = gate_all[:, g * DH:(g + 1) * DH]
        for h in range(A_HPG):
            def gcol(br):
                return gate[:, br * A_HPG + h:br * A_HPG + h + 1]
            o = gcol(0) * o_c[g, h] + gcol(1) * o_s[g, h] + gcol(2) * o_w[g, h]
            outs.append(o * za[:, (g * A_HPG + h) * DH:(g * A_HPG + h + 1) * DH])
    o_ref[...] = jnp.concatenate(outs, axis=1).astype(o_ref.dtype)

    nrow = win_ref.shape[1]
    win_out_ref[0, 0:nrow - 4 * t_new, :] = win_ref[0, 4 * t_new:nrow, :]
    win_out_ref[0, nrow - 4 * t_new:nrow, :] = kvw_new_ref[...]


def _nsa_sample(qa, kvs_new, kvw_new, cache_cmp, cache_slc, cache_win, page_table, wk, wv, pek, pev, gk,
                gate, za, rel_bias, t_new):
    n, n_pages = page_table.shape
    page = cache_cmp.shape[1] // 4
    past = n_pages * page
    wb = cache_win.shape[1] // 4
    assert wb == WINDOW and past % QB == 0 and past >= WINDOW
    nchunk = past // CMP_STRIDE
    n_cmp = (past + t_new - CMP_BLOCK) // CMP_STRIDE + 1
    assert n_cmp < nchunk
    n_slc = -(-(past + t_new) // SLC_BLOCK)
    nb_pad = -(-n_slc // 8) * 8
    bias_c = _bias_table(rel_bias, (t_new, nchunk), past - (CMP_BLOCK - 1), 1, -CMP_STRIDE, False)
    b_last = _bias_table(rel_bias, (t_new, QB), QB, 1, -1, True)
    b_new = _bias_table(rel_bias, (t_new, QB), 0, 1, -1, True)
    mt = jnp.asarray(_cmp_to_slc_t(nchunk, nb_pad, n_cmp, n_slc), dtype=bf16)
    e = jnp.asarray((np.arange(past)[None, :] // SLC_BLOCK == np.arange(nb_pad)[:, None]).astype(np.float32), dtype=bf16)

    def page_spec(p):
        return pl.BlockSpec((1, page * 4, DH), lambda i, pt: (pt[i, p], 0, 0))

    rowblk = lambda w: pl.BlockSpec((t_new, w), lambda i, pt: (i, 0))
    full = lambda a: pl.BlockSpec(a.shape, lambda i, pt: (0,) * a.ndim)
    in_specs = ([page_spec(p) for p in range(n_pages)] * 2 + [
        rowblk(A_WIDTH),
        pl.BlockSpec((t_new * 4, DH), lambda i, pt: (i, 0)),
        pl.BlockSpec((t_new * 4, DH), lambda i, pt: (i, 0)),
        pl.BlockSpec((1, wb * 4, DH), lambda i, pt: (i, 0, 0)),
        full(wk), full(wv), full(pek), full(pev), full(gk), full(bias_c), full(b_last), full(b_new),
        rowblk(A_KV_GROUPS * DH), rowblk(A_WIDTH), full(mt), full(e)])
    grid_spec = pltpu.PrefetchScalarGridSpec(
        num_scalar_prefetch=1, grid=(n,), in_specs=in_specs,
        out_specs=[rowblk(A_WIDTH), pl.BlockSpec((1, wb * 4, DH), lambda i, pt: (i, 0, 0))],
        scratch_shapes=[pltpu.VMEM((4, nchunk * CMP_PITCH, DH), f32)])
    return pl.pallas_call(
        functools.partial(_nsa_sample_kernel, n_pages=n_pages, page=page, t_new=t_new, n_blocks=n_slc),
        grid_spec=grid_spec,
        out_shape=[jax.ShapeDtypeStruct((n * t_new, A_WIDTH), f32), jax.ShapeDtypeStruct(cache_win.shape, f32)],
        compiler_params=pltpu.CompilerParams(dimension_semantics=("arbitrary",), vmem_limit_bytes=VMEM_LIMIT),
        name="nsa_sample",
    )(page_table, *([cache_cmp] * n_pages), *([cache_slc] * n_pages), qa, kvs_new, kvw_new, cache_win,
      wk, wv, pek, pev, gk, bias_c, b_last, b_new, gate, za, mt, e)


def _outproj_kernel(x_ref, oa_ref, ob_ref, w_ref, y_ref):
    acc = jnp.dot(oa_ref[...].astype(bf16), w_ref[0:A_WIDTH, :], preferred_element_type=f32)
    acc = acc + jnp.dot(ob_ref[...].astype(bf16), w_ref[A_WIDTH:, :], preferred_element_type=f32)
    y_ref[...] = x_ref[...] + acc


def _outproj(x2d, oa, ob, w_out_bf, tm):
    m = x2d.shape[0]
    assert m % tm == 0
    row = lambda i: (i, 0)
    return pl.pallas_call(
        _outproj_kernel, grid=(m // tm,),
        in_specs=[pl.BlockSpec((tm, D_MODEL), row), pl.BlockSpec((tm, A_WIDTH), row), pl.BlockSpec((tm, B_WIDTH), row),
                  pl.BlockSpec(w_out_bf.shape, lambda i: (0, 0))],
        out_specs=pl.BlockSpec((tm, D_MODEL), row),
        out_shape=jax.ShapeDtypeStruct((m, D_MODEL), f32),
        compiler_params=pltpu.CompilerParams(dimension_semantics=("arbitrary",), vmem_limit_bytes=VMEM_LIMIT),
        name="outproj",
    )(x2d, oa, ob, w_out_bf)


def kernel(x_prompt, x_sample, cache_kv_cmp, cache_kv_slc, cache_kv_win, state_hgrn, page_table, g_norm, w_in, w_out,
           g_q, g_k_slc, g_k_win, g_k_cmp, w_cmp_k, w_cmp_v, pe_cmp_k, pe_cmp_v, rel_bias, lb_logits, g_o_hgrn):
    depth = w_in.shape[0]
    assert depth == 1, "single-layer trunk"
    nb, t, _ = x_prompt.shape
    ns, ts, _ = x_sample.shape
    row = lambda a: a.astype(f32)[None]
    lower = jnp.cumsum(jax.nn.softmax(lb_logits.astype(f32), axis=0), axis=0)[0]
    w_main, w_gate = _prep_proj_weights(w_in[0])
    w_out_bf = w_out[0].astype(bf16)
    wk, pek = _prep_cmp_weights(w_cmp_k[0], pe_cmp_k[0])
    wv, pev = _prep_cmp_weights(w_cmp_v[0], pe_cmp_v[0])
    proj_args = (row(g_norm[0]), w_main, w_gate, row(g_q[0]), row(g_k_slc[0]), row(g_k_win[0]), row(lower))
    kv6 = lambda a, n_, t_: a.reshape(1, n_, t_, 2, A_KV_GROUPS, DH)

    xp = x_prompt.reshape(nb * t, D_MODEL)
    wlen = min(WINDOW, t)
    assert wlen == PROJ_TM, "the proj row tile doubles as the prompt's final window"
    qa, kvc, kvs, kvw, gate, za, qb, f, ib, zb, win_p = _proj(xp, *proj_args, tm=PROJ_TM, t_seq=t)
    s0 = jnp.zeros((nb, B_HEADS, DH, DH), f32)
    ob, st_p = _hgrn(qb, f, ib, zb, s0, row(g_o_hgrn[0]), nb, t, tc=128, sc=16, fc=64)
    kc, vct = _compress(kvc, wk, wv, pek, pev, row(g_k_cmp[0]), nb, t)
    oa = _nsa_prompt(qa, kc, vct, kvs, kvw, gate, za, rel_bias, nb, t)
    y_p = _outproj(xp, oa, ob, w_out_bf, tm=PROJ_TM).reshape(nb, t, D_MODEL)
    win_p = kv6(win_p, nb, wlen)

    xs = x_sample.reshape(ns * ts, D_MODEL)
    qa, kvc_s, kvs_s, kvw_s, gate, za, qb, f, ib, zb, _ = _proj(xs, *proj_args, tm=PROJ_TM, t_seq=PROJ_TM)
    seq_per_step = 8
    ob, st_s = _hgrn(qb, f, ib, zb, state_hgrn[0].astype(f32), row(g_o_hgrn[0]), ns, ts, tc=seq_per_step * ts, sc=ts,
                     ns=seq_per_step)
    pool = cache_kv_cmp.shape[1]
    page = cache_kv_cmp.shape[2]
    oa, win_s = _nsa_sample(
        qa, kvs_s, kvw_s, cache_kv_cmp[0].reshape(pool, page * 4, DH), cache_kv_slc[0].reshape(pool, page * 4, DH),
        cache_kv_win[0].reshape(ns, -1, DH), page_table, wk, wv, pek, pev, row(g_k_cmp[0]), gate, za, rel_bias, ts)
    y_s = _outproj(xs, oa, ob, w_out_bf, tm=PROJ_TM).reshape(ns, ts, D_MODEL)

    return (y_p, y_s, kv6(kvc, nb, t), kv6(kvs, nb, t), win_p, st_p[None].astype(x_prompt.dtype),
            kv6(kvc_s, ns, ts), kv6(kvs_s, ns, ts), kv6(win_s, ns, WINDOW), st_s[None].astype(state_hgrn.dtype))
```

```python
import functools
import math

import jax
import jax.numpy as jnp
import numpy as np
from jax import lax
from jax.experimental import pallas as pl
from jax.experimental.pallas import tpu as pltpu

f32 = jnp.float32
bf16 = jnp.bfloat16

D_MODEL = 2048
A_HEADS = 8
A_KV_GROUPS = 2
A_HPG = A_HEADS // A_KV_GROUPS
DH = 128
A_WIDTH = A_HEADS * DH
CMP_BLOCK = 32
CMP_STRIDE = 16
SLC_BLOCK = 64
N_SELECT = 16
WINDOW = 512
B_HEADS = 8
B_WIDTH = B_HEADS * DH
REL_BUCKETS = 32
REL_MAX_DIST = 128
EPS = 1e-6
NEG = -1e30

VMEM_LIMIT = 56 * 1024 * 1024


def _sigmoid(x):
    return 1.0 / (1.0 + jnp.exp(-x))


def _head_norm(a, g):
    return a * lax.rsqrt(jnp.mean(a * a, axis=-1, keepdims=True) + EPS) * g


def _dot_small_int_lhs(m_bf, x, dims):
    hi = x.astype(bf16)
    r1 = x - hi.astype(f32)
    mid = r1.astype(bf16)
    lo = (r1 - mid.astype(f32)).astype(bf16)
    d = lambda p: lax.dot_general(m_bf, p, dims, preferred_element_type=f32)
    return d(hi) + d(mid) + d(lo)


PROJ_TN = 512
PROJ_TM = 512
PROJ_SEG = {0: "qa", 1: "qa", 2: "kvc", 3: "kvs", 4: "kvw", 5: "za", 6: "za", 7: "qb", 8: "qb", 9: "f", 10: "f",
            11: "ib", 12: "ib", 13: "zb", 14: "zb"}
PROJ_SLABS = ((0, 5), (5, 11), (11, 15))


def _proj_tiles(h, w_ref, first_tile, n_tiles, tm, out, par):
    for jj in range(n_tiles):
        j = first_tile + jj
        acc = lax.dot_general(h, w_ref[jj * PROJ_TN:(jj + 1) * PROJ_TN, :], (((1,), (1,)), ((), ())),
                              preferred_element_type=f32)
        name = PROJ_SEG[j]
        k = j - min(t for t, s in PROJ_SEG.items() if s == name)
        lanes = slice(k * PROJ_TN, (k + 1) * PROJ_TN)

        def cols(c):
            return acc[:, c * DH:(c + 1) * DH]

        if name == "qa":
            gq = par["gq"][...]
            out["qa"][:, lanes] = jnp.concatenate(
                [_head_norm(cols(c), gq) * (DH ** -0.5) for c in range(4)], axis=1).astype(out["qa"].dtype)
        elif name in ("kvc", "kvs", "kvw"):
            g_ref = {"kvc": None, "kvs": par["gks"], "kvw": par["gkw"]}[name]
            for c in range(4):
                val = _head_norm(cols(c), g_ref[...]) if (g_ref is not None and c < 2) else cols(c)
                out[name][pl.ds(c, tm, stride=4), :] = val
                if name == "kvw":
                    out["tail"][pl.ds(c, tm, stride=4), :] = val
        elif name in ("za", "zb"):
            out[name][:, lanes] = acc * _sigmoid(acc)
        elif name == "f":
            lb = par["lb"][:, lanes]
            out["f"][:, lanes] = lb + (1.0 - lb) * _sigmoid(acc)
        else:
            out[name][:, lanes] = acc


def _proj_first_kernel(x_ref, gn_ref, w_ref, wg_ref, gq_ref, gks_ref, gkw_ref,
                       h_ref, qa_ref, kvc_ref, kvs_ref, kvw_ref, tail_ref, gate_ref):
    x = x_ref[...]
    h = (x * lax.rsqrt(jnp.mean(x * x, axis=-1, keepdims=True) + EPS) * gn_ref[...]).astype(bf16)
    h_ref[...] = h
    gate_ref[...] = _sigmoid(lax.dot_general(h, wg_ref[...], (((1,), (1,)), ((), ())), preferred_element_type=f32))
    first, last = PROJ_SLABS[0]
    _proj_tiles(h, w_ref, first, last - first, x_ref.shape[0],
                dict(qa=qa_ref, kvc=kvc_ref, kvs=kvs_ref, kvw=kvw_ref, tail=tail_ref),
                dict(gq=gq_ref, gks=gks_ref, gkw=gkw_ref))


def _proj_rest_kernel(h_ref, w_ref, lb_ref, *out_refs, slab):
    first, last = PROJ_SLABS[slab]
    names = list(dict.fromkeys(PROJ_SEG[j] for j in range(first, last)))
    _proj_tiles(h_ref[...], w_ref, first, last - first, h_ref.shape[0], dict(zip(names, out_refs)), dict(lb=lb_ref))


def _proj(x2d, gn, w_slabs, w_gate, gq, gks, gkw, lb, tm, t_seq):
    m = x2d.shape[0]
    assert m % tm == 0 and t_seq % tm == 0
    tiles_per_seq = t_seq // tm
    row = lambda i: (i, 0)
    const = lambda i: (0, 0)
    params = pltpu.CompilerParams(dimension_semantics=("arbitrary",), vmem_limit_bytes=VMEM_LIMIT)
    kv_shape = jax.ShapeDtypeStruct((m * 4, DH), f32)
    wide = lambda dt: jax.ShapeDtypeStruct((m, A_WIDTH), dt)
    h, qa, kvc, kvs, kvw, tail, gate = pl.pallas_call(
        _proj_first_kernel, grid=(m // tm,),
        in_specs=[pl.BlockSpec((tm, D_MODEL), row), pl.BlockSpec((1, D_MODEL), const),
                  pl.BlockSpec(w_slabs[0].shape, const), pl.BlockSpec(w_gate.shape, const),
                  pl.BlockSpec((1, DH), const), pl.BlockSpec((1, DH), const), pl.BlockSpec((1, DH), const)],
        out_specs=[pl.BlockSpec((tm, D_MODEL), row), pl.BlockSpec((tm, A_WIDTH), row)]
        + [pl.BlockSpec((tm * 4, DH), row)] * 3
        + [pl.BlockSpec((tm * 4, DH), lambda i: (i // tiles_per_seq, 0)), pl.BlockSpec((tm, A_KV_GROUPS * DH), row)],
        out_shape=[jax.ShapeDtypeStruct((m, D_MODEL), bf16), wide(bf16), kv_shape, kv_shape, kv_shape,
                   jax.ShapeDtypeStruct((m // tiles_per_seq * 4, DH), f32),
                   jax.ShapeDtypeStruct((m, A_KV_GROUPS * DH), f32)],
        compiler_params=params, name="proj_a",
    )(x2d, gn, w_slabs[0], w_gate, gq, gks, gkw)

    def rest(slab, n_out):
        return pl.pallas_call(
            functools.partial(_proj_rest_kernel, slab=slab), grid=(m // tm,),
            in_specs=[pl.BlockSpec((tm, D_MODEL), row), pl.BlockSpec(w_slabs[slab].shape, const),
                      pl.BlockSpec((1, B_WIDTH), const)],
            out_specs=[pl.BlockSpec((tm, A_WIDTH), row)] * n_out, out_shape=[wide(f32)] * n_out,
            compiler_params=params, name="proj_" + "abc"[slab],
        )(h, w_slabs[slab], lb)

    za, qb, f = rest(1, 3)
    ib, zb = rest(2, 2)
    return qa, kvc, kvs, kvw, gate, za, qb, f, ib, zb, tail


def _prep_proj_weights(w_in):
    a0 = A_WIDTH + 6 * A_KV_GROUPS * DH
    a1 = a0 + 3 * A_HEADS
    w_t = w_in.T
    col0 = lambda tile: tile * PROJ_TN if tile * PROJ_TN < a0 else a1 + tile * PROJ_TN - a0
    w_slabs = tuple(w_t[col0(first):col0(first) + (last - first) * PROJ_TN].astype(bf16) for first, last in PROJ_SLABS)
    wg = w_t[a0:a1].reshape(3, A_KV_GROUPS, A_HPG, -1).transpose(1, 0, 2, 3).reshape(A_KV_GROUPS, 3 * A_HPG, -1)
    w_gate = jnp.pad(wg, ((0, 0), (0, DH - 3 * A_HPG), (0, 0))).reshape(A_KV_GROUPS * DH, -1).astype(bf16)
    return w_slabs, w_gate


HGRN_SAFE_LOG_DECAY = 80.0


def _hgrn_exact(q, k, v, b, st):
    sc = q.shape[0]
    t_idx = lax.broadcasted_iota(jnp.int32, (sc, DH), 0)
    o = lax.dot_general((q * jnp.exp(b)).astype(bf16), st.astype(bf16), (((1,), (1,)), ((), ())),
                        preferred_element_type=f32)
    for s in range(sc):
        e = jnp.exp(jnp.minimum(b - b[s:s + 1, :], 0.0))
        a = jnp.where(t_idx >= s, q * e * k[s:s + 1, :], 0.0)
        o = o + jnp.sum(a, axis=-1, keepdims=True) * v[s:s + 1, :]
    bl = b[sc - 1:sc, :]
    ut = lax.dot_general(v.astype(bf16), (k * jnp.exp(bl - b)).astype(bf16), (((0,), (0,)), ((), ())),
                         preferred_element_type=f32)
    return o, st * jnp.exp(bl) + ut


def _hgrn_factored(q, k, v, b, st):
    fc = q.shape[0]
    qd = (q * jnp.exp(b)).astype(bf16)
    att = lax.dot_general(qd, (k * jnp.exp(-b)).astype(bf16), (((1,), (1,)), ((), ())), preferred_element_type=f32)
    tri = lax.broadcasted_iota(jnp.int32, (fc, fc), 0) >= lax.broadcasted_iota(jnp.int32, (fc, fc), 1)
    att = jnp.where(tri, att, 0.0)
    o = jnp.dot(att.astype(bf16), v.astype(bf16), preferred_element_type=f32)
    o = o + lax.dot_general(qd, st.astype(bf16), (((1,), (1,)), ((), ())), preferred_element_type=f32)
    bl = b[fc - 1:fc, :]
    ut = lax.dot_general(v.astype(bf16), (k * jnp.exp(bl - b)).astype(bf16), (((0,), (0,)), ((), ())),
                         preferred_element_type=f32)
    return o, st * jnp.exp(bl) + ut


def _hgrn_kernel(q_ref, f_ref, v_ref, zs_ref, s0_ref, go_ref, tri_ref, o_ref, sout_ref, st_scr, *, sc, fc):
    c = pl.program_id(1)
    tc = q_ref.shape[0]
    ns = st_scr.shape[0]
    t_seq = tc // ns

    @pl.when(c == 0)
    def _():
        for j in range(ns):
            for h in range(B_HEADS):
                st_scr[j, h] = s0_ref[j, h].T

    f = f_ref[...]
    b_all = _dot_small_int_lhs(tri_ref[...], jnp.log(f), (((1,), (0,)), ((), ())))
    go = go_ref[...]

    def finish(rows, lanes, o):
        o_ref[rows, lanes] = (_head_norm(o, go) * zs_ref[rows, lanes]).astype(o_ref.dtype)

    def exact_rows(r0, n_rows, rebase):
        for h in range(B_HEADS):
            lanes = slice(h * DH, (h + 1) * DH)
            outs = []
            for i in range(n_rows // sc):
                rows = slice(r0 + i * sc, r0 + (i + 1) * sc)
                b = b_all[rows, lanes]
                if rebase and i > 0:
                    b = b - b_all[r0 + i * sc - 1:r0 + i * sc, lanes]
                j = (r0 + i * sc) // t_seq
                o, st = _hgrn_exact(q_ref[rows, lanes], 1.0 - f[rows, lanes], v_ref[rows, lanes], b, st_scr[j, h])
                st_scr[j, h] = st
                outs.append(o)
            finish(slice(r0, r0 + n_rows), lanes, jnp.concatenate(outs, axis=0) if len(outs) > 1 else outs[0])

    if fc is None:
        exact_rows(0, tc, False)
    else:
        for i in range(tc // fc):
            r0 = i * fc
            rows = slice(r0, r0 + fc)
            total = jnp.max(-b_all[r0 + fc - 1:r0 + fc, :])

            @pl.when(total < HGRN_SAFE_LOG_DECAY)
            def _():
                for h in range(B_HEADS):
                    lanes = slice(h * DH, (h + 1) * DH)
                    o, st = _hgrn_factored(q_ref[rows, lanes], 1.0 - f[rows, lanes], v_ref[rows, lanes],
                                           b_all[rows, lanes], st_scr[r0 // t_seq, h])
                    st_scr[r0 // t_seq, h] = st
                    finish(rows, lanes, o)

            @pl.when(jnp.logical_not(total < HGRN_SAFE_LOG_DECAY))
            def _():
                exact_rows(r0, fc, True)

    @pl.when(c == pl.num_programs(1) - 1)
    def _():
        for j in range(ns):
            for h in range(B_HEADS):
                sout_ref[j, h] = st_scr[j, h].T


def _block_tri(tc, blk):
    r = np.arange(tc)
    tri = (r[:, None] // blk == r[None, :] // blk) & (r[None, :] <= r[:, None])
    return jnp.asarray(tri.astype(np.float32), dtype=bf16)


def _hgrn(qb, f, ib, zs, s0, go, n, t, tc, sc, fc=None, ns=1):
    assert tc % sc == 0 and (fc is None or (tc % fc == 0 and fc % sc == 0))
    assert (ns == 1 and t % tc == 0) or (tc == ns * t and n % ns == 0 and fc is None and t % sc == 0)
    nc = max(t // tc, 1)
    rows = lambda i, c: (i * nc + c, 0)
    st = lambda i, c: (i, 0, 0, 0)
    const = lambda i, c: (0, 0)
    return pl.pallas_call(
        functools.partial(_hgrn_kernel, sc=sc, fc=fc),
        grid=(n // ns, nc),
        in_specs=[pl.BlockSpec((tc, B_WIDTH), rows)] * 4 + [
            pl.BlockSpec((ns, B_HEADS, DH, DH), st),
            pl.BlockSpec((1, DH), const),
            pl.BlockSpec((tc, tc), const),
        ],
        out_specs=[pl.BlockSpec((tc, B_WIDTH), rows), pl.BlockSpec((ns, B_HEADS, DH, DH), st)],
        out_shape=[jax.ShapeDtypeStruct((n * t, B_WIDTH), bf16), jax.ShapeDtypeStruct((n, B_HEADS, DH, DH), f32)],
        scratch_shapes=[pltpu.VMEM((ns, B_HEADS, DH, DH), f32)],
        compiler_params=pltpu.CompilerParams(
            dimension_semantics=("arbitrary", "arbitrary"), vmem_limit_bytes=VMEM_LIMIT),
        name="hgrn",
    )(qb, f, ib, zs, s0, go, _block_tri(tc, sc if fc is None else fc))


def _rel_bucket_np(dist):
    n = np.maximum(dist, 0)
    exact = REL_BUCKETS // 2
    scale = np.float32((REL_BUCKETS - exact) / math.log(REL_MAX_DIST / exact))
    large = exact + (np.log(np.maximum(n, exact).astype(np.float32) / np.float32(exact)) * scale).astype(np.int32)
    return np.where(n < exact, n, np.minimum(large, REL_BUCKETS - 1)).astype(np.int32)


def _bucket_starts():
    buckets = _rel_bucket_np(np.arange(2 * REL_MAX_DIST))
    assert np.all(np.diff(buckets) >= 0) and buckets[-1] == REL_BUCKETS - 1
    return [int(np.argmax(buckets >= b)) for b in range(REL_BUCKETS)]


def _bias_table_kernel(rb_ref, o_ref, *, off, a0, a1, sub_far):
    h = pl.program_id(0)
    shape = o_ref.shape[1:]
    d = off + a0 * lax.broadcasted_iota(jnp.int32, shape, 0) + a1 * lax.broadcasted_iota(jnp.int32, shape, 1)
    starts = _bucket_starts()
    far = rb_ref[REL_BUCKETS - 1, h]
    v = jnp.full(shape, far, f32)
    for b in range(REL_BUCKETS - 2, -1, -1):
        v = jnp.where(d < starts[b + 1], rb_ref[b, h], v)
    o_ref[0] = v - far if sub_far else v


def _bias_table(rel_bias, shape, off, a0, a1, sub_far):
    return pl.pallas_call(
        functools.partial(_bias_table_kernel, off=off, a0=a0, a1=a1, sub_far=sub_far),
        grid=(A_HEADS,),
        in_specs=[pl.BlockSpec(memory_space=pltpu.SMEM)],
        out_specs=pl.BlockSpec((1,) + tuple(shape), lambda h: (h, 0, 0)),
        out_shape=jax.ShapeDtypeStruct((A_HEADS,) + tuple(shape), f32),
        compiler_params=pltpu.CompilerParams(dimension_semantics=("arbitrary",)),
        name="bias_table",
    )(rel_bias.astype(f32))


def _cmp_bias_t(rel_bias, t0, nt, n_chunk):
    return _bias_table(rel_bias, (n_chunk, nt), t0 - (CMP_BLOCK - 1), -CMP_STRIDE, 1, False)


def _near_tiles_t(rel_bias):
    return _bias_table(rel_bias, (QB, 2 * QB), 0, -1, 1, True)


def _compress_combine(lhs, wcat, pe2):
    out = jnp.dot(lhs, wcat, preferred_element_type=f32)
    pc = jnp.dot(pe2, wcat, preferred_element_type=f32)
    const = pc[0:1, :DH] + pc[1:2, DH:]
    r = lhs.shape[0]
    return out[:, :DH] + pltpu.roll(out[:, DH:], r - 1, 0) + const


def _compress_kernel(kv_ref, wk_ref, wv_ref, pek_ref, pev_ref, gk_ref, kc_ref, vct_ref):
    nchunk = kv_ref.shape[0] // (4 * CMP_STRIDE)

    def chunk_rows(cg):
        return jnp.concatenate(
            [kv_ref[pl.ds(4 * l + cg, nchunk, stride=4 * CMP_STRIDE), :].astype(bf16) for l in range(CMP_STRIDE)],
            axis=1)

    kc = _compress_combine(jnp.concatenate([chunk_rows(0), chunk_rows(1)], axis=0), wk_ref[...], pek_ref[...])
    vc = _compress_combine(jnp.concatenate([chunk_rows(2), chunk_rows(3)], axis=0), wv_ref[...], pev_ref[...])
    kc = _head_norm(kc, gk_ref[...])
    for g in range(A_KV_GROUPS):
        kc_ref[0, g] = kc[g * nchunk:(g + 1) * nchunk].astype(kc_ref.dtype)
        vct_ref[0, g] = vc[g * nchunk:(g + 1) * nchunk].T.astype(vct_ref.dtype)


def _prep_cmp_weights(w, pe):
    half = CMP_STRIDE * DH
    wcat = jnp.concatenate([w[:half], w[half:]], axis=1).astype(bf16)
    return wcat, pe.reshape(2, half).astype(bf16)


def _compress(kvc, wk, wv, pek, pev, gk, n, t):
    nchunk = t // CMP_STRIDE
    assert nchunk == DH, "the transposed v_c block is square"
    const = lambda i: (0, 0)
    return pl.pallas_call(
        _compress_kernel, grid=(n,),
        in_specs=[pl.BlockSpec((t * 4, DH), lambda i: (i, 0)),
                  pl.BlockSpec(wk.shape, const), pl.BlockSpec(wv.shape, const),
                  pl.BlockSpec(pek.shape, const), pl.BlockSpec(pev.shape, const), pl.BlockSpec((1, DH), const)],
        out_specs=[pl.BlockSpec((1, A_KV_GROUPS, nchunk, DH), lambda i: (i, 0, 0, 0))] * 2,
        out_shape=[jax.ShapeDtypeStruct((n, A_KV_GROUPS, nchunk, DH), bf16)] * 2,
        compiler_params=pltpu.CompilerParams(dimension_semantics=("arbitrary",), vmem_limit_bytes=VMEM_LIMIT),
        name="compress",
    )(kvc, wk, wv, pek, pev, gk)


def _softmax_tile(carry, s, mask, v_bf):
    m, l, acc = carry
    s = jnp.where(mask, s, NEG)
    m_new = jnp.maximum(m, jnp.max(s, axis=-1, keepdims=True))
    alpha = jnp.exp(m - m_new)
    p = jnp.where(mask, jnp.exp(s - m_new), 0.0)
    l = alpha * l + jnp.sum(p, axis=-1, keepdims=True)
    h, r, k = p.shape
    pv = jnp.dot(p.reshape(h * r, k).astype(bf16), v_bf, preferred_element_type=f32).reshape(h, r, DH)
    return m_new, l, alpha * acc + pv


def _softmax_finish(carry):
    m, l, acc = carry
    return acc / jnp.where(l > 0, l, 1.0)


def _select_blocks(score_t, tpos, n_blocks):
    nb = score_t.shape[0]
    j = lax.broadcasted_iota(jnp.int32, score_t.shape, 0)
    valid = (j * SLC_BLOCK <= tpos) & (j < n_blocks)
    cur = tpos >> 6
    forced = (j == 0) | (j == cur) | (j == cur - 1)
    val = jnp.where(valid, jnp.where(forced, 1e30, score_t), -1.0)
    rank = jnp.zeros(score_t.shape, jnp.int32)
    for i in range(n_blocks):
        vi = val[i:i + 1, :]
        beats = (vi > val) | ((vi == val) & (i < j))
        rank = rank + beats.astype(jnp.int32)
    return jnp.where((rank < N_SELECT) & valid, 1.0, 0.0)


QB = 128
NSA_PROMPT_SEQS = 2


NEG_M = -1e30
NEG_S = -2e30


def _nsa_prompt_kernel(q_ref, kc_ref, vct_ref, bc_ref, kvs_ref, kvw_ref, gate_ref, za_ref, bt_ref, mt_ref,
                       o_ref, ks_scr, vts_scr, kw_scr, vtw_scr, km_scr):
    qb = pl.program_id(1)
    q0 = qb * QB
    n_far = jnp.maximum(qb - 1, 0)
    nkt = vts_scr.shape[1]
    nq4 = A_HPG * QB
    nseq = q_ref.shape[0]
    chains = [(s, g) for s in range(nseq) for g in range(A_KV_GROUPS)]
    groups = range(len(chains))

    @pl.when(qb == 0)
    def _():
        for c, (s, g) in enumerate(chains):
            for kt in range(nkt):
                rows = slice(kt * QB, (kt + 1) * QB)
                base = kt * 4 * QB
                ks_scr[c, rows, :] = kvs_ref[s, pl.ds(base + g, QB, stride=4), :].astype(bf16)
                vts_scr[c, kt] = kvs_ref[s, pl.ds(base + 2 + g, QB, stride=4), :].T.astype(bf16)
                kw_scr[c, rows, :] = kvw_ref[s, pl.ds(base + g, QB, stride=4), :].astype(bf16)
                vtw_scr[c, kt] = kvw_ref[s, pl.ds(base + 2 + g, QB, stride=4), :].T.astype(bf16)

    qts = [jnp.concatenate([q_ref[s, :, (g * A_HPG + h) * DH:(g * A_HPG + h + 1) * DH].astype(f32).T
                            for h in range(A_HPG)], axis=1).astype(bf16) for s, g in chains]
    kk = lax.broadcasted_iota(jnp.int32, (QB, nq4), 0)
    tt = lax.broadcasted_iota(jnp.int32, (QB, nq4), 1) & (QB - 1)

    def heads(ref, g, *idx):
        return jnp.concatenate([ref[(g * A_HPG + h,) + idx] for h in range(A_HPG)], axis=1)

    o_c = []
    for c, (s, g) in enumerate(chains):
        sc = jnp.dot(kc_ref[s, g], qts[c], preferred_element_type=f32) + heads(bc_ref, g)
        mask_c = q0 + tt >= CMP_STRIDE * kk + (CMP_BLOCK - 1)
        sc = jnp.where(mask_c, sc, NEG)
        mc = jnp.max(sc, axis=0, keepdims=True)
        ec = jnp.where(mask_c, jnp.exp(sc - mc), 0.0)
        lc = jnp.sum(ec, axis=0, keepdims=True)
        pc = ec / jnp.where(lc > 0, lc, 1.0)
        o_c.append(jnp.dot(vct_ref[s, g], pc.astype(bf16), preferred_element_type=f32))
        ps = pc[:, 0:QB] + pc[:, QB:2 * QB] + pc[:, 2 * QB:3 * QB] + pc[:, 3 * QB:4 * QB]
        score_t = _dot_small_int_lhs(mt_ref[...], ps, (((1,), (0,)), ((), ())))
        nb = score_t.shape[0]
        sel_t = _select_blocks(score_t, q0 + lax.broadcasted_iota(jnp.int32, (nb, QB), 1), nb)
        for j in range(nb):
            km_scr[c, j * SLC_BLOCK:(j + 1) * SLC_BLOCK, :] = jnp.broadcast_to(sel_t[j:j + 1, :], (SLC_BLOCK, QB))

    def tile(carry, qt, k_tile, vt_tile, mask, bias):
        m, l, acc = carry
        s = jnp.dot(k_tile, qt, preferred_element_type=f32)
        if bias is not None:
            s = s + bias
        s = jnp.where(mask, s, NEG_S)
        m_new = jnp.maximum(m, jnp.max(s, axis=0, keepdims=True))
        alpha = jnp.exp(m - m_new)
        p = jnp.exp(s - m_new)
        l = alpha * l + jnp.sum(p, axis=0, keepdims=True)
        return m_new, l, alpha * acc + jnp.dot(vt_tile, p.astype(bf16), preferred_element_type=f32)

    def init():
        return jnp.full((1, nq4), NEG_M, f32), jnp.zeros((1, nq4), f32), jnp.zeros((DH, nq4), f32)

    def finish(carry):
        m, l, acc = carry
        return acc / jnp.where(l > 0, l, 1.0)

    def key_rows(kt):
        return pl.ds(pl.multiple_of(kt * QB, QB), QB)

    def lanes4(x):
        return jnp.concatenate([x] * A_HPG, axis=1)

    def near_span(g, k_scr, vt_scr, rs, use_sel):
        ks, vts, masks, biases = [], [], [], []
        for r in rs:
            kt = jnp.maximum(qb - r, 0)
            ks.append(k_scr[g, key_rows(kt), :])
            vts.append(vt_scr[g, kt])
            if use_sel:
                mask = lanes4(km_scr[g, key_rows(kt), :]) > 0.5
                if r == 0:
                    mask = mask & (kk <= tt)
            else:
                mask = (kk <= tt) if r == 0 else (kk >= tt) if r == WINDOW // QB else (kk >= 0)
            masks.append(mask)
            off = jnp.where(qb - r >= 0, 0.0, NEG_S)
            biases.append(heads(bt_ref, chains[g][1], slice(None), slice(r * QB, (r + 1) * QB)) + off if r < 2
                          else jnp.full((QB, nq4), off, f32))
        cat = lambda xs, axis: jnp.concatenate(xs, axis=axis)
        return cat(ks, 0), cat(vts, 1), cat(masks, 0), cat(biases, 0)

    o_w = [finish(tile(init(), qts[g], *near_span(g, kw_scr, vtw_scr, (4, 3, 2, 1, 0), False))) for g in groups]

    kk2 = lax.broadcasted_iota(jnp.int32, (2 * QB, nq4), 0)

    def far(i, carries):
        rows = pl.ds(pl.multiple_of(i * 2 * QB, 2 * QB), 2 * QB)
        in_range = (kk2 + i * 2 * QB) < n_far * QB
        out = []
        for g in groups:
            mask = (lanes4(km_scr[g, rows, :]) > 0.5) & in_range
            vt = jnp.concatenate([vts_scr[g, 2 * i], vts_scr[g, jnp.minimum(2 * i + 1, nkt - 1)]], axis=1)
            out.append(tile(carries[g], qts[g], ks_scr[g, rows, :], vt, mask, None))
        return tuple(out)

    carries = lax.fori_loop(0, (n_far + 1) // 2, far, tuple(init() for g in groups))
    o_s = [finish(tile(carries[g], qts[g], *near_span(g, ks_scr, vts_scr, (1, 0), True))) for g in groups]

    for s in range(nseq):
        za = za_ref[s]
        outs = []
        for g in range(A_KV_GROUPS):
            c = s * A_KV_GROUPS + g
            gate_t = gate_ref[s, :, g * DH:(g + 1) * DH].T
            for h in range(A_HPG):
                cols = slice(h * QB, (h + 1) * QB)

                def grow(br):
                    return gate_t[br * A_HPG + h:br * A_HPG + h + 1, :]
                o = grow(0) * o_c[c][:, cols] + grow(1) * o_s[c][:, cols] + grow(2) * o_w[c][:, cols]
                outs.append(o.T * za[:, (g * A_HPG + h) * DH:(g * A_HPG + h + 1) * DH])
        o_ref[s] = jnp.concatenate(outs, axis=1).astype(o_ref.dtype)


def _cmp_to_slc_t(n_cmp_pad, n_slc_pad, n_cmp, n_slc):
    c0 = np.arange(n_cmp_pad)[None, :] * CMP_STRIDE
    s0 = np.arange(n_slc_pad)[:, None] * SLC_BLOCK
    ov = np.minimum(c0 + CMP_BLOCK, s0 + SLC_BLOCK) - np.maximum(c0, s0)
    m = np.maximum(ov, 0).astype(np.float32) / CMP_STRIDE
    m[:, n_cmp:] = 0
    m[n_slc:, :] = 0
    return m


def _nsa_prompt(qa, kc, vct, kvs, kvw, gate, za, rel_bias, n, t):
    nq = t // QB
    n_cmp = (t - CMP_BLOCK) // CMP_STRIDE + 1
    n_slc = -(-t // SLC_BLOCK)
    assert kc.shape[2] == QB and n_slc % 8 == 0
    bias_c = _cmp_bias_t(rel_bias, 0, t, QB)
    bt = _near_tiles_t(rel_bias)
    mt = jnp.asarray(_cmp_to_slc_t(QB, n_slc, n_cmp, n_slc), dtype=bf16)
    nseq = NSA_PROMPT_SEQS
    assert n % nseq == 0
    ng = A_KV_GROUPS
    nchain = nseq * ng
    qrow = lambda w: pl.BlockSpec((nseq, QB, w), lambda i, b: (i, b, 0))
    seq4 = lambda shape: pl.BlockSpec((nseq,) + shape, lambda i, b: (i, 0, 0, 0))
    seq3 = lambda shape: pl.BlockSpec((nseq,) + shape, lambda i, b: (i, 0, 0), pipeline_mode=pl.Buffered(1))
    by_seq = lambda a, w: a.reshape(n, -1, w)
    out = pl.pallas_call(
        _nsa_prompt_kernel, grid=(n // nseq, nq),
        in_specs=[
            qrow(A_WIDTH),
            seq4((ng, QB, DH)),
            seq4((ng, DH, QB)),
            pl.BlockSpec((A_HEADS, QB, QB), lambda i, b: (0, 0, b)),
            seq3((t * 4, DH)),
            seq3((t * 4, DH)),
            qrow(ng * DH),
            qrow(A_WIDTH),
            pl.BlockSpec((A_HEADS, QB, 2 * QB), lambda i, b: (0, 0, 0)),
            pl.BlockSpec(mt.shape, lambda i, b: (0, 0)),
        ],
        out_specs=qrow(A_WIDTH),
        out_shape=jax.ShapeDtypeStruct((n, t, A_WIDTH), bf16),
        scratch_shapes=[pltpu.VMEM((nchain, t, DH), bf16), pltpu.VMEM((nchain, nq, DH, QB), bf16),
                        pltpu.VMEM((nchain, t, DH), bf16), pltpu.VMEM((nchain, nq, DH, QB), bf16),
                        pltpu.VMEM((nchain, t, QB), f32)],
        compiler_params=pltpu.CompilerParams(
            dimension_semantics=("arbitrary", "arbitrary"), vmem_limit_bytes=VMEM_LIMIT),
        name="nsa_prompt",
    )(by_seq(qa, A_WIDTH), kc, vct, bias_c, by_seq(kvs, DH), by_seq(kvw, DH), by_seq(gate, ng * DH),
      by_seq(za, A_WIDTH), bt, mt)
    return out.reshape(n * t, A_WIDTH)


CMP_PITCH = 24


def _nsa_sample_kernel(pt_ref, cmp_hbm, slc_hbm, q_ref, kvs_new_ref, kvw_new_ref, win_ref, wk_ref, wv_ref, pek_ref,
                       pev_ref, gk_ref, bc_ref, bl_ref, bn_ref, gate_ref, za_ref, mt_ref, e_ref, o_ref, win_out_ref,
                       xc_scr, cmp_buf, slc_buf, page_sem, *, n_pages, page, t_new, n_blocks):
    i = pl.program_id(0)
    n_seq = pl.num_programs(0)
    slot = i % 2

    def page_copies(seq, sl):
        for p in range(n_pages):
            yield pltpu.make_async_copy(cmp_hbm.at[pt_ref[seq, p]], cmp_buf.at[sl, p], page_sem.at[0, sl])
            yield pltpu.make_async_copy(slc_hbm.at[pt_ref[seq, p]], slc_buf.at[sl, p], page_sem.at[1, sl])

    @pl.when(i == 0)
    def _():
        for cp in page_copies(0, 0):
            cp.start()

    for cp in page_copies(jnp.minimum(i + 1, n_seq - 1), 1 - slot):
        cp.start()
    for cp in page_copies(i, slot):
        cp.wait()
    cmp_pages = [cmp_buf.at[slot, p] for p in range(n_pages)]
    slc_pages = [slc_buf.at[slot, p] for p in range(n_pages)]
    past = n_pages * page
    nchunk = past // CMP_STRIDE
    pad_new = QB
    ng = A_KV_GROUPS
    groups = range(ng)

    cpp = page // CMP_STRIDE
    for p in range(n_pages):
        for cg in range(4):
            x = cmp_pages[p][pl.ds(cg, page, stride=4), :]
            for c in range(cpp):
                r0 = (p * cpp + c) * CMP_PITCH
                xc_scr[cg, r0:r0 + CMP_STRIDE, :] = x[c * CMP_STRIDE:(c + 1) * CMP_STRIDE]

    def chunk_rows(cg):
        return jnp.concatenate(
            [xc_scr[cg, pl.ds(l, nchunk, stride=CMP_PITCH), :].astype(bf16) for l in range(CMP_STRIDE)], axis=1)

    kc = _compress_combine(jnp.concatenate([chunk_rows(0), chunk_rows(1)], axis=0), wk_ref[...], pek_ref[...])
    vc = _compress_combine(jnp.concatenate([chunk_rows(2), chunk_rows(3)], axis=0), wv_ref[...], pev_ref[...])
    kc = _head_norm(kc, gk_ref[...]).astype(bf16)
    vc = vc.astype(bf16)

    qall = q_ref[...]
    qs = [jnp.concatenate([qall[:, (g * A_HPG + h) * DH:(g * A_HPG + h + 1) * DH] for h in range(A_HPG)], axis=0)
          for g in groups]
    zeros_pad = jnp.zeros((pad_new - t_new, DH), bf16)

    def new_rows(ref, c):
        return jnp.concatenate([ref[pl.ds(c, t_new, stride=4), :].astype(bf16), zeros_pad], axis=0)

    def logits(ks):
        return jnp.stack([lax.dot_general(qs[g], ks[g], (((1,), (1,)), ((), ())), preferred_element_type=f32)
                          .reshape(A_HPG, t_new, ks[g].shape[0]) for g in groups], axis=0)

    def softmax_many(items):
        kmax = max(s.shape[-1] for s, _ in items)
        padded = []
        for s, m in items:
            s = jnp.where(m, s, NEG)
            if s.shape[-1] < kmax:
                s = jnp.concatenate([s, jnp.full(s.shape[:-1] + (kmax - s.shape[-1],), NEG, f32)], axis=-1)
            padded.append(s)
        s_all = jnp.stack(padded, axis=0)
        valid = s_all > 0.5 * NEG
        e = jnp.where(valid, jnp.exp(s_all - jnp.max(s_all, axis=-1, keepdims=True)), 0.0)
        l = jnp.sum(e, axis=-1, keepdims=True)
        p = e / jnp.where(l > 0, l, 1.0)
        return [p[i][..., :s.shape[-1]] for i, (s, _) in enumerate(items)]

    def pv(p, vs):
        return jnp.stack([jnp.dot(p[g].reshape(A_HPG * t_new, -1).astype(bf16), vs[g], preferred_element_type=f32)
                          .reshape(A_HPG, t_new, DH) for g in groups], axis=0)

    def tail_bias(s):
        bias = jnp.concatenate([bl_ref[...], bn_ref[...]], axis=-1).reshape(ng, A_HPG, t_new, 2 * QB)
        nk = s.shape[-1]
        return jnp.concatenate([s[..., :nk - 2 * QB], s[..., nk - 2 * QB:] + bias], axis=-1)

    tt = lax.broadcasted_iota(jnp.int32, (t_new, pad_new), 0)
    uu = lax.broadcasted_iota(jnp.int32, (t_new, pad_new), 1)
    new_mask = uu <= tt

    ti = lax.broadcasted_iota(jnp.int32, (t_new, nchunk), 0)
    ci = lax.broadcasted_iota(jnp.int32, (t_new, nchunk), 1)
    cmask = (past + ti >= CMP_STRIDE * ci + (CMP_BLOCK - 1))[None, None]
    sc = logits([kc[g * nchunk:(g + 1) * nchunk] for g in groups]) + bc_ref[...].reshape(ng, A_HPG, t_new, nchunk)
    nw = win_ref.shape[1] // 4
    kw = [jnp.concatenate([win_ref[0, pl.ds(g, nw, stride=4), :].astype(bf16), new_rows(kvw_new_ref, g)], axis=0)
          for g in groups]
    vw = [jnp.concatenate([win_ref[0, pl.ds(2 + g, nw, stride=4), :].astype(bf16), new_rows(kvw_new_ref, 2 + g)], axis=0)
          for g in groups]
    wmask = lax.broadcasted_iota(jnp.int32, (t_new, nw), 1) >= lax.broadcasted_iota(jnp.int32, (t_new, nw), 0)
    pc, pw = softmax_many([(sc, cmask),
                           (tail_bias(logits(kw)), jnp.concatenate([wmask, new_mask], axis=1)[None, None])])
    o_c = pv(pc, [vc[g * nchunk:(g + 1) * nchunk] for g in groups])
    o_w = pv(pw, vw)

    ps = jnp.concatenate([pc[g, 0] + pc[g, 1] + pc[g, 2] + pc[g, 3] for g in groups], axis=0)
    score_t = _dot_small_int_lhs(mt_ref[...], ps, (((1,), (1,)), ((), ())))
    lane = lax.broadcasted_iota(jnp.int32, score_t.shape, 1)
    tpos = past + jnp.where(lane >= t_new, lane - t_new, lane)
    sel_t = _select_blocks(score_t, tpos, n_blocks)
    key_mask = lax.dot_general(sel_t.astype(bf16), e_ref[...], (((0,), (0,)), ((), ())),
                               preferred_element_type=f32)

    new_mask_f = jnp.where(new_mask, 1.0, 0.0)
    sel_mask = jnp.concatenate([key_mask, jnp.concatenate([new_mask_f] * ng, axis=0)], axis=1)
    sel_mask = sel_mask.reshape(ng, 1, t_new, past + pad_new) > 0.5
    ks = [jnp.concatenate([slc_pages[p][pl.ds(g, page, stride=4), :].astype(bf16) for p in range(n_pages)]
                          + [new_rows(kvs_new_ref, g)], axis=0) for g in groups]
    vs = [jnp.concatenate([slc_pages[p][pl.ds(2 + g, page, stride=4), :].astype(bf16) for p in range(n_pages)]
                          + [new_rows(kvs_new_ref, 2 + g)], axis=0) for g in groups]
    o_s = pv(softmax_many([(tail_bias(logits(ks)), sel_mask)])[0], vs)

    gate_all = gate_ref[...]
    za = za_ref[...]
    outs = []
    for g in groups:
        gate = gate_all[:, g * DH:(g + 1) * DH]
        for h in range(A_HPG):
            def gcol(br):
                return gate[:, br * A_HPG + h:br * A_HPG + h + 1]
            o = gcol(0) * o_c[g, h] + gcol(1) * o_s[g, h] + gcol(2) * o_w[g, h]
            outs.append(o * za[:, (g * A_HPG + h) * DH:(g * A_HPG + h + 1) * DH])
    o_ref[...] = jnp.concatenate(outs, axis=1).astype(o_ref.dtype)

    nrow = win_ref.shape[1]
    win_out_ref[0, 0:nrow - 4 * t_new, :] = win_ref[0, 4 * t_new:nrow, :]
    win_out_ref[0, nrow - 4 * t_new:nrow, :] = kvw_new_ref[...]

    @pl.when(i == n_seq - 1)
    def _():
        for cp in page_copies(n_seq - 1, 1 - slot):
            cp.wait()


def _nsa_sample(qa, kvs_new, kvw_new, cache_cmp, cache_slc, cache_win, page_table, wk, wv, pek, pev, gk,
                gate, za, rel_bias, t_new):
    n, n_pages = page_table.shape
    page = cache_cmp.shape[1] // 4
    past = n_pages * page
    wb = cache_win.shape[1] // 4
    assert wb == WINDOW and past % QB == 0 and past >= WINDOW
    nchunk = past // CMP_STRIDE
    n_cmp = (past + t_new - CMP_BLOCK) // CMP_STRIDE + 1
    assert n_cmp < nchunk
    n_slc = -(-(past + t_new) // SLC_BLOCK)
    nb_pad = -(-n_slc // 8) * 8
    bias_c = _bias_table(rel_bias, (t_new, nchunk), past - (CMP_BLOCK - 1), 1, -CMP_STRIDE, False)
    b_last = _bias_table(rel_bias, (t_new, QB), QB, 1, -1, True)
    b_new = _bias_table(rel_bias, (t_new, QB), 0, 1, -1, True)
    mt = jnp.asarray(_cmp_to_slc_t(nchunk, nb_pad, n_cmp, n_slc), dtype=bf16)
    e = jnp.asarray((np.arange(past)[None, :] // SLC_BLOCK == np.arange(nb_pad)[:, None]).astype(np.float32), dtype=bf16)

    rowblk = lambda w: pl.BlockSpec((t_new, w), lambda i, pt: (i, 0))
    full = lambda a: pl.BlockSpec(a.shape, lambda i, pt: (0,) * a.ndim)
    in_specs = ([pl.BlockSpec(memory_space=pl.ANY)] * 2 + [
        rowblk(A_WIDTH),
        pl.BlockSpec((t_new * 4, DH), lambda i, pt: (i, 0)),
        pl.BlockSpec((t_new * 4, DH), lambda i, pt: (i, 0)),
        pl.BlockSpec((1, wb * 4, DH), lambda i, pt: (i, 0, 0)),
        full(wk), full(wv), full(pek), full(pev), full(gk), full(bias_c), full(b_last), full(b_new),
        rowblk(A_KV_GROUPS * DH), rowblk(A_WIDTH), full(mt), full(e)])
    grid_spec = pltpu.PrefetchScalarGridSpec(
        num_scalar_prefetch=1, grid=(n,), in_specs=in_specs,
        out_specs=[rowblk(A_WIDTH), pl.BlockSpec((1, wb * 4, DH), lambda i, pt: (i, 0, 0))],
        scratch_shapes=[pltpu.VMEM((4, nchunk * CMP_PITCH, DH), f32),
                        pltpu.VMEM((2, n_pages, page * 4, DH), f32), pltpu.VMEM((2, n_pages, page * 4, DH), f32),
                        pltpu.SemaphoreType.DMA((2, 2))])
    return pl.pallas_call(
        functools.partial(_nsa_sample_kernel, n_pages=n_pages, page=page, t_new=t_new, n_blocks=n_slc),
        grid_spec=grid_spec,
        out_shape=[jax.ShapeDtypeStruct((n * t_new, A_WIDTH), f32), jax.ShapeDtypeStruct(cache_win.shape, f32)],
        compiler_params=pltpu.CompilerParams(dimension_semantics=("arbitrary",), vmem_limit_bytes=VMEM_LIMIT),
        name="nsa_sample",
    )(page_table, cache_cmp, cache_slc, qa, kvs_new, kvw_new, cache_win,
      wk, wv, pek, pev, gk, bias_c, b_last, b_new, gate, za, mt, e)


def _outproj_kernel(x_ref, oa_ref, ob_ref, w_ref, y_ref):
    acc = jnp.dot(oa_ref[...].astype(bf16), w_ref[0:A_WIDTH, :], preferred_element_type=f32)
    acc = acc + jnp.dot(ob_ref[...].astype(bf16), w_ref[A_WIDTH:, :], preferred_element_type=f32)
    y_ref[...] = x_ref[...] + acc


def _outproj(x2d, oa, ob, w_out_bf, tm):
    m = x2d.shape[0]
    assert m % tm == 0
    row = lambda i: (i, 0)
    return pl.pallas_call(
        _outproj_kernel, grid=(m // tm,),
        in_specs=[pl.BlockSpec((tm, D_MODEL), row), pl.BlockSpec((tm, A_WIDTH), row), pl.BlockSpec((tm, B_WIDTH), row),
                  pl.BlockSpec(w_out_bf.shape, lambda i: (0, 0))],
        out_specs=pl.BlockSpec((tm, D_MODEL), row),
        out_shape=jax.ShapeDtypeStruct((m, D_MODEL), f32),
        compiler_params=pltpu.CompilerParams(dimension_semantics=("arbitrary",), vmem_limit_bytes=VMEM_LIMIT),
        name="outproj",
    )(x2d, oa, ob, w_out_bf)


def kernel(x_prompt, x_sample, cache_kv_cmp, cache_kv_slc, cache_kv_win, state_hgrn, page_table, g_norm, w_in, w_out,
           g_q, g_k_slc, g_k_win, g_k_cmp, w_cmp_k, w_cmp_v, pe_cmp_k, pe_cmp_v, rel_bias, lb_logits, g_o_hgrn):
    depth = w_in.shape[0]
    assert depth == 1, "single-layer trunk"
    nb, t, _ = x_prompt.shape
    ns, ts, _ = x_sample.shape
    row = lambda a: a.astype(f32)[None]
    lower = jnp.cumsum(jax.nn.softmax(lb_logits.astype(f32), axis=0), axis=0)[0]
    w_main, w_gate = _prep_proj_weights(w_in[0])
    w_out_bf = w_out[0].astype(bf16)
    wk, pek = _prep_cmp_weights(w_cmp_k[0], pe_cmp_k[0])
    wv, pev = _prep_cmp_weights(w_cmp_v[0], pe_cmp_v[0])
    proj_args = (row(g_norm[0]), w_main, w_gate, row(g_q[0]), row(g_k_slc[0]), row(g_k_win[0]), row(lower))
    kv6 = lambda a, n_, t_: a.reshape(1, n_, t_, 2, A_KV_GROUPS, DH)

    xp = x_prompt.reshape(nb * t, D_MODEL)
    wlen = min(WINDOW, t)
    assert wlen == PROJ_TM, "the proj row tile doubles as the prompt's final window"
    qa, kvc, kvs, kvw, gate, za, qb, f, ib, zb, win_p = _proj(xp, *proj_args, tm=PROJ_TM, t_seq=t)
    s0 = jnp.zeros((nb, B_HEADS, DH, DH), f32)
    ob, st_p = _hgrn(qb, f, ib, zb, s0, row(g_o_hgrn[0]), nb, t, tc=128, sc=16, fc=64)
    kc, vct = _compress(kvc, wk, wv, pek, pev, row(g_k_cmp[0]), nb, t)
    oa = _nsa_prompt(qa, kc, vct, kvs, kvw, gate, za, rel_bias, nb, t)
    y_p = _outproj(xp, oa, ob, w_out_bf, tm=PROJ_TM).reshape(nb, t, D_MODEL)
    win_p = kv6(win_p, nb, wlen)

    xs = x_sample.reshape(ns * ts, D_MODEL)
    qa, kvc_s, kvs_s, kvw_s, gate, za, qb, f, ib, zb, _ = _proj(xs, *proj_args, tm=PROJ_TM, t_seq=PROJ_TM)
    seq_per_step = 8
    ob, st_s = _hgrn(qb, f, ib, zb, state_hgrn[0].astype(f32), row(g_o_hgrn[0]), ns, ts, tc=seq_per_step * ts, sc=ts,
                     ns=seq_per_step)
    pool = cache_kv_cmp.shape[1]
    page = cache_kv_cmp.shape[2]
    oa, win_s = _nsa_sample(
        qa, kvs_s, kvw_s, cache_kv_cmp[0].reshape(pool, page * 4, DH), cache_kv_slc[0].reshape(pool, page * 4, DH),
        cache_kv_win[0].reshape(ns, -1, DH), page_table, wk, wv, pek, pev, row(g_k_cmp[0]), gate, za, rel_bias, ts)
    y_s = _outproj(xs, oa, ob, w_out_bf, tm=PROJ_TM).reshape(ns, ts, D_MODEL)

    return (y_p, y_s, kv6(kvc, nb, t), kv6(kvs, nb, t), win_p, st_p[None].astype(x_prompt.dtype),
            kv6(kvc_s, ns, ts), kv6(kvs_s, ns, ts), kv6(win_s, ns, WINDOW), st_s[None].astype(state_hgrn.dtype))
```

```python
import functools
import math

import jax
import jax.numpy as jnp
import numpy as np
from jax import lax
from jax.experimental import pallas as pl
from jax.experimental.pallas import tpu as pltpu

f32 = jnp.float32
bf16 = jnp.bfloat16

D_MODEL = 2048
A_HEADS = 8
A_KV_GROUPS = 2
A_HPG = A_HEADS // A_KV_GROUPS
DH = 128
A_WIDTH = A_HEADS * DH
CMP_BLOCK = 32
CMP_STRIDE = 16
SLC_BLOCK = 64
N_SELECT = 16
WINDOW = 512
B_HEADS = 8
B_WIDTH = B_HEADS * DH
REL_BUCKETS = 32
REL_MAX_DIST = 128
EPS = 1e-6
NEG = -1e30

VMEM_LIMIT = 56 * 1024 * 1024


def _sigmoid(x):
    return 1.0 / (1.0 + jnp.exp(-x))


def _head_norm(a, g):
    return a * lax.rsqrt(jnp.mean(a * a, axis=-1, keepdims=True) + EPS) * g


def _dot_small_int_lhs(m_bf, x, dims):
    hi = x.astype(bf16)
    r1 = x - hi.astype(f32)
    mid = r1.astype(bf16)
    lo = (r1 - mid.astype(f32)).astype(bf16)
    d = lambda p: lax.dot_general(m_bf, p, dims, preferred_element_type=f32)
    return d(hi) + d(mid) + d(lo)


PROJ_TN = 512
PROJ_TM = 512
PROJ_SEG = {0: "qa", 1: "qa", 2: "kvc", 3: "kvs", 4: "kvw", 5: "za", 6: "za", 7: "qb", 8: "qb", 9: "f", 10: "f",
            11: "ib", 12: "ib", 13: "zb", 14: "zb"}
PROJ_SLABS = ((0, 5), (5, 11), (11, 15))


def _proj_tiles(h, w_refs, first_tile, tm, out, par):
    for jj, w_ref in enumerate(w_refs):
        j = first_tile + jj
        acc = lax.dot_general(h, w_ref[...], (((1,), (1,)), ((), ())), preferred_element_type=f32)
        name = PROJ_SEG[j]
        k = j - min(t for t, s in PROJ_SEG.items() if s == name)
        lanes = slice(k * PROJ_TN, (k + 1) * PROJ_TN)

        def cols(c):
            return acc[:, c * DH:(c + 1) * DH]

        if name == "qa":
            gq = par["gq"][...]
            out["qa"][:, lanes] = jnp.concatenate(
                [_head_norm(cols(c), gq) * (DH ** -0.5) for c in range(4)], axis=1).astype(out["qa"].dtype)
        elif name in ("kvc", "kvs", "kvw"):
            g_ref = {"kvc": None, "kvs": par["gks"], "kvw": par["gkw"]}[name]
            for c in range(4):
                val = _head_norm(cols(c), g_ref[...]) if (g_ref is not None and c < 2) else cols(c)
                out[name][pl.ds(c, tm, stride=4), :] = val
                if name == "kvw":
                    out["tail"][pl.ds(c, tm, stride=4), :] = val
        elif name in ("za", "zb"):
            out[name][:, lanes] = acc * _sigmoid(acc)
        elif name == "f":
            lb = par["lb"][:, lanes]
            out["f"][:, lanes] = lb + (1.0 - lb) * _sigmoid(acc)
        else:
            out[name][:, lanes] = acc


def _proj_first_kernel(x_ref, gn_ref, *refs):
    first, last = PROJ_SLABS[0]
    w_refs = refs[:last - first]
    (wg_ref, gq_ref, gks_ref, gkw_ref,
     h_ref, qa_ref, kvc_ref, kvs_ref, kvw_ref, tail_ref, gate_ref) = refs[last - first:]
    x = x_ref[...]
    h = (x * lax.rsqrt(jnp.mean(x * x, axis=-1, keepdims=True) + EPS) * gn_ref[...]).astype(bf16)
    h_ref[...] = h
    gate_ref[...] = _sigmoid(lax.dot_general(h, wg_ref[...], (((1,), (1,)), ((), ())), preferred_element_type=f32))
    _proj_tiles(h, w_refs, first, x_ref.shape[0],
                dict(qa=qa_ref, kvc=kvc_ref, kvs=kvs_ref, kvw=kvw_ref, tail=tail_ref),
                dict(gq=gq_ref, gks=gks_ref, gkw=gkw_ref))


def _proj_rest_kernel(h_ref, *refs, slab):
    first, last = PROJ_SLABS[slab]
    w_refs, lb_ref, out_refs = refs[:last - first], refs[last - first], refs[last - first + 1:]
    names = list(dict.fromkeys(PROJ_SEG[j] for j in range(first, last)))
    _proj_tiles(h_ref[...], w_refs, first, h_ref.shape[0], dict(zip(names, out_refs)), dict(lb=lb_ref))


def _proj(x2d, gn, w_main, w_gate, gq, gks, gkw, lb, tm, t_seq):
    m = x2d.shape[0]
    assert m % tm == 0 and t_seq % tm == 0
    tiles_per_seq = t_seq // tm
    row = lambda i: (i, 0)
    const = lambda i: (0, 0)
    w_specs = lambda slab: [pl.BlockSpec((PROJ_TN, D_MODEL), lambda i, j=j: (j, 0)) for j in range(*PROJ_SLABS[slab])]
    w_args = lambda slab: [w_main] * (PROJ_SLABS[slab][1] - PROJ_SLABS[slab][0])
    params = pltpu.CompilerParams(dimension_semantics=("arbitrary",), vmem_limit_bytes=VMEM_LIMIT)
    kv_shape = jax.ShapeDtypeStruct((m * 4, DH), f32)
    wide = lambda dt: jax.ShapeDtypeStruct((m, A_WIDTH), dt)
    h, qa, kvc, kvs, kvw, tail, gate = pl.pallas_call(
        _proj_first_kernel, grid=(m // tm,),
        in_specs=[pl.BlockSpec((tm, D_MODEL), row), pl.BlockSpec((1, D_MODEL), const)] + w_specs(0)
        + [pl.BlockSpec(w_gate.shape, const),
           pl.BlockSpec((1, DH), const), pl.BlockSpec((1, DH), const), pl.BlockSpec((1, DH), const)],
        out_specs=[pl.BlockSpec((tm, D_MODEL), row), pl.BlockSpec((tm, A_WIDTH), row)]
        + [pl.BlockSpec((tm * 4, DH), row)] * 3
        + [pl.BlockSpec((tm * 4, DH), lambda i: (i // tiles_per_seq, 0)), pl.BlockSpec((tm, A_KV_GROUPS * DH), row)],
        out_shape=[jax.ShapeDtypeStruct((m, D_MODEL), bf16), wide(bf16), kv_shape, kv_shape, kv_shape,
                   jax.ShapeDtypeStruct((m // tiles_per_seq * 4, DH), f32),
                   jax.ShapeDtypeStruct((m, A_KV_GROUPS * DH), f32)],
        compiler_params=params, name="proj_a",
    )(x2d, gn, *w_args(0), w_gate, gq, gks, gkw)

    def rest(slab, n_out):
        return pl.pallas_call(
            functools.partial(_proj_rest_kernel, slab=slab), grid=(m // tm,),
            in_specs=[pl.BlockSpec((tm, D_MODEL), row)] + w_specs(slab) + [pl.BlockSpec((1, B_WIDTH), const)],
            out_specs=[pl.BlockSpec((tm, A_WIDTH), row)] * n_out, out_shape=[wide(f32)] * n_out,
            compiler_params=params, name="proj_" + "abc"[slab],
        )(h, *w_args(slab), lb)

    za, qb, f = rest(1, 3)
    ib, zb = rest(2, 2)
    return qa, kvc, kvs, kvw, gate, za, qb, f, ib, zb, tail


def _prep_proj_weights(w_in):
    a0 = A_WIDTH + 6 * A_KV_GROUPS * DH
    a1 = a0 + 3 * A_HEADS
    w_t = w_in.T
    n_tiles = PROJ_SLABS[-1][1]
    first_after_gate = a0 // PROJ_TN

    def cast_tile(w_ref, o_ref):
        o_ref[...] = w_ref[...].astype(bf16)

    w_main = pl.pallas_call(
        cast_tile, grid=(n_tiles,),
        in_specs=[pl.BlockSpec((pl.Element(PROJ_TN), pl.Element(D_MODEL)),
                               lambda j: (pl.multiple_of(jnp.where(j < first_after_gate, 0, a1 - a0) + j * PROJ_TN, 8), 0))],
        out_specs=pl.BlockSpec((PROJ_TN, D_MODEL), lambda j: (j, 0)),
        out_shape=jax.ShapeDtypeStruct((n_tiles * PROJ_TN, D_MODEL), bf16),
        compiler_params=pltpu.CompilerParams(dimension_semantics=("arbitrary",)),
        name="cast_w",
    )(w_t.astype(f32))
    wg = w_t[a0:a1].reshape(3, A_KV_GROUPS, A_HPG, -1).transpose(1, 0, 2, 3).reshape(A_KV_GROUPS, 3 * A_HPG, -1)
    w_gate = jnp.pad(wg, ((0, 0), (0, DH - 3 * A_HPG), (0, 0))).reshape(A_KV_GROUPS * DH, -1).astype(bf16)
    return w_main, w_gate


HGRN_SAFE_LOG_DECAY = 80.0


def _hgrn_exact(q, k, v, b, st):
    sc = q.shape[0]
    t_idx = lax.broadcasted_iota(jnp.int32, (sc, DH), 0)
    o = lax.dot_general((q * jnp.exp(b)).astype(bf16), st.astype(bf16), (((1,), (1,)), ((), ())),
                        preferred_element_type=f32)
    for s in range(sc):
        e = jnp.exp(jnp.minimum(b - b[s:s + 1, :], 0.0))
        a = jnp.where(t_idx >= s, q * e * k[s:s + 1, :], 0.0)
        o = o + jnp.sum(a, axis=-1, keepdims=True) * v[s:s + 1, :]
    bl = b[sc - 1:sc, :]
    ut = lax.dot_general(v.astype(bf16), (k * jnp.exp(bl - b)).astype(bf16), (((0,), (0,)), ((), ())),
                         preferred_element_type=f32)
    return o, st * jnp.exp(bl) + ut


def _hgrn_factored(q, k, v, b, st):
    fc = q.shape[0]
    qd = (q * jnp.exp(b)).astype(bf16)
    att = lax.dot_general(qd, (k * jnp.exp(-b)).astype(bf16), (((1,), (1,)), ((), ())), preferred_element_type=f32)
    tri = lax.broadcasted_iota(jnp.int32, (fc, fc), 0) >= lax.broadcasted_iota(jnp.int32, (fc, fc), 1)
    att = jnp.where(tri, att, 0.0)
    o = jnp.dot(att.astype(bf16), v.astype(bf16), preferred_element_type=f32)
    o = o + lax.dot_general(qd, st.astype(bf16), (((1,), (1,)), ((), ())), preferred_element_type=f32)
    bl = b[fc - 1:fc, :]
    ut = lax.dot_general(v.astype(bf16), (k * jnp.exp(bl - b)).astype(bf16), (((0,), (0,)), ((), ())),
                         preferred_element_type=f32)
    return o, st * jnp.exp(bl) + ut


def _hgrn_kernel(q_ref, f_ref, v_ref, zs_ref, s0_ref, go_ref, tri_ref, o_ref, sout_ref, st_scr, *, sc, fc):
    c = pl.program_id(1)
    tc = q_ref.shape[0]
    ns = st_scr.shape[0]
    t_seq = tc // ns

    @pl.when(c == 0)
    def _():
        for j in range(ns):
            for h in range(B_HEADS):
                st_scr[j, h] = s0_ref[j, h].T

    f = f_ref[...]
    b_all = _dot_small_int_lhs(tri_ref[...], jnp.log(f), (((1,), (0,)), ((), ())))
    go = go_ref[...]

    def finish(rows, lanes, o):
        o_ref[rows, lanes] = (_head_norm(o, go) * zs_ref[rows, lanes]).astype(o_ref.dtype)

    def exact_rows(r0, n_rows, rebase):
        for h in range(B_HEADS):
            lanes = slice(h * DH, (h + 1) * DH)
            outs = []
            for i in range(n_rows // sc):
                rows = slice(r0 + i * sc, r0 + (i + 1) * sc)
                b = b_all[rows, lanes]
                if rebase and i > 0:
                    b = b - b_all[r0 + i * sc - 1:r0 + i * sc, lanes]
                j = (r0 + i * sc) // t_seq
                o, st = _hgrn_exact(q_ref[rows, lanes], 1.0 - f[rows, lanes], v_ref[rows, lanes], b, st_scr[j, h])
                st_scr[j, h] = st
                outs.append(o)
            finish(slice(r0, r0 + n_rows), lanes, jnp.concatenate(outs, axis=0) if len(outs) > 1 else outs[0])

    if fc is None:
        exact_rows(0, tc, False)
    else:
        for i in range(tc // fc):
            r0 = i * fc
            rows = slice(r0, r0 + fc)
            total = jnp.max(-b_all[r0 + fc - 1:r0 + fc, :])

            @pl.when(total < HGRN_SAFE_LOG_DECAY)
            def _():
                for h in range(B_HEADS):
                    lanes = slice(h * DH, (h + 1) * DH)
                    o, st = _hgrn_factored(q_ref[rows, lanes], 1.0 - f[rows, lanes], v_ref[rows, lanes],
                                           b_all[rows, lanes], st_scr[r0 // t_seq, h])
                    st_scr[r0 // t_seq, h] = st
                    finish(rows, lanes, o)

            @pl.when(jnp.logical_not(total < HGRN_SAFE_LOG_DECAY))
            def _():
                exact_rows(r0, fc, True)

    @pl.when(c == pl.num_programs(1) - 1)
    def _():
        for j in range(ns):
            for h in range(B_HEADS):
                sout_ref[j, h] = st_scr[j, h].T


def _block_tri(tc, blk):
    r = np.arange(tc)
    tri = (r[:, None] // blk == r[None, :] // blk) & (r[None, :] <= r[:, None])
    return jnp.asarray(tri.astype(np.float32), dtype=bf16)


def _hgrn(qb, f, ib, zs, s0, go, n, t, tc, sc, fc=None, ns=1):
    assert tc % sc == 0 and (fc is None or (tc % fc == 0 and fc % sc == 0))
    assert (ns == 1 and t % tc == 0) or (tc == ns * t and n % ns == 0 and fc is None and t % sc == 0)
    nc = max(t // tc, 1)
    rows = lambda i, c: (i * nc + c, 0)
    st = lambda i, c: (i, 0, 0, 0)
    const = lambda i, c: (0, 0)
    return pl.pallas_call(
        functools.partial(_hgrn_kernel, sc=sc, fc=fc),
        grid=(n // ns, nc),
        in_specs=[pl.BlockSpec((tc, B_WIDTH), rows)] * 4 + [
            pl.BlockSpec((ns, B_HEADS, DH, DH), st),
            pl.BlockSpec((1, DH), const),
            pl.BlockSpec((tc, tc), const),
        ],
        out_specs=[pl.BlockSpec((tc, B_WIDTH), rows), pl.BlockSpec((ns, B_HEADS, DH, DH), st)],
        out_shape=[jax.ShapeDtypeStruct((n * t, B_WIDTH), bf16), jax.ShapeDtypeStruct((n, B_HEADS, DH, DH), f32)],
        scratch_shapes=[pltpu.VMEM((ns, B_HEADS, DH, DH), f32)],
        compiler_params=pltpu.CompilerParams(
            dimension_semantics=("arbitrary", "arbitrary"), vmem_limit_bytes=VMEM_LIMIT),
        name="hgrn",
    )(qb, f, ib, zs, s0, go, _block_tri(tc, sc if fc is None else fc))


def _rel_bucket_np(dist):
    n = np.maximum(dist, 0)
    exact = REL_BUCKETS // 2
    scale = np.float32((REL_BUCKETS - exact) / math.log(REL_MAX_DIST / exact))
    large = exact + (np.log(np.maximum(n, exact).astype(np.float32) / np.float32(exact)) * scale).astype(np.int32)
    return np.where(n < exact, n, np.minimum(large, REL_BUCKETS - 1)).astype(np.int32)


def _bucket_starts():
    buckets = _rel_bucket_np(np.arange(2 * REL_MAX_DIST))
    assert np.all(np.diff(buckets) >= 0) and buckets[-1] == REL_BUCKETS - 1
    return [int(np.argmax(buckets >= b)) for b in range(REL_BUCKETS)]


def _bias_table_kernel(rb_ref, o_ref, *, off, a0, a1, sub_far):
    h = pl.program_id(0)
    shape = o_ref.shape[1:]
    d = off + a0 * lax.broadcasted_iota(jnp.int32, shape, 0) + a1 * lax.broadcasted_iota(jnp.int32, shape, 1)
    starts = _bucket_starts()
    far = rb_ref[REL_BUCKETS - 1, h]
    v = jnp.full(shape, far, f32)
    for b in range(REL_BUCKETS - 2, -1, -1):
        v = jnp.where(d < starts[b + 1], rb_ref[b, h], v)
    o_ref[0] = v - far if sub_far else v


def _bias_table(rel_bias, shape, off, a0, a1, sub_far):
    return pl.pallas_call(
        functools.partial(_bias_table_kernel, off=off, a0=a0, a1=a1, sub_far=sub_far),
        grid=(A_HEADS,),
        in_specs=[pl.BlockSpec(memory_space=pltpu.SMEM)],
        out_specs=pl.BlockSpec((1,) + tuple(shape), lambda h: (h, 0, 0)),
        out_shape=jax.ShapeDtypeStruct((A_HEADS,) + tuple(shape), f32),
        compiler_params=pltpu.CompilerParams(dimension_semantics=("arbitrary",)),
        name="bias_table",
    )(rel_bias.astype(f32))


def _cmp_bias_t(rel_bias, t0, nt, n_chunk):
    return _bias_table(rel_bias, (n_chunk, nt), t0 - (CMP_BLOCK - 1), -CMP_STRIDE, 1, False)


def _near_tiles_t(rel_bias):
    return _bias_table(rel_bias, (QB, 2 * QB), 0, -1, 1, True)


def _compress_combine(lhs, wcat, pe2):
    out = jnp.dot(lhs, wcat, preferred_element_type=f32)
    pc = jnp.dot(pe2, wcat, preferred_element_type=f32)
    const = pc[0:1, :DH] + pc[1:2, DH:]
    r = lhs.shape[0]
    return out[:, :DH] + pltpu.roll(out[:, DH:], r - 1, 0) + const


def _compress_kernel(kv_ref, wk_ref, wv_ref, pek_ref, pev_ref, gk_ref, kc_ref, vct_ref):
    nchunk = kv_ref.shape[0] // (4 * CMP_STRIDE)

    def chunk_rows(cg):
        return jnp.concatenate(
            [kv_ref[pl.ds(4 * l + cg, nchunk, stride=4 * CMP_STRIDE), :].astype(bf16) for l in range(CMP_STRIDE)],
            axis=1)

    kc = _compress_combine(jnp.concatenate([chunk_rows(0), chunk_rows(1)], axis=0), wk_ref[...], pek_ref[...])
    vc = _compress_combine(jnp.concatenate([chunk_rows(2), chunk_rows(3)], axis=0), wv_ref[...], pev_ref[...])
    kc = _head_norm(kc, gk_ref[...])
    for g in range(A_KV_GROUPS):
        kc_ref[0, g] = kc[g * nchunk:(g + 1) * nchunk].astype(kc_ref.dtype)
        vct_ref[0, g] = vc[g * nchunk:(g + 1) * nchunk].T.astype(vct_ref.dtype)


def _prep_cmp_weights(w, pe):
    half = CMP_STRIDE * DH
    wcat = jnp.concatenate([w[:half], w[half:]], axis=1).astype(bf16)
    return wcat, pe.reshape(2, half).astype(bf16)


def _compress(kvc, wk, wv, pek, pev, gk, n, t):
    nchunk = t // CMP_STRIDE
    assert nchunk == DH, "the transposed v_c block is square"
    const = lambda i: (0, 0)
    return pl.pallas_call(
        _compress_kernel, grid=(n,),
        in_specs=[pl.BlockSpec((t * 4, DH), lambda i: (i, 0)),
                  pl.BlockSpec(wk.shape, const), pl.BlockSpec(wv.shape, const),
                  pl.BlockSpec(pek.shape, const), pl.BlockSpec(pev.shape, const), pl.BlockSpec((1, DH), const)],
        out_specs=[pl.BlockSpec((1, A_KV_GROUPS, nchunk, DH), lambda i: (i, 0, 0, 0))] * 2,
        out_shape=[jax.ShapeDtypeStruct((n, A_KV_GROUPS, nchunk, DH), bf16)] * 2,
        compiler_params=pltpu.CompilerParams(dimension_semantics=("arbitrary",), vmem_limit_bytes=VMEM_LIMIT),
        name="compress",
    )(kvc, wk, wv, pek, pev, gk)


def _softmax_tile(carry, s, mask, v_bf):
    m, l, acc = carry
    s = jnp.where(mask, s, NEG)
    m_new = jnp.maximum(m, jnp.max(s, axis=-1, keepdims=True))
    alpha = jnp.exp(m - m_new)
    p = jnp.where(mask, jnp.exp(s - m_new), 0.0)
    l = alpha * l + jnp.sum(p, axis=-1, keepdims=True)
    h, r, k = p.shape
    pv = jnp.dot(p.reshape(h * r, k).astype(bf16), v_bf, preferred_element_type=f32).reshape(h, r, DH)
    return m_new, l, alpha * acc + pv


def _softmax_finish(carry):
    m, l, acc = carry
    return acc / jnp.where(l > 0, l, 1.0)


def _select_blocks(score_t, tpos, n_blocks):
    nb = score_t.shape[0]
    j = lax.broadcasted_iota(jnp.int32, score_t.shape, 0)
    valid = (j * SLC_BLOCK <= tpos) & (j < n_blocks)
    cur = tpos >> 6
    forced = (j == 0) | (j == cur) | (j == cur - 1)
    val = jnp.where(valid, jnp.where(forced, 1e30, score_t), -1.0)
    rank = jnp.zeros(score_t.shape, jnp.int32)
    for i in range(n_blocks):
        vi = val[i:i + 1, :]
        beats = (vi > val) | ((vi == val) & (i < j))
        rank = rank + beats.astype(jnp.int32)
    return jnp.where((rank < N_SELECT) & valid, 1.0, 0.0)


QB = 128
NSA_PROMPT_SEQS = 2


NEG_M = -1e30
NEG_S = -2e30


def _nsa_prompt_kernel(q_ref, kc_ref, vct_ref, bc_ref, kvs_ref, kvw_ref, gate_ref, za_ref, bt_ref, mt_ref,
                       o_ref, ks_scr, vts_scr, kw_scr, vtw_scr, km_scr):
    qb = pl.program_id(1)
    q0 = qb * QB
    n_far = jnp.maximum(qb - 1, 0)
    nkt = vts_scr.shape[1]
    nq4 = A_HPG * QB
    nseq = q_ref.shape[0]
    chains = [(s, g) for s in range(nseq) for g in range(A_KV_GROUPS)]
    groups = range(len(chains))

    @pl.when(qb == 0)
    def _():
        for c, (s, g) in enumerate(chains):
            for kt in range(nkt):
                rows = slice(kt * QB, (kt + 1) * QB)
                base = kt * 4 * QB
                ks_scr[c, rows, :] = kvs_ref[s, pl.ds(base + g, QB, stride=4), :].astype(bf16)
                vts_scr[c, kt] = kvs_ref[s, pl.ds(base + 2 + g, QB, stride=4), :].T.astype(bf16)
                kw_scr[c, rows, :] = kvw_ref[s, pl.ds(base + g, QB, stride=4), :].astype(bf16)
                vtw_scr[c, kt] = kvw_ref[s, pl.ds(base + 2 + g, QB, stride=4), :].T.astype(bf16)

    qts = [jnp.concatenate([q_ref[s, :, (g * A_HPG + h) * DH:(g * A_HPG + h + 1) * DH].astype(f32).T
                            for h in range(A_HPG)], axis=1).astype(bf16) for s, g in chains]
    kk = lax.broadcasted_iota(jnp.int32, (QB, nq4), 0)
    tt = lax.broadcasted_iota(jnp.int32, (QB, nq4), 1) & (QB - 1)

    def heads(ref, g, *idx):
        return jnp.concatenate([ref[(g * A_HPG + h,) + idx] for h in range(A_HPG)], axis=1)

    o_c = []
    for c, (s, g) in enumerate(chains):
        sc = jnp.dot(kc_ref[s, g], qts[c], preferred_element_type=f32) + heads(bc_ref, g)
        mask_c = q0 + tt >= CMP_STRIDE * kk + (CMP_BLOCK - 1)
        sc = jnp.where(mask_c, sc, NEG)
        mc = jnp.max(sc, axis=0, keepdims=True)
        ec = jnp.where(mask_c, jnp.exp(sc - mc), 0.0)
        lc = jnp.sum(ec, axis=0, keepdims=True)
        pc = ec / jnp.where(lc > 0, lc, 1.0)
        o_c.append(jnp.dot(vct_ref[s, g], pc.astype(bf16), preferred_element_type=f32))
        ps = pc[:, 0:QB] + pc[:, QB:2 * QB] + pc[:, 2 * QB:3 * QB] + pc[:, 3 * QB:4 * QB]
        score_t = _dot_small_int_lhs(mt_ref[...], ps, (((1,), (0,)), ((), ())))
        nb = score_t.shape[0]
        sel_t = _select_blocks(score_t, q0 + lax.broadcasted_iota(jnp.int32, (nb, QB), 1), nb)
        for j in range(nb):
            km_scr[c, j * SLC_BLOCK:(j + 1) * SLC_BLOCK, :] = jnp.broadcast_to(sel_t[j:j + 1, :], (SLC_BLOCK, QB))

    def tile(carry, qt, k_tile, vt_tile, mask, bias):
        m, l, acc = carry
        s = jnp.dot(k_tile, qt, preferred_element_type=f32)
        if bias is not None:
            s = s + bias
        s = jnp.where(mask, s, NEG_S)
        m_new = jnp.maximum(m, jnp.max(s, axis=0, keepdims=True))
        alpha = jnp.exp(m - m_new)
        p = jnp.exp(s - m_new)
        l = alpha * l + jnp.sum(p, axis=0, keepdims=True)
        return m_new, l, alpha * acc + jnp.dot(vt_tile, p.astype(bf16), preferred_element_type=f32)

    def init():
        return jnp.full((1, nq4), NEG_M, f32), jnp.zeros((1, nq4), f32), jnp.zeros((DH, nq4), f32)

    def finish(carry):
        m, l, acc = carry
        return acc / jnp.where(l > 0, l, 1.0)

    def key_rows(kt):
        return pl.ds(pl.multiple_of(kt * QB, QB), QB)

    def lanes4(x):
        return jnp.concatenate([x] * A_HPG, axis=1)

    def near_span(g, k_scr, vt_scr, rs, use_sel):
        ks, vts, masks, biases = [], [], [], []
        for r in rs:
            kt = jnp.maximum(qb - r, 0)
            ks.append(k_scr[g, key_rows(kt), :])
            vts.append(vt_scr[g, kt])
            if use_sel:
                mask = lanes4(km_scr[g, key_rows(kt), :]) > 0.5
                if r == 0:
                    mask = mask & (kk <= tt)
            else:
                mask = (kk <= tt) if r == 0 else (kk >= tt) if r == WINDOW // QB else (kk >= 0)
            masks.append(mask)
            off = jnp.where(qb - r >= 0, 0.0, NEG_S)
            biases.append(heads(bt_ref, chains[g][1], slice(None), slice(r * QB, (r + 1) * QB)) + off if r < 2
                          else jnp.full((QB, nq4), off, f32))
        cat = lambda xs, axis: jnp.concatenate(xs, axis=axis)
        return cat(ks, 0), cat(vts, 1), cat(masks, 0), cat(biases, 0)

    o_w = [finish(tile(init(), qts[g], *near_span(g, kw_scr, vtw_scr, (4, 3, 2, 1, 0), False))) for g in groups]

    kk2 = lax.broadcasted_iota(jnp.int32, (2 * QB, nq4), 0)

    def far(i, carries):
        rows = pl.ds(pl.multiple_of(i * 2 * QB, 2 * QB), 2 * QB)
        in_range = (kk2 + i * 2 * QB) < n_far * QB
        out = []
        for g in groups:
            mask = (lanes4(km_scr[g, rows, :]) > 0.5) & in_range
            vt = jnp.concatenate([vts_scr[g, 2 * i], vts_scr[g, jnp.minimum(2 * i + 1, nkt - 1)]], axis=1)
            out.append(tile(carries[g], qts[g], ks_scr[g, rows, :], vt, mask, None))
        return tuple(out)

    carries = lax.fori_loop(0, (n_far + 1) // 2, far, tuple(init() for g in groups))
    o_s = [finish(tile(carries[g], qts[g], *near_span(g, ks_scr, vts_scr, (1, 0), True))) for g in groups]

    for s in range(nseq):
        za = za_ref[s]
        outs = []
        for g in range(A_KV_GROUPS):
            c = s * A_KV_GROUPS + g
            gate_t = gate_ref[s, :, g * DH:(g + 1) * DH].T
            for h in range(A_HPG):
                cols = slice(h * QB, (h + 1) * QB)

                def grow(br):
                    return gate_t[br * A_HPG + h:br * A_HPG + h + 1, :]
                o = grow(0) * o_c[c][:, cols] + grow(1) * o_s[c][:, cols] + grow(2) * o_w[c][:, cols]
                outs.append(o.T * za[:, (g * A_HPG + h) * DH:(g * A_HPG + h + 1) * DH])
        o_ref[s] = jnp.concatenate(outs, axis=1).astype(o_ref.dtype)


def _cmp_to_slc_t(n_cmp_pad, n_slc_pad, n_cmp, n_slc):
    c0 = np.arange(n_cmp_pad)[None, :] * CMP_STRIDE
    s0 = np.arange(n_slc_pad)[:, None] * SLC_BLOCK
    ov = np.minimum(c0 + CMP_BLOCK, s0 + SLC_BLOCK) - np.maximum(c0, s0)
    m = np.maximum(ov, 0).astype(np.float32) / CMP_STRIDE
    m[:, n_cmp:] = 0
    m[n_slc:, :] = 0
    return m


def _nsa_prompt(qa, kc, vct, kvs, kvw, gate, za, rel_bias, n, t):
    nq = t // QB
    n_cmp = (t - CMP_BLOCK) // CMP_STRIDE + 1
    n_slc = -(-t // SLC_BLOCK)
    assert kc.shape[2] == QB and n_slc % 8 == 0
    bias_c = _cmp_bias_t(rel_bias, 0, t, QB)
    bt = _near_tiles_t(rel_bias)
    mt = jnp.asarray(_cmp_to_slc_t(QB, n_slc, n_cmp, n_slc), dtype=bf16)
    nseq = NSA_PROMPT_SEQS
    assert n % nseq == 0
    ng = A_KV_GROUPS
    nchain = nseq * ng
    qrow = lambda w: pl.BlockSpec((nseq, QB, w), lambda i, b: (i, b, 0))
    seq4 = lambda shape: pl.BlockSpec((nseq,) + shape, lambda i, b: (i, 0, 0, 0))
    seq3 = lambda shape: pl.BlockSpec((nseq,) + shape, lambda i, b: (i, 0, 0), pipeline_mode=pl.Buffered(1))
    by_seq = lambda a, w: a.reshape(n, -1, w)
    out = pl.pallas_call(
        _nsa_prompt_kernel, grid=(n // nseq, nq),
        in_specs=[
            qrow(A_WIDTH),
            seq4((ng, QB, DH)),
            seq4((ng, DH, QB)),
            pl.BlockSpec((A_HEADS, QB, QB), lambda i, b: (0, 0, b)),
            seq3((t * 4, DH)),
            seq3((t * 4, DH)),
            qrow(ng * DH),
            qrow(A_WIDTH),
            pl.BlockSpec((A_HEADS, QB, 2 * QB), lambda i, b: (0, 0, 0)),
            pl.BlockSpec(mt.shape, lambda i, b: (0, 0)),
        ],
        out_specs=qrow(A_WIDTH),
        out_shape=jax.ShapeDtypeStruct((n, t, A_WIDTH), bf16),
        scratch_shapes=[pltpu.VMEM((nchain, t, DH), bf16), pltpu.VMEM((nchain, nq, DH, QB), bf16),
                        pltpu.VMEM((nchain, t, DH), bf16), pltpu.VMEM((nchain, nq, DH, QB), bf16),
                        pltpu.VMEM((nchain, t, QB), f32)],
        compiler_params=pltpu.CompilerParams(
            dimension_semantics=("arbitrary", "arbitrary"), vmem_limit_bytes=VMEM_LIMIT),
        name="nsa_prompt",
    )(by_seq(qa, A_WIDTH), kc, vct, bias_c, by_seq(kvs, DH), by_seq(kvw, DH), by_seq(gate, ng * DH),
      by_seq(za, A_WIDTH), bt, mt)
    return out.reshape(n * t, A_WIDTH)


CMP_PITCH = 24


def _nsa_sample_kernel(pt_ref, *refs, n_pages, page, t_new, n_blocks):
    del pt_ref
    cmp_pages = refs[:n_pages]
    slc_pages = refs[n_pages:2 * n_pages]
    (q_ref, kvs_new_ref, kvw_new_ref, win_ref, wk_ref, wv_ref, pek_ref, pev_ref, gk_ref, bc_ref, bl_ref, bn_ref,
     gate_ref, za_ref, mt_ref, e_ref, o_ref, win_out_ref, xc_scr) = refs[2 * n_pages:]
    past = n_pages * page
    nchunk = past // CMP_STRIDE
    pad_new = QB
    ng = A_KV_GROUPS
    groups = range(ng)

    cpp = page // CMP_STRIDE
    for p in range(n_pages):
        for cg in range(4):
            x = cmp_pages[p][0, pl.ds(cg, page, stride=4), :]
            for c in range(cpp):
                r0 = (p * cpp + c) * CMP_PITCH
                xc_scr[cg, r0:r0 + CMP_STRIDE, :] = x[c * CMP_STRIDE:(c + 1) * CMP_STRIDE]

    def chunk_rows(cg):
        return jnp.concatenate(
            [xc_scr[cg, pl.ds(l, nchunk, stride=CMP_PITCH), :].astype(bf16) for l in range(CMP_STRIDE)], axis=1)

    kc = _compress_combine(jnp.concatenate([chunk_rows(0), chunk_rows(1)], axis=0), wk_ref[...], pek_ref[...])
    vc = _compress_combine(jnp.concatenate([chunk_rows(2), chunk_rows(3)], axis=0), wv_ref[...], pev_ref[...])
    kc = _head_norm(kc, gk_ref[...]).astype(bf16)
    vc = vc.astype(bf16)

    qall = q_ref[...]
    qs = [jnp.concatenate([qall[:, (g * A_HPG + h) * DH:(g * A_HPG + h + 1) * DH] for h in range(A_HPG)], axis=0)
          for g in groups]
    zeros_pad = jnp.zeros((pad_new - t_new, DH), bf16)

    def new_rows(ref, c):
        return jnp.concatenate([ref[pl.ds(c, t_new, stride=4), :].astype(bf16), zeros_pad], axis=0)

    def logits(ks):
        return jnp.stack([lax.dot_general(qs[g], ks[g], (((1,), (1,)), ((), ())), preferred_element_type=f32)
                          .reshape(A_HPG, t_new, ks[g].shape[0]) for g in groups], axis=0)

    def softmax_many(items):
        kmax = max(s.shape[-1] for s, _ in items)
        padded = []
        for s, m in items:
            s = jnp.where(m, s, NEG)
            if s.shape[-1] < kmax:
                s = jnp.concatenate([s, jnp.full(s.shape[:-1] + (kmax - s.shape[-1],), NEG, f32)], axis=-1)
            padded.append(s)
        s_all = jnp.stack(padded, axis=0)
        valid = s_all > 0.5 * NEG
        e = jnp.where(valid, jnp.exp(s_all - jnp.max(s_all, axis=-1, keepdims=True)), 0.0)
        l = jnp.sum(e, axis=-1, keepdims=True)
        p = e / jnp.where(l > 0, l, 1.0)
        return [p[i][..., :s.shape[-1]] for i, (s, _) in enumerate(items)]

    def pv(p, vs):
        return jnp.stack([jnp.dot(p[g].reshape(A_HPG * t_new, -1).astype(bf16), vs[g], preferred_element_type=f32)
                          .reshape(A_HPG, t_new, DH) for g in groups], axis=0)

    def tail_bias(s):
        bias = jnp.concatenate([bl_ref[...], bn_ref[...]], axis=-1).reshape(ng, A_HPG, t_new, 2 * QB)
        nk = s.shape[-1]
        return jnp.concatenate([s[..., :nk - 2 * QB], s[..., nk - 2 * QB:] + bias], axis=-1)

    tt = lax.broadcasted_iota(jnp.int32, (t_new, pad_new), 0)
    uu = lax.broadcasted_iota(jnp.int32, (t_new, pad_new), 1)
    new_mask = uu <= tt

    ti = lax.broadcasted_iota(jnp.int32, (t_new, nchunk), 0)
    ci = lax.broadcasted_iota(jnp.int32, (t_new, nchunk), 1)
    cmask = (past + ti >= CMP_STRIDE * ci + (CMP_BLOCK - 1))[None, None]
    sc = logits([kc[g * nchunk:(g + 1) * nchunk] for g in groups]) + bc_ref[...].reshape(ng, A_HPG, t_new, nchunk)
    nw = win_ref.shape[1] // 4
    kw = [jnp.concatenate([win_ref[0, pl.ds(g, nw, stride=4), :].astype(bf16), new_rows(kvw_new_ref, g)], axis=0)
          for g in groups]
    vw = [jnp.concatenate([win_ref[0, pl.ds(2 + g, nw, stride=4), :].astype(bf16), new_rows(kvw_new_ref, 2 + g)], axis=0)
          for g in groups]
    wmask = lax.broadcasted_iota(jnp.int32, (t_new, nw), 1) >= lax.broadcasted_iota(jnp.int32, (t_new, nw), 0)
    pc, pw = softmax_many([(sc, cmask),
                           (tail_bias(logits(kw)), jnp.concatenate([wmask, new_mask], axis=1)[None, None])])
    o_c = pv(pc, [vc[g * nchunk:(g + 1) * nchunk] for g in groups])
    o_w = pv(pw, vw)

    ps = jnp.concatenate([pc[g, 0] + pc[g, 1] + pc[g, 2] + pc[g, 3] for g in groups], axis=0)
    score_t = _dot_small_int_lhs(mt_ref[...], ps, (((1,), (1,)), ((), ())))
    lane = lax.broadcasted_iota(jnp.int32, score_t.shape, 1)
    tpos = past + jnp.where(lane >= t_new, lane - t_new, lane)
    sel_t = _select_blocks(score_t, tpos, n_blocks)
    key_mask = lax.dot_general(sel_t.astype(bf16), e_ref[...], (((0,), (0,)), ((), ())),
                               preferred_element_type=f32)

    new_mask_f = jnp.where(new_mask, 1.0, 0.0)
    sel_mask = jnp.concatenate([key_mask, jnp.concatenate([new_mask_f] * ng, axis=0)], axis=1)
    sel_mask = sel_mask.reshape(ng, 1, t_new, past + pad_new) > 0.5
    ks = [jnp.concatenate([slc_pages[p][0, pl.ds(g, page, stride=4), :].astype(bf16) for p in range(n_pages)]
                          + [new_rows(kvs_new_ref, g)], axis=0) for g in groups]
    vs = [jnp.concatenate([slc_pages[p][0, pl.ds(2 + g, page, stride=4), :].astype(bf16) for p in range(n_pages)]
                          + [new_rows(kvs_new_ref, 2 + g)], axis=0) for g in groups]
    o_s = pv(softmax_many([(tail_bias(logits(ks)), sel_mask)])[0], vs)

    gate_all = gate_ref[...]
    za = za_ref[...]
    outs = []
    for g in groups:
        gate = gate_all[:, g * DH:(g + 1) * DH]
        for h in range(A_HPG):
            def gcol(br):
                return gate[:, br * A_HPG + h:br * A_HPG + h + 1]
            o = gcol(0) * o_c[g, h] + gcol(1) * o_s[g, h] + gcol(2) * o_w[g, h]
            outs.append(o * za[:, (g * A_HPG + h) * DH:(g * A_HPG + h + 1) * DH])
    o_ref[...] = jnp.concatenate(outs, axis=1).astype(o_ref.dtype)

    nrow = win_ref.shape[1]
    win_out_ref[0, 0:nrow - 4 * t_new, :] = win_ref[0, 4 * t_new:nrow, :]
    win_out_ref[0, nrow - 4 * t_new:nrow, :] = kvw_new_ref[...]


def _nsa_sample(qa, kvs_new, kvw_new, cache_cmp, cache_slc, cache_win, page_table, wk, wv, pek, pev, gk,
                gate, za, rel_bias, t_new):
    n, n_pages = page_table.shape
    page = cache_cmp.shape[1] // 4
    past = n_pages * page
    wb = cache_win.shape[1] // 4
    assert wb == WINDOW and past % QB == 0 and past >= WINDOW
    nchunk = past // CMP_STRIDE
    n_cmp = (past + t_new - CMP_BLOCK) // CMP_STRIDE + 1
    assert n_cmp < nchunk
    n_slc = -(-(past + t_new) // SLC_BLOCK)
    nb_pad = -(-n_slc // 8) * 8
    bias_c = _bias_table(rel_bias, (t_new, nchunk), past - (CMP_BLOCK - 1), 1, -CMP_STRIDE, False)
    b_last = _bias_table(rel_bias, (t_new, QB), QB, 1, -1, True)
    b_new = _bias_table(rel_bias, (t_new, QB), 0, 1, -1, True)
    mt = jnp.asarray(_cmp_to_slc_t(nchunk, nb_pad, n_cmp, n_slc), dtype=bf16)
    e = jnp.asarray((np.arange(past)[None, :] // SLC_BLOCK == np.arange(nb_pad)[:, None]).astype(np.float32), dtype=bf16)

    def page_spec(p):
        return pl.BlockSpec((1, page * 4, DH), lambda i, pt: (pt[i, p], 0, 0))

    rowblk = lambda w: pl.BlockSpec((t_new, w), lambda i, pt: (i, 0))
    full = lambda a: pl.BlockSpec(a.shape, lambda i, pt: (0,) * a.ndim)
    in_specs = ([page_spec(p) for p in range(n_pages)] * 2 + [
        rowblk(A_WIDTH),
        pl.BlockSpec((t_new * 4, DH), lambda i, pt: (i, 0)),
        pl.BlockSpec((t_new * 4, DH), lambda i, pt: (i, 0)),
        pl.BlockSpec((1, wb * 4, DH), lambda i, pt: (i, 0, 0)),
        full(wk), full(wv), full(pek), full(pev), full(gk), full(bias_c), full(b_last), full(b_new),
        rowblk(A_KV_GROUPS * DH), rowblk(A_WIDTH), full(mt), full(e)])
    grid_spec = pltpu.PrefetchScalarGridSpec(
        num_scalar_prefetch=1, grid=(n,), in_specs=in_specs,
        out_specs=[rowblk(A_WIDTH), pl.BlockSpec((1, wb * 4, DH), lambda i, pt: (i, 0, 0))],
        scratch_shapes=[pltpu.VMEM((4, nchunk * CMP_PITCH, DH), f32)])
    return pl.pallas_call(
        functools.partial(_nsa_sample_kernel, n_pages=n_pages, page=page, t_new=t_new, n_blocks=n_slc),
        grid_spec=grid_spec,
        out_shape=[jax.ShapeDtypeStruct((n * t_new, A_WIDTH), f32), jax.ShapeDtypeStruct(cache_win.shape, f32)],
        compiler_params=pltpu.CompilerParams(dimension_semantics=("arbitrary",), vmem_limit_bytes=VMEM_LIMIT),
        name="nsa_sample",
    )(page_table, *([cache_cmp] * n_pages), *([cache_slc] * n_pages), qa, kvs_new, kvw_new, cache_win,
      wk, wv, pek, pev, gk, bias_c, b_last, b_new, gate, za, mt, e)


def _outproj_kernel(x_ref, oa_ref, ob_ref, w_ref, y_ref):
    acc = jnp.dot(oa_ref[...].astype(bf16), w_ref[0:A_WIDTH, :], preferred_element_type=f32)
    acc = acc + jnp.dot(ob_ref[...].astype(bf16), w_ref[A_WIDTH:, :], preferred_element_type=f32)
    y_ref[...] = x_ref[...] + acc


def _outproj(x2d, oa, ob, w_out_bf, tm):
    m = x2d.shape[0]
    assert m % tm == 0
    row = lambda i: (i, 0)
    return pl.pallas_call(
        _outproj_kernel, grid=(m // tm,),
        in_specs=[pl.BlockSpec((tm, D_MODEL), row), pl.BlockSpec((tm, A_WIDTH), row), pl.BlockSpec((tm, B_WIDTH), row),
                  pl.BlockSpec(w_out_bf.shape, lambda i: (0, 0))],
        out_specs=pl.BlockSpec((tm, D_MODEL), row),
        out_shape=jax.ShapeDtypeStruct((m, D_MODEL), f32),
        compiler_params=pltpu.CompilerParams(dimension_semantics=("arbitrary",), vmem_limit_bytes=VMEM_LIMIT),
        name="outproj",
    )(x2d, oa, ob, w_out_bf)


def kernel(x_prompt, x_sample, cache_kv_cmp, cache_kv_slc, cache_kv_win, state_hgrn, page_table, g_norm, w_in, w_out,
           g_q, g_k_slc, g_k_win, g_k_cmp, w_cmp_k, w_cmp_v, pe_cmp_k, pe_cmp_v, rel_bias, lb_logits, g_o_hgrn):
    depth = w_in.shape[0]
    assert depth == 1, "single-layer trunk"
    nb, t, _ = x_prompt.shape
    ns, ts, _ = x_sample.shape
    row = lambda a: a.astype(f32)[None]
    lower = jnp.cumsum(jax.nn.softmax(lb_logits.astype(f32), axis=0), axis=0)[0]
    w_main, w_gate = _prep_proj_weights(w_in[0])
    w_out_bf = w_out[0].astype(bf16)
    wk, pek = _prep_cmp_weights(w_cmp_k[0], pe_cmp_k[0])
    wv, pev = _prep_cmp_weights(w_cmp_v[0], pe_cmp_v[0])
    proj_args = (row(g_norm[0]), w_main, w_gate, row(g_q[0]), row(g_k_slc[0]), row(g_k_win[0]), row(lower))
    kv6 = lambda a, n_, t_: a.reshape(1, n_, t_, 2, A_KV_GROUPS, DH)

    xp = x_prompt.reshape(nb * t, D_MODEL)
    wlen = min(WINDOW, t)
    assert wlen == PROJ_TM, "the proj row tile doubles as the prompt's final window"
    qa, kvc, kvs, kvw, gate, za, qb, f, ib, zb, win_p = _proj(xp, *proj_args, tm=PROJ_TM, t_seq=t)
    s0 = jnp.zeros((nb, B_HEADS, DH, DH), f32)
    ob, st_p = _hgrn(qb, f, ib, zb, s0, row(g_o_hgrn[0]), nb, t, tc=128, sc=16, fc=64)
    kc, vct = _compress(kvc, wk, wv, pek, pev, row(g_k_cmp[0]), nb, t)
    oa = _nsa_prompt(qa, kc, vct, kvs, kvw, gate, za, rel_bias, nb, t)
    y_p = _outproj(xp, oa, ob, w_out_bf, tm=PROJ_TM).reshape(nb, t, D_MODEL)
    win_p = kv6(win_p, nb, wlen)

    xs = x_sample.reshape(ns * ts, D_MODEL)
    qa, kvc_s, kvs_s, kvw_s, gate, za, qb, f, ib, zb, _ = _proj(xs, *proj_args, tm=PROJ_TM, t_seq=PROJ_TM)
    seq_per_step = 8
    ob, st_s = _hgrn(qb, f, ib, zb, state_hgrn[0].astype(f32), row(g_o_hgrn[0]), ns, ts, tc=seq_per_step * ts, sc=ts,
                     ns=seq_per_step)
    pool = cache_kv_cmp.shape[1]
    page = cache_kv_cmp.shape[2]
    oa, win_s = _nsa_sample(
        qa, kvs_s, kvw_s, cache_kv_cmp[0].reshape(pool, page * 4, DH), cache_kv_slc[0].reshape(pool, page * 4, DH),
        cache_kv_win[0].reshape(ns, -1, DH), page_table, wk, wv, pek, pev, row(g_k_cmp[0]), gate, za, rel_bias, ts)
    y_s = _outproj(xs, oa, ob, w_out_bf, tm=PROJ_TM).reshape(ns, ts, D_MODEL)

    return (y_p, y_s, kv6(kvc, nb, t), kv6(kvs, nb, t), win_p, st_p[None].astype(x_prompt.dtype),
            kv6(kvc_s, ns, ts), kv6(kvs_s, ns, ts), kv6(win_s, ns, WINDOW), st_s[None].astype(state_hgrn.dtype))
```

```python
import functools
import math

import jax
import jax.numpy as jnp
import numpy as np
from jax import lax
from jax.experimental import pallas as pl
from jax.experimental.pallas import tpu as pltpu

f32 = jnp.float32
bf16 = jnp.bfloat16

D_MODEL = 2048
A_HEADS = 8
A_KV_GROUPS = 2
A_HPG = A_HEADS // A_KV_GROUPS
DH = 128
A_WIDTH = A_HEADS * DH
CMP_BLOCK = 32
CMP_STRIDE = 16
SLC_BLOCK = 64
N_SELECT = 16
WINDOW = 512
B_HEADS = 8
B_WIDTH = B_HEADS * DH
REL_BUCKETS = 32
REL_MAX_DIST = 128
EPS = 1e-6
NEG = -1e30

VMEM_LIMIT = 56 * 1024 * 1024


def _sigmoid(x):
    return 1.0 / (1.0 + jnp.exp(-x))


def _head_norm(a, g):
    return a * lax.rsqrt(jnp.mean(a * a, axis=-1, keepdims=True) + EPS) * g


def _dot_small_int_lhs(m_bf, x, dims):
    hi = x.astype(bf16)
    r1 = x - hi.astype(f32)
    mid = r1.astype(bf16)
    lo = (r1 - mid.astype(f32)).astype(bf16)
    d = lambda p: lax.dot_general(m_bf, p, dims, preferred_element_type=f32)
    return d(hi) + d(mid) + d(lo)


PROJ_TN = 512
PROJ_TM = 512
PROJ_SEG = {0: "qa", 1: "qa", 2: "kvc", 3: "kvs", 4: "kvw", 5: "za", 6: "za", 7: "qb", 8: "qb", 9: "f", 10: "f",
            11: "ib", 12: "ib", 13: "zb", 14: "zb"}
PROJ_SLABS = ((0, 5), (5, 11), (11, 15))


def _proj_tiles(h, w_refs, first_tile, tm, out, par):
    for jj, w_ref in enumerate(w_refs):
        j = first_tile + jj
        acc = lax.dot_general(h, w_ref[...], (((1,), (1,)), ((), ())), preferred_element_type=f32)
        name = PROJ_SEG[j]
        k = j - min(t for t, s in PROJ_SEG.items() if s == name)
        lanes = slice(k * PROJ_TN, (k + 1) * PROJ_TN)

        def cols(c):
            return acc[:, c * DH:(c + 1) * DH]

        if name == "qa":
            gq = par["gq"][...]
            out["qa"][:, lanes] = jnp.concatenate(
                [_head_norm(cols(c), gq) * (DH ** -0.5) for c in range(4)], axis=1).astype(out["qa"].dtype)
        elif name in ("kvc", "kvs", "kvw"):
            g_ref = {"kvc": None, "kvs": par["gks"], "kvw": par["gkw"]}[name]
            for c in range(4):
                val = _head_norm(cols(c), g_ref[...]) if (g_ref is not None and c < 2) else cols(c)
                out[name][pl.ds(c, tm, stride=4), :] = val
                if name == "kvw":
                    out["tail"][pl.ds(c, tm, stride=4), :] = val
        elif name in ("za", "zb"):
            out[name][:, lanes] = acc * _sigmoid(acc)
        elif name == "f":
            lb = par["lb"][:, lanes]
            out["f"][:, lanes] = lb + (1.0 - lb) * _sigmoid(acc)
        else:
            out[name][:, lanes] = acc


def _proj_first_kernel(x_ref, gn_ref, *refs):
    first, last = PROJ_SLABS[0]
    w_refs = refs[:last - first]
    (wg_ref, gq_ref, gks_ref, gkw_ref,
     h_ref, qa_ref, kvc_ref, kvs_ref, kvw_ref, tail_ref, gate_ref) = refs[last - first:]
    x = x_ref[...]
    h = (x * lax.rsqrt(jnp.mean(x * x, axis=-1, keepdims=True) + EPS) * gn_ref[...]).astype(bf16)
    h_ref[...] = h
    gate_ref[...] = _sigmoid(lax.dot_general(h, wg_ref[...], (((1,), (1,)), ((), ())), preferred_element_type=f32))
    _proj_tiles(h, w_refs, first, x_ref.shape[0],
                dict(qa=qa_ref, kvc=kvc_ref, kvs=kvs_ref, kvw=kvw_ref, tail=tail_ref),
                dict(gq=gq_ref, gks=gks_ref, gkw=gkw_ref))


def _proj_rest_kernel(h_ref, *refs, slab):
    first, last = PROJ_SLABS[slab]
    w_refs, lb_ref, out_refs = refs[:last - first], refs[last - first], refs[last - first + 1:]
    names = list(dict.fromkeys(PROJ_SEG[j] for j in range(first, last)))
    _proj_tiles(h_ref[...], w_refs, first, h_ref.shape[0], dict(zip(names, out_refs)), dict(lb=lb_ref))


def _proj(x2d, gn, w_main, w_gate, gq, gks, gkw, lb, tm, t_seq):
    m = x2d.shape[0]
    assert m % tm == 0 and t_seq % tm == 0
    tiles_per_seq = t_seq // tm
    row = lambda i: (i, 0)
    const = lambda i: (0, 0)
    w_specs = lambda slab: [pl.BlockSpec((PROJ_TN, D_MODEL), lambda i, j=j: (j, 0)) for j in range(*PROJ_SLABS[slab])]
    w_args = lambda slab: [w_main] * (PROJ_SLABS[slab][1] - PROJ_SLABS[slab][0])
    params = pltpu.CompilerParams(dimension_semantics=("arbitrary",), vmem_limit_bytes=VMEM_LIMIT)
    kv_shape = jax.ShapeDtypeStruct((m * 4, DH), f32)
    wide = lambda dt: jax.ShapeDtypeStruct((m, A_WIDTH), dt)
    h, qa, kvc, kvs, kvw, tail, gate = pl.pallas_call(
        _proj_first_kernel, grid=(m // tm,),
        in_specs=[pl.BlockSpec((tm, D_MODEL), row), pl.BlockSpec((1, D_MODEL), const)] + w_specs(0)
        + [pl.BlockSpec(w_gate.shape, const),
           pl.BlockSpec((1, DH), const), pl.BlockSpec((1, DH), const), pl.BlockSpec((1, DH), const)],
        out_specs=[pl.BlockSpec((tm, D_MODEL), row), pl.BlockSpec((tm, A_WIDTH), row)]
        + [pl.BlockSpec((tm * 4, DH), row)] * 3
        + [pl.BlockSpec((tm * 4, DH), lambda i: (i // tiles_per_seq, 0)), pl.BlockSpec((tm, A_KV_GROUPS * DH), row)],
        out_shape=[jax.ShapeDtypeStruct((m, D_MODEL), bf16), wide(bf16), kv_shape, kv_shape, kv_shape,
                   jax.ShapeDtypeStruct((m // tiles_per_seq * 4, DH), f32),
                   jax.ShapeDtypeStruct((m, A_KV_GROUPS * DH), f32)],
        compiler_params=params, name="proj_a",
    )(x2d, gn, *w_args(0), w_gate, gq, gks, gkw)

    def rest(slab, n_out):
        return pl.pallas_call(
            functools.partial(_proj_rest_kernel, slab=slab), grid=(m // tm,),
            in_specs=[pl.BlockSpec((tm, D_MODEL), row)] + w_specs(slab) + [pl.BlockSpec((1, B_WIDTH), const)],
            out_specs=[pl.BlockSpec((tm, A_WIDTH), row)] * n_out, out_shape=[wide(f32)] * n_out,
            compiler_params=params, name="proj_" + "abc"[slab],
        )(h, *w_args(slab), lb)

    za, qb, f = rest(1, 3)
    ib, zb = rest(2, 2)
    return qa, kvc, kvs, kvw, gate, za, qb, f, ib, zb, tail


def _prep_proj_weights(w_in):
    a0 = A_WIDTH + 6 * A_KV_GROUPS * DH
    a1 = a0 + 3 * A_HEADS
    w_t = w_in.T
    n_tiles = PROJ_SLABS[-1][1]
    first_after_gate = a0 // PROJ_TN

    def cast_tile(w_ref, o_ref):
        o_ref[...] = w_ref[...].astype(bf16)

    w_main = pl.pallas_call(
        cast_tile, grid=(n_tiles,),
        in_specs=[pl.BlockSpec((pl.Element(PROJ_TN), pl.Element(D_MODEL)),
                               lambda j: (pl.multiple_of(jnp.where(j < first_after_gate, 0, a1 - a0) + j * PROJ_TN, 8), 0))],
        out_specs=pl.BlockSpec((PROJ_TN, D_MODEL), lambda j: (j, 0)),
        out_shape=jax.ShapeDtypeStruct((n_tiles * PROJ_TN, D_MODEL), bf16),
        compiler_params=pltpu.CompilerParams(dimension_semantics=("arbitrary",)),
        name="cast_w",
    )(w_t.astype(f32))
    wg = w_t[a0:a1].reshape(3, A_KV_GROUPS, A_HPG, -1).transpose(1, 0, 2, 3).reshape(A_KV_GROUPS, 3 * A_HPG, -1)
    w_gate = jnp.pad(wg, ((0, 0), (0, DH - 3 * A_HPG), (0, 0))).reshape(A_KV_GROUPS * DH, -1).astype(bf16)
    return w_main, w_gate


HGRN_SAFE_LOG_DECAY = 80.0


def _hgrn_exact(q, k, v, b, st):
    sc = q.shape[0]
    t_idx = lax.broadcasted_iota(jnp.int32, (sc, DH), 0)
    o = lax.dot_general((q * jnp.exp(b)).astype(bf16), st.astype(bf16), (((1,), (1,)), ((), ())),
                        preferred_element_type=f32)
    for s in range(sc):
        e = jnp.exp(jnp.minimum(b - b[s:s + 1, :], 0.0))
        a = jnp.where(t_idx >= s, q * e * k[s:s + 1, :], 0.0)
        o = o + jnp.sum(a, axis=-1, keepdims=True) * v[s:s + 1, :]
    bl = b[sc - 1:sc, :]
    ut = lax.dot_general(v.astype(bf16), (k * jnp.exp(bl - b)).astype(bf16), (((0,), (0,)), ((), ())),
                         preferred_element_type=f32)
    return o, st * jnp.exp(bl) + ut


def _hgrn_factored(q, k, v, b, st):
    fc = q.shape[0]
    qd = (q * jnp.exp(b)).astype(bf16)
    att = lax.dot_general(qd, (k * jnp.exp(-b)).astype(bf16), (((1,), (1,)), ((), ())), preferred_element_type=f32)
    tri = lax.broadcasted_iota(jnp.int32, (fc, fc), 0) >= lax.broadcasted_iota(jnp.int32, (fc, fc), 1)
    att = jnp.where(tri, att, 0.0)
    o = jnp.dot(att.astype(bf16), v.astype(bf16), preferred_element_type=f32)
    o = o + lax.dot_general(qd, st.astype(bf16), (((1,), (1,)), ((), ())), preferred_element_type=f32)
    bl = b[fc - 1:fc, :]
    ut = lax.dot_general(v.astype(bf16), (k * jnp.exp(bl - b)).astype(bf16), (((0,), (0,)), ((), ())),
                         preferred_element_type=f32)
    return o, st * jnp.exp(bl) + ut


def _hgrn_kernel(q_ref, f_ref, v_ref, zs_ref, s0_ref, go_ref, tri_ref, o_ref, sout_ref, st_scr, *, sc, fc):
    c = pl.program_id(1)
    tc = q_ref.shape[0]
    ns = st_scr.shape[0]
    t_seq = tc // ns

    @pl.when(c == 0)
    def _():
        for j in range(ns):
            for h in range(B_HEADS):
                st_scr[j, h] = s0_ref[j, h].T

    f = f_ref[...]
    b_all = _dot_small_int_lhs(tri_ref[...], jnp.log(f), (((1,), (0,)), ((), ())))
    go = go_ref[...]

    def finish(rows, lanes, o):
        o_ref[rows, lanes] = (_head_norm(o, go) * zs_ref[rows, lanes]).astype(o_ref.dtype)

    def exact_rows(r0, n_rows, rebase):
        for h in range(B_HEADS):
            lanes = slice(h * DH, (h + 1) * DH)
            outs = []
            for i in range(n_rows // sc):
                rows = slice(r0 + i * sc, r0 + (i + 1) * sc)
                b = b_all[rows, lanes]
                if rebase and i > 0:
                    b = b - b_all[r0 + i * sc - 1:r0 + i * sc, lanes]
                j = (r0 + i * sc) // t_seq
                o, st = _hgrn_exact(q_ref[rows, lanes], 1.0 - f[rows, lanes], v_ref[rows, lanes], b, st_scr[j, h])
                st_scr[j, h] = st
                outs.append(o)
            finish(slice(r0, r0 + n_rows), lanes, jnp.concatenate(outs, axis=0) if len(outs) > 1 else outs[0])

    if fc is None:
        exact_rows(0, tc, False)
    else:
        total = jnp.max(-jnp.concatenate([b_all[r0 + fc - 1:r0 + fc, :] for r0 in range(0, tc, fc)], axis=0))

        @pl.when(total < HGRN_SAFE_LOG_DECAY)
        def _():
            for h in range(B_HEADS):
                lanes = slice(h * DH, (h + 1) * DH)
                for r0 in range(0, tc, fc):
                    rows = slice(r0, r0 + fc)
                    o, st = _hgrn_factored(q_ref[rows, lanes], 1.0 - f[rows, lanes], v_ref[rows, lanes],
                                           b_all[rows, lanes], st_scr[r0 // t_seq, h])
                    st_scr[r0 // t_seq, h] = st
                    finish(rows, lanes, o)

        @pl.when(jnp.logical_not(total < HGRN_SAFE_LOG_DECAY))
        def _():
            for r0 in range(0, tc, fc):
                exact_rows(r0, fc, True)

    @pl.when(c == pl.num_programs(1) - 1)
    def _():
        for j in range(ns):
            for h in range(B_HEADS):
                sout_ref[j, h] = st_scr[j, h].T


def _block_tri(tc, blk):
    r = np.arange(tc)
    tri = (r[:, None] // blk == r[None, :] // blk) & (r[None, :] <= r[:, None])
    return jnp.asarray(tri.astype(np.float32), dtype=bf16)


def _hgrn(qb, f, ib, zs, s0, go, n, t, tc, sc, fc=None, ns=1):
    assert tc % sc == 0 and (fc is None or (tc % fc == 0 and fc % sc == 0))
    assert (ns == 1 and t % tc == 0) or (tc == ns * t and n % ns == 0 and fc is None and t % sc == 0)
    nc = max(t // tc, 1)
    rows = lambda i, c: (i * nc + c, 0)
    st = lambda i, c: (i, 0, 0, 0)
    const = lambda i, c: (0, 0)
    return pl.pallas_call(
        functools.partial(_hgrn_kernel, sc=sc, fc=fc),
        grid=(n // ns, nc),
        in_specs=[pl.BlockSpec((tc, B_WIDTH), rows)] * 4 + [
            pl.BlockSpec((ns, B_HEADS, DH, DH), st),
            pl.BlockSpec((1, DH), const),
            pl.BlockSpec((tc, tc), const),
        ],
        out_specs=[pl.BlockSpec((tc, B_WIDTH), rows), pl.BlockSpec((ns, B_HEADS, DH, DH), st)],
        out_shape=[jax.ShapeDtypeStruct((n * t, B_WIDTH), bf16), jax.ShapeDtypeStruct((n, B_HEADS, DH, DH), f32)],
        scratch_shapes=[pltpu.VMEM((ns, B_HEADS, DH, DH), f32)],
        compiler_params=pltpu.CompilerParams(
            dimension_semantics=("arbitrary", "arbitrary"), vmem_limit_bytes=VMEM_LIMIT),
        name="hgrn",
    )(qb, f, ib, zs, s0, go, _block_tri(tc, sc if fc is None else fc))


def _rel_bucket_np(dist):
    n = np.maximum(dist, 0)
    exact = REL_BUCKETS // 2
    scale = np.float32((REL_BUCKETS - exact) / math.log(REL_MAX_DIST / exact))
    large = exact + (np.log(np.maximum(n, exact).astype(np.float32) / np.float32(exact)) * scale).astype(np.int32)
    return np.where(n < exact, n, np.minimum(large, REL_BUCKETS - 1)).astype(np.int32)


def _bucket_starts():
    buckets = _rel_bucket_np(np.arange(2 * REL_MAX_DIST))
    assert np.all(np.diff(buckets) >= 0) and buckets[-1] == REL_BUCKETS - 1
    return [int(np.argmax(buckets >= b)) for b in range(REL_BUCKETS)]


def _bias_table_kernel(rb_ref, o_ref, *, off, a0, a1, sub_far):
    h = pl.program_id(0)
    shape = o_ref.shape[1:]
    d = off + a0 * lax.broadcasted_iota(jnp.int32, shape, 0) + a1 * lax.broadcasted_iota(jnp.int32, shape, 1)
    starts = _bucket_starts()
    far = rb_ref[REL_BUCKETS - 1, h]
    v = jnp.full(shape, far, f32)
    for b in range(REL_BUCKETS - 2, -1, -1):
        v = jnp.where(d < starts[b + 1], rb_ref[b, h], v)
    o_ref[0] = v - far if sub_far else v


def _bias_table(rel_bias, shape, off, a0, a1, sub_far):
    return pl.pallas_call(
        functools.partial(_bias_table_kernel, off=off, a0=a0, a1=a1, sub_far=sub_far),
        grid=(A_HEADS,),
        in_specs=[pl.BlockSpec(memory_space=pltpu.SMEM)],
        out_specs=pl.BlockSpec((1,) + tuple(shape), lambda h: (h, 0, 0)),
        out_shape=jax.ShapeDtypeStruct((A_HEADS,) + tuple(shape), f32),
        compiler_params=pltpu.CompilerParams(dimension_semantics=("arbitrary",)),
        name="bias_table",
    )(rel_bias.astype(f32))


def _cmp_bias_t(rel_bias, t0, nt, n_chunk):
    return _bias_table(rel_bias, (n_chunk, nt), t0 - (CMP_BLOCK - 1), -CMP_STRIDE, 1, False)


def _near_tiles_t(rel_bias):
    return _bias_table(rel_bias, (QB, 2 * QB), 0, -1, 1, True)


def _compress_combine(lhs, wcat, pe2):
    out = jnp.dot(lhs, wcat, preferred_element_type=f32)
    pc = jnp.dot(pe2, wcat, preferred_element_type=f32)
    const = pc[0:1, :DH] + pc[1:2, DH:]
    r = lhs.shape[0]
    return out[:, :DH] + pltpu.roll(out[:, DH:], r - 1, 0) + const


def _compress_kernel(kv_ref, wk_ref, wv_ref, pek_ref, pev_ref, gk_ref, kc_ref, vct_ref):
    nchunk = kv_ref.shape[0] // (4 * CMP_STRIDE)

    def chunk_rows(cg):
        return jnp.concatenate(
            [kv_ref[pl.ds(4 * l + cg, nchunk, stride=4 * CMP_STRIDE), :].astype(bf16) for l in range(CMP_STRIDE)],
            axis=1)

    kc = _compress_combine(jnp.concatenate([chunk_rows(0), chunk_rows(1)], axis=0), wk_ref[...], pek_ref[...])
    vc = _compress_combine(jnp.concatenate([chunk_rows(2), chunk_rows(3)], axis=0), wv_ref[...], pev_ref[...])
    kc = _head_norm(kc, gk_ref[...])
    for g in range(A_KV_GROUPS):
        kc_ref[0, g] = kc[g * nchunk:(g + 1) * nchunk].astype(kc_ref.dtype)
        vct_ref[0, g] = vc[g * nchunk:(g + 1) * nchunk].T.astype(vct_ref.dtype)


def _prep_cmp_weights(w, pe):
    half = CMP_STRIDE * DH
    wcat = jnp.concatenate([w[:half], w[half:]], axis=1).astype(bf16)
    return wcat, pe.reshape(2, half).astype(bf16)


def _compress(kvc, wk, wv, pek, pev, gk, n, t):
    nchunk = t // CMP_STRIDE
    assert nchunk == DH, "the transposed v_c block is square"
    const = lambda i: (0, 0)
    return pl.pallas_call(
        _compress_kernel, grid=(n,),
        in_specs=[pl.BlockSpec((t * 4, DH), lambda i: (i, 0)),
                  pl.BlockSpec(wk.shape, const), pl.BlockSpec(wv.shape, const),
                  pl.BlockSpec(pek.shape, const), pl.BlockSpec(pev.shape, const), pl.BlockSpec((1, DH), const)],
        out_specs=[pl.BlockSpec((1, A_KV_GROUPS, nchunk, DH), lambda i: (i, 0, 0, 0))] * 2,
        out_shape=[jax.ShapeDtypeStruct((n, A_KV_GROUPS, nchunk, DH), bf16)] * 2,
        compiler_params=pltpu.CompilerParams(dimension_semantics=("arbitrary",), vmem_limit_bytes=VMEM_LIMIT),
        name="compress",
    )(kvc, wk, wv, pek, pev, gk)


def _softmax_tile(carry, s, mask, v_bf):
    m, l, acc = carry
    s = jnp.where(mask, s, NEG)
    m_new = jnp.maximum(m, jnp.max(s, axis=-1, keepdims=True))
    alpha = jnp.exp(m - m_new)
    p = jnp.where(mask, jnp.exp(s - m_new), 0.0)
    l = alpha * l + jnp.sum(p, axis=-1, keepdims=True)
    h, r, k = p.shape
    pv = jnp.dot(p.reshape(h * r, k).astype(bf16), v_bf, preferred_element_type=f32).reshape(h, r, DH)
    return m_new, l, alpha * acc + pv


def _softmax_finish(carry):
    m, l, acc = carry
    return acc / jnp.where(l > 0, l, 1.0)


def _select_blocks(score_t, tpos, n_blocks):
    nb = score_t.shape[0]
    j = lax.broadcasted_iota(jnp.int32, score_t.shape, 0)
    valid = (j * SLC_BLOCK <= tpos) & (j < n_blocks)
    cur = tpos >> 6
    forced = (j == 0) | (j == cur) | (j == cur - 1)
    val = jnp.where(valid, jnp.where(forced, 1e30, score_t), -1.0)
    rank = jnp.zeros(score_t.shape, jnp.int32)
    for i in range(n_blocks):
        vi = val[i:i + 1, :]
        beats = (vi > val) | ((vi == val) & (i < j))
        rank = rank + beats.astype(jnp.int32)
    return jnp.where((rank < N_SELECT) & valid, 1.0, 0.0)


QB = 128
NSA_PROMPT_SEQS = 2


NEG_M = -1e30
NEG_S = -2e30


def _nsa_prompt_kernel(q_ref, kc_ref, vct_ref, bc_ref, kvs_ref, kvw_ref, gate_ref, za_ref, bt_ref, mt_ref,
                       o_ref, ks_scr, vts_scr, kw_scr, vtw_scr, km_scr):
    qb = pl.program_id(1)
    q0 = qb * QB
    n_far = jnp.maximum(qb - 1, 0)
    nkt = vts_scr.shape[1]
    nq4 = A_HPG * QB
    nseq = q_ref.shape[0]
    chains = [(s, g) for s in range(nseq) for g in range(A_KV_GROUPS)]
    groups = range(len(chains))

    @pl.when(qb == 0)
    def _():
        for c, (s, g) in enumerate(chains):
            for kt in range(nkt):
                rows = slice(kt * QB, (kt + 1) * QB)
                base = kt * 4 * QB
                ks_scr[c, rows, :] = kvs_ref[s, pl.ds(base + g, QB, stride=4), :].astype(bf16)
                vts_scr[c, kt] = kvs_ref[s, pl.ds(base + 2 + g, QB, stride=4), :].T.astype(bf16)
                kw_scr[c, rows, :] = kvw_ref[s, pl.ds(base + g, QB, stride=4), :].astype(bf16)
                vtw_scr[c, kt] = kvw_ref[s, pl.ds(base + 2 + g, QB, stride=4), :].T.astype(bf16)

    qts = [jnp.concatenate([q_ref[s, :, (g * A_HPG + h) * DH:(g * A_HPG + h + 1) * DH].astype(f32).T
                            for h in range(A_HPG)], axis=1).astype(bf16) for s, g in chains]
    kk = lax.broadcasted_iota(jnp.int32, (QB, nq4), 0)
    tt = lax.broadcasted_iota(jnp.int32, (QB, nq4), 1) & (QB - 1)

    def heads(ref, g, *idx):
        return jnp.concatenate([ref[(g * A_HPG + h,) + idx] for h in range(A_HPG)], axis=1)

    o_c = []
    for c, (s, g) in enumerate(chains):
        sc = jnp.dot(kc_ref[s, g], qts[c], preferred_element_type=f32) + heads(bc_ref, g)
        mask_c = q0 + tt >= CMP_STRIDE * kk + (CMP_BLOCK - 1)
        sc = jnp.where(mask_c, sc, NEG)
        mc = jnp.max(sc, axis=0, keepdims=True)
        ec = jnp.where(mask_c, jnp.exp(sc - mc), 0.0)
        lc = jnp.sum(ec, axis=0, keepdims=True)
        pc = ec / jnp.where(lc > 0, lc, 1.0)
        o_c.append(jnp.dot(vct_ref[s, g], pc.astype(bf16), preferred_element_type=f32))
        ps = pc[:, 0:QB] + pc[:, QB:2 * QB] + pc[:, 2 * QB:3 * QB] + pc[:, 3 * QB:4 * QB]
        score_t = _dot_small_int_lhs(mt_ref[...], ps, (((1,), (0,)), ((), ())))
        nb = score_t.shape[0]
        sel_t = _select_blocks(score_t, q0 + lax.broadcasted_iota(jnp.int32, (nb, QB), 1), nb)
        for j in range(nb):
            km_scr[c, j * SLC_BLOCK:(j + 1) * SLC_BLOCK, :] = jnp.broadcast_to(sel_t[j:j + 1, :], (SLC_BLOCK, QB))

    def tile(carry, qt, k_tile, vt_tile, mask, bias):
        m, l, acc = carry
        s = jnp.dot(k_tile, qt, preferred_element_type=f32)
        if bias is not None:
            s = s + bias
        s = jnp.where(mask, s, NEG_S)
        m_new = jnp.maximum(m, jnp.max(s, axis=0, keepdims=True))
        alpha = jnp.exp(m - m_new)
        p = jnp.exp(s - m_new)
        l = alpha * l + jnp.sum(p, axis=0, keepdims=True)
        return m_new, l, alpha * acc + jnp.dot(vt_tile, p.astype(bf16), preferred_element_type=f32)

    def init():
        return jnp.full((1, nq4), NEG_M, f32), jnp.zeros((1, nq4), f32), jnp.zeros((DH, nq4), f32)

    def finish(carry):
        m, l, acc = carry
        return acc / jnp.where(l > 0, l, 1.0)

    def key_rows(kt):
        return pl.ds(pl.multiple_of(kt * QB, QB), QB)

    def lanes4(x):
        return jnp.concatenate([x] * A_HPG, axis=1)

    def near_span(g, k_scr, vt_scr, rs, use_sel):
        ks, vts, masks, biases = [], [], [], []
        for r in rs:
            kt = jnp.maximum(qb - r, 0)
            ks.append(k_scr[g, key_rows(kt), :])
            vts.append(vt_scr[g, kt])
            if use_sel:
                mask = lanes4(km_scr[g, key_rows(kt), :]) > 0.5
                if r == 0:
                    mask = mask & (kk <= tt)
            else:
                mask = (kk <= tt) if r == 0 else (kk >= tt) if r == WINDOW // QB else (kk >= 0)
            masks.append(mask)
            off = jnp.where(qb - r >= 0, 0.0, NEG_S)
            biases.append(heads(bt_ref, chains[g][1], slice(None), slice(r * QB, (r + 1) * QB)) + off if r < 2
                          else jnp.full((QB, nq4), off, f32))
        cat = lambda xs, axis: jnp.concatenate(xs, axis=axis)
        return cat(ks, 0), cat(vts, 1), cat(masks, 0), cat(biases, 0)

    o_w = [finish(tile(init(), qts[g], *near_span(g, kw_scr, vtw_scr, (4, 3, 2, 1, 0), False))) for g in groups]

    kk2 = lax.broadcasted_iota(jnp.int32, (2 * QB, nq4), 0)

    def far(i, carries):
        rows = pl.ds(pl.multiple_of(i * 2 * QB, 2 * QB), 2 * QB)
        in_range = (kk2 + i * 2 * QB) < n_far * QB
        out = []
        for g in groups:
            mask = (lanes4(km_scr[g, rows, :]) > 0.5) & in_range
            vt = jnp.concatenate([vts_scr[g, 2 * i], vts_scr[g, jnp.minimum(2 * i + 1, nkt - 1)]], axis=1)
            out.append(tile(carries[g], qts[g], ks_scr[g, rows, :], vt, mask, None))
        return tuple(out)

    carries = lax.fori_loop(0, (n_far + 1) // 2, far, tuple(init() for g in groups))
    o_s = [finish(tile(carries[g], qts[g], *near_span(g, ks_scr, vts_scr, (1, 0), True))) for g in groups]

    for s in range(nseq):
        za = za_ref[s]
        outs = []
        for g in range(A_KV_GROUPS):
            c = s * A_KV_GROUPS + g
            gate_t = gate_ref[s, :, g * DH:(g + 1) * DH].T
            for h in range(A_HPG):
                cols = slice(h * QB, (h + 1) * QB)

                def grow(br):
                    return gate_t[br * A_HPG + h:br * A_HPG + h + 1, :]
                o = grow(0) * o_c[c][:, cols] + grow(1) * o_s[c][:, cols] + grow(2) * o_w[c][:, cols]
                outs.append(o.T * za[:, (g * A_HPG + h) * DH:(g * A_HPG + h + 1) * DH])
        o_ref[s] = jnp.concatenate(outs, axis=1).astype(o_ref.dtype)


def _cmp_to_slc_t(n_cmp_pad, n_slc_pad, n_cmp, n_slc):
    c0 = np.arange(n_cmp_pad)[None, :] * CMP_STRIDE
    s0 = np.arange(n_slc_pad)[:, None] * SLC_BLOCK
    ov = np.minimum(c0 + CMP_BLOCK, s0 + SLC_BLOCK) - np.maximum(c0, s0)
    m = np.maximum(ov, 0).astype(np.float32) / CMP_STRIDE
    m[:, n_cmp:] = 0
    m[n_slc:, :] = 0
    return m


def _nsa_prompt(qa, kc, vct, kvs, kvw, gate, za, rel_bias, n, t):
    nq = t // QB
    n_cmp = (t - CMP_BLOCK) // CMP_STRIDE + 1
    n_slc = -(-t // SLC_BLOCK)
    assert kc.shape[2] == QB and n_slc % 8 == 0
    bias_c = _cmp_bias_t(rel_bias, 0, t, QB)
    bt = _near_tiles_t(rel_bias)
    mt = jnp.asarray(_cmp_to_slc_t(QB, n_slc, n_cmp, n_slc), dtype=bf16)
    nseq = NSA_PROMPT_SEQS
    assert n % nseq == 0
    ng = A_KV_GROUPS
    nchain = nseq * ng
    qrow = lambda w: pl.BlockSpec((nseq, QB, w), lambda i, b: (i, b, 0))
    seq4 = lambda shape: pl.BlockSpec((nseq,) + shape, lambda i, b: (i, 0, 0, 0))
    seq3 = lambda shape: pl.BlockSpec((nseq,) + shape, lambda i, b: (i, 0, 0), pipeline_mode=pl.Buffered(1))
    by_seq = lambda a, w: a.reshape(n, -1, w)
    out = pl.pallas_call(
        _nsa_prompt_kernel, grid=(n // nseq, nq),
        in_specs=[
            qrow(A_WIDTH),
            seq4((ng, QB, DH)),
            seq4((ng, DH, QB)),
            pl.BlockSpec((A_HEADS, QB, QB), lambda i, b: (0, 0, b)),
            seq3((t * 4, DH)),
            seq3((t * 4, DH)),
            qrow(ng * DH),
            qrow(A_WIDTH),
            pl.BlockSpec((A_HEADS, QB, 2 * QB), lambda i, b: (0, 0, 0)),
            pl.BlockSpec(mt.shape, lambda i, b: (0, 0)),
        ],
        out_specs=qrow(A_WIDTH),
        out_shape=jax.ShapeDtypeStruct((n, t, A_WIDTH), bf16),
        scratch_shapes=[pltpu.VMEM((nchain, t, DH), bf16), pltpu.VMEM((nchain, nq, DH, QB), bf16),
                        pltpu.VMEM((nchain, t, DH), bf16), pltpu.VMEM((nchain, nq, DH, QB), bf16),
                        pltpu.VMEM((nchain, t, QB), f32)],
        compiler_params=pltpu.CompilerParams(
            dimension_semantics=("arbitrary", "arbitrary"), vmem_limit_bytes=VMEM_LIMIT),
        name="nsa_prompt",
    )(by_seq(qa, A_WIDTH), kc, vct, bias_c, by_seq(kvs, DH), by_seq(kvw, DH), by_seq(gate, ng * DH),
      by_seq(za, A_WIDTH), bt, mt)
    return out.reshape(n * t, A_WIDTH)


CMP_PITCH = 24


def _nsa_sample_kernel(pt_ref, *refs, n_pages, page, t_new, n_blocks):
    del pt_ref
    cmp_pages = refs[:n_pages]
    slc_pages = refs[n_pages:2 * n_pages]
    (q_ref, kvs_new_ref, kvw_new_ref, win_ref, wk_ref, wv_ref, pek_ref, pev_ref, gk_ref, bc_ref, bl_ref, bn_ref,
     gate_ref, za_ref, mt_ref, e_ref, o_ref, win_out_ref, xc_scr) = refs[2 * n_pages:]
    past = n_pages * page
    nchunk = past // CMP_STRIDE
    pad_new = QB
    ng = A_KV_GROUPS
    groups = range(ng)

    cpp = page // CMP_STRIDE
    for p in range(n_pages):
        for cg in range(4):
            x = cmp_pages[p][0, pl.ds(cg, page, stride=4), :]
            for c in range(cpp):
                r0 = (p * cpp + c) * CMP_PITCH
                xc_scr[cg, r0:r0 + CMP_STRIDE, :] = x[c * CMP_STRIDE:(c + 1) * CMP_STRIDE]

    def chunk_rows(cg):
        return jnp.concatenate(
            [xc_scr[cg, pl.ds(l, nchunk, stride=CMP_PITCH), :].astype(bf16) for l in range(CMP_STRIDE)], axis=1)

    kc = _compress_combine(jnp.concatenate([chunk_rows(0), chunk_rows(1)], axis=0), wk_ref[...], pek_ref[...])
    vc = _compress_combine(jnp.concatenate([chunk_rows(2), chunk_rows(3)], axis=0), wv_ref[...], pev_ref[...])
    kc = _head_norm(kc, gk_ref[...]).astype(bf16)
    vc = vc.astype(bf16)

    qall = q_ref[...]
    qs = [jnp.concatenate([qall[:, (g * A_HPG + h) * DH:(g * A_HPG + h + 1) * DH] for h in range(A_HPG)], axis=0)
          for g in groups]
    zeros_pad = jnp.zeros((pad_new - t_new, DH), bf16)

    def new_rows(ref, c):
        return jnp.concatenate([ref[pl.ds(c, t_new, stride=4), :].astype(bf16), zeros_pad], axis=0)

    def logits(ks):
        return jnp.stack([lax.dot_general(qs[g], ks[g], (((1,), (1,)), ((), ())), preferred_element_type=f32)
                          .reshape(A_HPG, t_new, ks[g].shape[0]) for g in groups], axis=0)

    def softmax_many(items):
        kmax = max(s.shape[-1] for s, _ in items)
        padded = []
        for s, m in items:
            s = jnp.where(m, s, NEG)
            if s.shape[-1] < kmax:
                s = jnp.concatenate([s, jnp.full(s.shape[:-1] + (kmax - s.shape[-1],), NEG, f32)], axis=-1)
            padded.append(s)
        s_all = jnp.stack(padded, axis=0)
        valid = s_all > 0.5 * NEG
        e = jnp.where(valid, jnp.exp(s_all - jnp.max(s_all, axis=-1, keepdims=True)), 0.0)
        l = jnp.sum(e, axis=-1, keepdims=True)
        p = e / jnp.where(l > 0, l, 1.0)
        return [p[i][..., :s.shape[-1]] for i, (s, _) in enumerate(items)]

    def pv(p, vs):
        return jnp.stack([jnp.dot(p[g].reshape(A_HPG * t_new, -1).astype(bf16), vs[g], preferred_element_type=f32)
                          .reshape(A_HPG, t_new, DH) for g in groups], axis=0)

    def tail_bias(s):
        bias = jnp.concatenate([bl_ref[...], bn_ref[...]], axis=-1).reshape(ng, A_HPG, t_new, 2 * QB)
        nk = s.shape[-1]
        return jnp.concatenate([s[..., :nk - 2 * QB], s[..., nk - 2 * QB:] + bias], axis=-1)

    tt = lax.broadcasted_iota(jnp.int32, (t_new, pad_new), 0)
    uu = lax.broadcasted_iota(jnp.int32, (t_new, pad_new), 1)
    new_mask = uu <= tt

    ti = lax.broadcasted_iota(jnp.int32, (t_new, nchunk), 0)
    ci = lax.broadcasted_iota(jnp.int32, (t_new, nchunk), 1)
    cmask = (past + ti >= CMP_STRIDE * ci + (CMP_BLOCK - 1))[None, None]
    sc = logits([kc[g * nchunk:(g + 1) * nchunk] for g in groups]) + bc_ref[...].reshape(ng, A_HPG, t_new, nchunk)
    nw = win_ref.shape[1] // 4
    kw = [jnp.concatenate([win_ref[0, pl.ds(g, nw, stride=4), :].astype(bf16), new_rows(kvw_new_ref, g)], axis=0)
          for g in groups]
    vw = [jnp.concatenate([win_ref[0, pl.ds(2 + g, nw, stride=4), :].astype(bf16), new_rows(kvw_new_ref, 2 + g)], axis=0)
          for g in groups]
    wmask = lax.broadcasted_iota(jnp.int32, (t_new, nw), 1) >= lax.broadcasted_iota(jnp.int32, (t_new, nw), 0)
    pc, pw = softmax_many([(sc, cmask),
                           (tail_bias(logits(kw)), jnp.concatenate([wmask, new_mask], axis=1)[None, None])])
    o_c = pv(pc, [vc[g * nchunk:(g + 1) * nchunk] for g in groups])
    o_w = pv(pw, vw)

    ps = jnp.concatenate([pc[g, 0] + pc[g, 1] + pc[g, 2] + pc[g, 3] for g in groups], axis=0)
    score_t = _dot_small_int_lhs(mt_ref[...], ps, (((1,), (1,)), ((), ())))
    lane = lax.broadcasted_iota(jnp.int32, score_t.shape, 1)
    tpos = past + jnp.where(lane >= t_new, lane - t_new, lane)
    sel_t = _select_blocks(score_t, tpos, n_blocks)
    key_mask = lax.dot_general(sel_t.astype(bf16), e_ref[...], (((0,), (0,)), ((), ())),
                               preferred_element_type=f32)

    new_mask_f = jnp.where(new_mask, 1.0, 0.0)
    sel_mask = jnp.concatenate([key_mask, jnp.concatenate([new_mask_f] * ng, axis=0)], axis=1)
    sel_mask = sel_mask.reshape(ng, 1, t_new, past + pad_new) > 0.5
    ks = [jnp.concatenate([slc_pages[p][0, pl.ds(g, page, stride=4), :].astype(bf16) for p in range(n_pages)]
                          + [new_rows(kvs_new_ref, g)], axis=0) for g in groups]
    vs = [jnp.concatenate([slc_pages[p][0, pl.ds(2 + g, page, stride=4), :].astype(bf16) for p in range(n_pages)]
                          + [new_rows(kvs_new_ref, 2 + g)], axis=0) for g in groups]
    o_s = pv(softmax_many([(tail_bias(logits(ks)), sel_mask)])[0], vs)

    gate_all = gate_ref[...]
    za = za_ref[...]
    outs = []
    for g in groups:
        gate = gate_all[:, g * DH:(g + 1) * DH]
        for h in range(A_HPG):
            def gcol(br):
                return gate[:, br * A_HPG + h:br * A_HPG + h + 1]
            o = gcol(0) * o_c[g, h] + gcol(1) * o_s[g, h] + gcol(2) * o_w[g, h]
            outs.append(o * za[:, (g * A_HPG + h) * DH:(g * A_HPG + h + 1) * DH])
    o_ref[...] = jnp.concatenate(outs, axis=1).astype(o_ref.dtype)

    nrow = win_ref.shape[1]
    win_out_ref[0, 0:nrow - 4 * t_new, :] = win_ref[0, 4 * t_new:nrow, :]
    win_out_ref[0, nrow - 4 * t_new:nrow, :] = kvw_new_ref[...]


def _nsa_sample(qa, kvs_new, kvw_new, cache_cmp, cache_slc, cache_win, page_table, wk, wv, pek, pev, gk,
                gate, za, rel_bias, t_new):
    n, n_pages = page_table.shape
    page = cache_cmp.shape[1] // 4
    past = n_pages * page
    wb = cache_win.shape[1] // 4
    assert wb == WINDOW and past % QB == 0 and past >= WINDOW
    nchunk = past // CMP_STRIDE
    n_cmp = (past + t_new - CMP_BLOCK) // CMP_STRIDE + 1
    assert n_cmp < nchunk
    n_slc = -(-(past + t_new) // SLC_BLOCK)
    nb_pad = -(-n_slc // 8) * 8
    bias_c = _bias_table(rel_bias, (t_new, nchunk), past - (CMP_BLOCK - 1), 1, -CMP_STRIDE, False)
    b_last = _bias_table(rel_bias, (t_new, QB), QB, 1, -1, True)
    b_new = _bias_table(rel_bias, (t_new, QB), 0, 1, -1, True)
    mt = jnp.asarray(_cmp_to_slc_t(nchunk, nb_pad, n_cmp, n_slc), dtype=bf16)
    e = jnp.asarray((np.arange(past)[None, :] // SLC_BLOCK == np.arange(nb_pad)[:, None]).astype(np.float32), dtype=bf16)

    def page_spec(p):
        return pl.BlockSpec((1, page * 4, DH), lambda i, pt: (pt[i, p], 0, 0))

    rowblk = lambda w: pl.BlockSpec((t_new, w), lambda i, pt: (i, 0))
    full = lambda a: pl.BlockSpec(a.shape, lambda i, pt: (0,) * a.ndim)
    in_specs = ([page_spec(p) for p in range(n_pages)] * 2 + [
        rowblk(A_WIDTH),
        pl.BlockSpec((t_new * 4, DH), lambda i, pt: (i, 0)),
        pl.BlockSpec((t_new * 4, DH), lambda i, pt: (i, 0)),
        pl.BlockSpec((1, wb * 4, DH), lambda i, pt: (i, 0, 0)),
        full(wk), full(wv), full(pek), full(pev), full(gk), full(bias_c), full(b_last), full(b_new),
        rowblk(A_KV_GROUPS * DH), rowblk(A_WIDTH), full(mt), full(e)])
    grid_spec = pltpu.PrefetchScalarGridSpec(
        num_scalar_prefetch=1, grid=(n,), in_specs=in_specs,
        out_specs=[rowblk(A_WIDTH), pl.BlockSpec((1, wb * 4, DH), lambda i, pt: (i, 0, 0))],
        scratch_shapes=[pltpu.VMEM((4, nchunk * CMP_PITCH, DH), f32)])
    return pl.pallas_call(
        functools.partial(_nsa_sample_kernel, n_pages=n_pages, page=page, t_new=t_new, n_blocks=n_slc),
        grid_spec=grid_spec,
        out_shape=[jax.ShapeDtypeStruct((n * t_new, A_WIDTH), f32), jax.ShapeDtypeStruct(cache_win.shape, f32)],
        compiler_params=pltpu.CompilerParams(dimension_semantics=("arbitrary",), vmem_limit_bytes=VMEM_LIMIT),
        name="nsa_sample",
    )(page_table, *([cache_cmp] * n_pages), *([cache_slc] * n_pages), qa, kvs_new, kvw_new, cache_win,
      wk, wv, pek, pev, gk, bias_c, b_last, b_new, gate, za, mt, e)


def _outproj_kernel(x_ref, oa_ref, ob_ref, w_ref, y_ref, w_scr):
    @pl.when(pl.program_id(0) == 0)
    def _():
        w_scr[...] = w_ref[...].astype(bf16)

    acc = jnp.dot(oa_ref[...].astype(bf16), w_scr[0:A_WIDTH, :], preferred_element_type=f32)
    acc = acc + jnp.dot(ob_ref[...].astype(bf16), w_scr[A_WIDTH:, :], preferred_element_type=f32)
    y_ref[...] = x_ref[...] + acc


def _outproj(x2d, oa, ob, w_out, tm):
    m = x2d.shape[0]
    assert m % tm == 0
    row = lambda i: (i, 0)
    return pl.pallas_call(
        _outproj_kernel, grid=(m // tm,),
        in_specs=[pl.BlockSpec((tm, D_MODEL), row), pl.BlockSpec((tm, A_WIDTH), row), pl.BlockSpec((tm, B_WIDTH), row),
                  pl.BlockSpec(w_out.shape, lambda i: (0, 0), pipeline_mode=pl.Buffered(1))],
        out_specs=pl.BlockSpec((tm, D_MODEL), row),
        out_shape=jax.ShapeDtypeStruct((m, D_MODEL), f32),
        scratch_shapes=[pltpu.VMEM(w_out.shape, bf16)],
        compiler_params=pltpu.CompilerParams(dimension_semantics=("arbitrary",), vmem_limit_bytes=VMEM_LIMIT),
        name="outproj",
    )(x2d, oa, ob, w_out)


def kernel(x_prompt, x_sample, cache_kv_cmp, cache_kv_slc, cache_kv_win, state_hgrn, page_table, g_norm, w_in, w_out,
           g_q, g_k_slc, g_k_win, g_k_cmp, w_cmp_k, w_cmp_v, pe_cmp_k, pe_cmp_v, rel_bias, lb_logits, g_o_hgrn):
    depth = w_in.shape[0]
    assert depth == 1, "single-layer trunk"
    nb, t, _ = x_prompt.shape
    ns, ts, _ = x_sample.shape
    row = lambda a: a.astype(f32)[None]
    lower = jnp.cumsum(jax.nn.softmax(lb_logits.astype(f32), axis=0), axis=0)[0]
    w_main, w_gate = _prep_proj_weights(w_in[0])
    w_out_bf = w_out[0].astype(f32)
    wk, pek = _prep_cmp_weights(w_cmp_k[0], pe_cmp_k[0])
    wv, pev = _prep_cmp_weights(w_cmp_v[0], pe_cmp_v[0])
    proj_args = (row(g_norm[0]), w_main, w_gate, row(g_q[0]), row(g_k_slc[0]), row(g_k_win[0]), row(lower))
    kv6 = lambda a, n_, t_: a.reshape(1, n_, t_, 2, A_KV_GROUPS, DH)

    xp = x_prompt.reshape(nb * t, D_MODEL)
    wlen = min(WINDOW, t)
    assert wlen == PROJ_TM, "the proj row tile doubles as the prompt's final window"
    qa, kvc, kvs, kvw, gate, za, qb, f, ib, zb, win_p = _proj(xp, *proj_args, tm=PROJ_TM, t_seq=t)
    s0 = jnp.zeros((nb, B_HEADS, DH, DH), f32)
    ob, st_p = _hgrn(qb, f, ib, zb, s0, row(g_o_hgrn[0]), nb, t, tc=128, sc=16, fc=64)
    kc, vct = _compress(kvc, wk, wv, pek, pev, row(g_k_cmp[0]), nb, t)
    oa = _nsa_prompt(qa, kc, vct, kvs, kvw, gate, za, rel_bias, nb, t)
    y_p = _outproj(xp, oa, ob, w_out_bf, tm=PROJ_TM).reshape(nb, t, D_MODEL)
    win_p = kv6(win_p, nb, wlen)

    xs = x_sample.reshape(ns * ts, D_MODEL)
    qa, kvc_s, kvs_s, kvw_s, gate, za, qb, f, ib, zb, _ = _proj(xs, *proj_args, tm=PROJ_TM, t_seq=PROJ_TM)
    seq_per_step = 8
    ob, st_s = _hgrn(qb, f, ib, zb, state_hgrn[0].astype(f32), row(g_o_hgrn[0]), ns, ts, tc=seq_per_step * ts, sc=ts,
                     ns=seq_per_step)
    pool = cache_kv_cmp.shape[1]
    page = cache_kv_cmp.shape[2]
    oa, win_s = _nsa_sample(
        qa, kvs_s, kvw_s, cache_kv_cmp[0].reshape(pool, page * 4, DH), cache_kv_slc[0].reshape(pool, page * 4, DH),
        cache_kv_win[0].reshape(ns, -1, DH), page_table, wk, wv, pek, pev, row(g_k_cmp[0]), gate, za, rel_bias, ts)
    y_s = _outproj(xs, oa, ob, w_out_bf, tm=PROJ_TM).reshape(ns, ts, D_MODEL)

    return (y_p, y_s, kv6(kvc, nb, t), kv6(kvs, nb, t), win_p, st_p[None].astype(x_prompt.dtype),
            kv6(kvc_s, ns, ts), kv6(kvs_s, ns, ts), kv6(win_s, ns, WINDOW), st_s[None].astype(state_hgrn.dtype))
```
